```python
import math
import jax
import jax.numpy as jnp
from jax import lax
import numpy as np

D_MODEL = 2048
BATCH = 2
SEQ = 4096
DEPTH = 1

CHUNK = 64
Q_BLOCK = 128
PLE_DIM = 256
RMS_EPS = 1e-6

DA_HEADS = 8
DA_HEAD_DIM = 128
DA_V_DIM = 2 * DA_HEAD_DIM
DA_QK_WIDTH = DA_HEADS * 2 * DA_HEAD_DIM
DA_V_WIDTH = DA_HEADS * DA_V_DIM

GLA_HEADS = 4
GLA_KEY_WIDTH = D_MODEL // 2
GLA_VAL_WIDTH = D_MODEL
GLA_DK = GLA_KEY_WIDTH // GLA_HEADS
GLA_DV = GLA_VAL_WIDTH // GLA_HEADS
GLA_GATE_RANK = 16
GLA_TAU = 16.0

N_EXPERTS = 32
TOP_K = 4
D_EXPERT = D_MODEL
SWIGLU_LIMIT = 7.0
SWIGLU_ALPHA = 1.702
MOE_BLOCK = 128

SPLIT_SIZES = (DA_QK_WIDTH, DA_QK_WIDTH, DA_V_WIDTH, GLA_KEY_WIDTH, GLA_KEY_WIDTH, GLA_VAL_WIDTH, GLA_VAL_WIDTH, GLA_GATE_RANK)
IN_WIDTH = DA_QK_WIDTH * 2 + DA_V_WIDTH + GLA_KEY_WIDTH * 2 + GLA_VAL_WIDTH * 2 + GLA_GATE_RANK

kernel_name = "hybrid_diffattn_gla_moe_block"


def _split_points(sizes):
    pts, acc = [], 0
    for s in sizes[:-1]:
        acc += s
        pts.append(acc)
    return pts


def rmsnorm(x, g, eps=RMS_EPS):
    xf = x.astype(jnp.float32)
    y = xf * lax.rsqrt(jnp.mean(xf * xf, axis=-1, keepdims=True) + eps)
    return (y * g.astype(jnp.float32)).astype(x.dtype)


def diff_attention(q, k, v, g_q, g_k, lam, subln_g, lambda_init):
    bsz, s_len = q.shape[0], q.shape[1]
    q = rmsnorm(q, g_q) * (DA_HEAD_DIM ** -0.5)
    k = rmsnorm(k, g_k)
    slopes = jnp.exp2(-8.0 * jnp.arange(1, DA_HEADS + 1, dtype=jnp.float32) / DA_HEADS)
    pos = jnp.arange(s_len)
    key_chunk = pos // CHUNK
    n_blocks = s_len // Q_BLOCK
    qb = q.reshape(bsz, n_blocks, Q_BLOCK, DA_HEADS, 2, DA_HEAD_DIM).transpose(1, 0, 2, 3, 4, 5)
    starts = jnp.arange(n_blocks) * Q_BLOCK

    def block(args):
        q_blk, start = args
        t = start + jnp.arange(Q_BLOCK)
        sc = jnp.einsum('bqhcd,bkhcd->bhcqk', q_blk, k, preferred_element_type=jnp.float32)
        dist = jnp.abs(t[:, None] - pos[None, :]).astype(jnp.float32)
        bias = -slopes[:, None, None] * dist
        allowed = key_chunk[None, :] <= (t // CHUNK)[:, None]
        sc = jnp.where(allowed, sc + bias[None, :, None], -jnp.inf)
        a = jax.nn.softmax(sc, axis=-1)
        a = a[:, :, 0] - lam * a[:, :, 1]
        return jnp.einsum('bhqk,bkhe->bqhe', a.astype(v.dtype), v, preferred_element_type=jnp.float32)

    o = lax.map(block, (qb, starts))
    o = o.transpose(1, 0, 2, 3, 4).reshape(bsz, s_len, DA_HEADS, DA_V_DIM)
    o = rmsnorm(o, subln_g) * (1.0 - lambda_init)
    return o.reshape(bsz, s_len, DA_V_WIDTH).astype(v.dtype)


def gla(q, k, v, g, log_a, out_g):
    bsz, s_len = q.shape[0], q.shape[1]
    nc = s_len // CHUNK

    def to_chunks(t):
        return t.reshape(bsz, nc, CHUNK, GLA_HEADS, t.shape[-1]).transpose(1, 0, 3, 2, 4)

    qc = to_chunks(q * (GLA_DK ** -0.5))
    kc = to_chunks(k)
    vc = to_chunks(v)
    bc = jnp.cumsum(to_chunks(log_a.astype(jnp.float32)), axis=3)

    def step(state, inp):
        q_, k_, v_, b_ = inp
        decay = jnp.exp(-jnp.abs(b_[:, :, :, None, :] - b_[:, :, None, :, :]))
        attn = jnp.einsum('bhtk,bhsk,bhtsk->bhts', q_, k_, decay)
        o_intra = jnp.einsum('bhts,bhsv->bhtv', attn, v_)
        o_inter = jnp.einsum('bhtk,bhkv->bhtv', q_ * jnp.exp(b_), state)
        b_last = b_[:, :, -1:, :]
        state = state * jnp.exp(b_last)[:, :, 0, :, None] + jnp.einsum(
            'bhsk,bhsv->bhkv', k_ * jnp.exp(b_last - b_), v_)
        return state, (o_intra + o_inter).astype(jnp.float32)

    state0 = jnp.zeros((bsz, GLA_HEADS, GLA_DK, GLA_DV), jnp.float32)
    _, o = lax.scan(step, state0, (qc, kc, vc, bc))
    o = o.transpose(1, 0, 3, 2, 4).reshape(bsz, s_len, GLA_HEADS, GLA_DV)
    o = rmsnorm(o, out_g) * jax.nn.silu(g.astype(jnp.float32))
    return o.reshape(bsz, s_len, GLA_VAL_WIDTH).astype(v.dtype)


def moe(h, router_w, router_b, w_gu, b_gu, w_dn, b_dn):
    n_tok, dim = h.shape
    logits = jnp.matmul(h, router_w, preferred_element_type=jnp.float32) + router_b.astype(jnp.float32)
    top_val, top_idx = lax.top_k(logits, TOP_K)
    gates = jax.nn.softmax(top_val, axis=-1)
    n_pairs = n_tok * TOP_K
    flat_e = top_idx.reshape(-1)
    flat_tok = jnp.arange(n_pairs, dtype=jnp.int32) // TOP_K
    order = jnp.argsort(flat_e)
    sorted_e = flat_e[order]
    counts = jnp.bincount(flat_e, length=N_EXPERTS)
    padded = (counts + MOE_BLOCK - 1) // MOE_BLOCK * MOE_BLOCK
    start = jnp.cumsum(counts) - counts
    pad_end = jnp.cumsum(padded)
    pad_start = pad_end - padded
    rank = jnp.arange(n_pairs) - start[sorted_e]
    dest_sorted = (pad_start[sorted_e] + rank).astype(jnp.int32)
    n_blocks = -(-n_pairs // MOE_BLOCK) + N_EXPERTS
    n_rows = n_blocks * MOE_BLOCK
    row_tok = jnp.full((n_rows,), n_tok, jnp.int32).at[dest_sorted].set(flat_tok[order])
    block_start = jnp.arange(n_blocks) * MOE_BLOCK
    block_expert = jnp.minimum(jnp.searchsorted(pad_end, block_start, side='right'), N_EXPERTS - 1)
    h_pad = jnp.concatenate([h, jnp.zeros((1, dim), h.dtype)], axis=0)
    xs = h_pad[row_tok].reshape(n_blocks, MOE_BLOCK, dim)

    def expert_block(args):
        xb, e = args
        gu = xb @ w_gu[e] + b_gu[e]
        gate = jnp.minimum(gu[..., 0::2], SWIGLU_LIMIT)
        up = jnp.clip(gu[..., 1::2], -SWIGLU_LIMIT, SWIGLU_LIMIT)
        act = (up + 1.0) * gate * jax.nn.sigmoid(SWIGLU_ALPHA * gate)
        return act @ w_dn[e] + b_dn[e]

    ys = lax.map(expert_block, (xs, block_expert)).reshape(n_rows, dim)
    pair_row = jnp.zeros((n_pairs,), jnp.int32).at[order].set(dest_sorted)
    y_pairs = ys[pair_row].reshape(n_tok, TOP_K, dim)
    return jnp.einsum('tk,tkd->td', gates.astype(h.dtype), y_pairs)


def setup_inputs(seed: int = 0) -> dict:
    key = jax.random.key(seed)
    ks = iter(jax.random.split(key, 40))

    def nrm(shape, scale):
        return jax.random.normal(next(ks), shape, jnp.float32) * scale

    def gain(shape):
        return 1.0 + nrm(shape, 0.02)

    L, D = DEPTH, D_MODEL
    return {
        "x": nrm((BATCH, SEQ, D), 1.0),
        "p": nrm((DEPTH, BATCH, SEQ, PLE_DIM), 1.0),
        "w_in": nrm((L, D, IN_WIDTH), D ** -0.5),
        "da_q_norm": gain((L, DA_HEAD_DIM)),
        "da_k_norm": gain((L, DA_HEAD_DIM)),
        "da_lambda_q1": nrm((L, DA_HEAD_DIM), 0.1),
        "da_lambda_k1": nrm((L, DA_HEAD_DIM), 0.1),
        "da_lambda_q2": nrm((L, DA_HEAD_DIM), 0.1),
        "da_lambda_k2": nrm((L, DA_HEAD_DIM), 0.1),
        "da_subln": gain((L, DA_V_DIM)),
        "gla_gate_w2": nrm((L, GLA_GATE_RANK, GLA_KEY_WIDTH), GLA_GATE_RANK ** -0.5),
        "gla_gate_b": nrm((L, GLA_KEY_WIDTH), 0.1),
        "gla_out_norm": gain((L, GLA_DV)),
        "w_branch_da": nrm((L, DA_V_WIDTH, D), DA_V_WIDTH ** -0.5),
        "w_branch_gla": nrm((L, GLA_VAL_WIDTH, D), GLA_VAL_WIDTH ** -0.5),
        "w_merge_gate": nrm((L, D, 2 * D), D ** -0.5),
        "b_merge_gate": nrm((L, 2 * D), 0.02),
        "w_out": nrm((L, D, D), D ** -0.5),
        "norm_mix": gain((L, D)),
        "norm_ffn": gain((L, D)),
        "router_w": nrm((L, D, N_EXPERTS), D ** -0.5),
        "router_b": nrm((L, N_EXPERTS), 0.01),
        "w_gate_up": nrm((L, N_EXPERTS, D, 2 * D_EXPERT), D ** -0.5),
        "b_gate_up": nrm((L, N_EXPERTS, 2 * D_EXPERT), 0.02),
        "w_down": nrm((L, N_EXPERTS, D_EXPERT, D), D_EXPERT ** -0.5),
        "b_down": nrm((L, N_EXPERTS, D), 0.02),
        "norm_ple": gain((L, D)),
        "w_ple_gate": nrm((L, D, D), D ** -0.5),
        "w_ple_up": nrm((L, PLE_DIM, D), PLE_DIM ** -0.5),
        "norm_ple_post": gain((L, D)),
    }


def reference(x, p, w_in, da_q_norm, da_k_norm, da_lambda_q1, da_lambda_k1, da_lambda_q2, da_lambda_k2,
              da_subln, gla_gate_w2, gla_gate_b, gla_out_norm, w_branch_da, w_branch_gla, w_merge_gate,
              b_merge_gate, w_out, norm_mix, norm_ffn, router_w, router_b, w_gate_up, b_gate_up, w_down,
              b_down, norm_ple, w_ple_gate, w_ple_up, norm_ple_post):
    bsz, s_len, dim = x.shape
    split_pts = _split_points(SPLIT_SIZES)
    for i in range(DEPTH):
        h = rmsnorm(x, norm_mix[i])
        proj = h @ w_in[i]
        dq, dk, dv, gq, gk, gv, gg, glr = jnp.split(proj, split_pts, axis=-1)

        lambda_init = 0.8 - 0.6 * math.exp(-0.3 * i)
        lam = (jnp.exp(jnp.sum(da_lambda_q1[i].astype(jnp.float32) * da_lambda_k1[i].astype(jnp.float32)))
               - jnp.exp(jnp.sum(da_lambda_q2[i].astype(jnp.float32) * da_lambda_k2[i].astype(jnp.float32)))
               + lambda_init)
        y_da = diff_attention(
            dq.reshape(bsz, s_len, DA_HEADS, 2, DA_HEAD_DIM),
            dk.reshape(bsz, s_len, DA_HEADS, 2, DA_HEAD_DIM),
            dv.reshape(bsz, s_len, DA_HEADS, DA_V_DIM),
            da_q_norm[i], da_k_norm[i], lam, da_subln[i], lambda_init)

        log_a = jax.nn.log_sigmoid(jnp.matmul(glr, gla_gate_w2[i], preferred_element_type=jnp.float32)
                                   + gla_gate_b[i].astype(jnp.float32)) / GLA_TAU
        y_gla = gla(
            gq.reshape(bsz, s_len, GLA_HEADS, GLA_DK),
            gk.reshape(bsz, s_len, GLA_HEADS, GLA_DK),
            gv.reshape(bsz, s_len, GLA_HEADS, GLA_DV),
            gg.reshape(bsz, s_len, GLA_HEADS, GLA_DV),
            log_a.reshape(bsz, s_len, GLA_HEADS, GLA_DK),
            gla_out_norm[i])

        gates = jax.nn.sigmoid(h @ w_merge_gate[i] + b_merge_gate[i])
        g_da, g_gla = jnp.split(gates, 2, axis=-1)
        mixed = g_da * (y_da @ w_branch_da[i]) + g_gla * (y_gla @ w_branch_gla[i])
        x = x + mixed @ w_out[i]

        h = rmsnorm(x, norm_ffn[i])
        x = x + moe(h.reshape(bsz * s_len, dim), router_w[i], router_b[i], w_gate_up[i], b_gate_up[i],
                    w_down[i], b_down[i]).reshape(bsz, s_len, dim)

        h = rmsnorm(x, norm_ple[i])
        ple = rmsnorm(p[i] @ w_ple_up[i], norm_ple_post[i]) * jax.nn.sigmoid(h @ w_ple_gate[i])
        x = x + ple
    return x
```

```python
import functools
import math

import jax
import jax.numpy as jnp
from jax import lax
from jax.experimental import pallas as pl
from jax.experimental.pallas import tpu as pltpu

F32 = jnp.float32
BF16 = jnp.bfloat16
HIGHEST = lax.Precision.HIGHEST

CHUNK = 64
CHUNK_SHIFT = CHUNK.bit_length() - 1
RMS_EPS = 1e-6
DA_HEADS = 8
DA_HEAD_DIM = 128
DA_V_DIM = 2 * DA_HEAD_DIM
GLA_HEADS = 4
GLA_GATE_RANK = 16
GLA_TAU = 16.0
N_EXPERTS = 32
TOP_K = 4
SWIGLU_LIMIT = 7.0
SWIGLU_ALPHA = 1.702

LANES = 128
NEG_BIG = -1e30

MIB = 1024 * 1024


def _cparams(sem, vmem_mib):
    return pltpu.CompilerParams(dimension_semantics=sem, vmem_limit_bytes=vmem_mib * MIB)


def _dot(a, b):
    return jnp.dot(a, b, preferred_element_type=F32)


def _dot_nt(a, b):
    return lax.dot_general(a, b, (((1,), (1,)), ((), ())), preferred_element_type=F32)


def _dot_tn(a, b):
    return lax.dot_general(a, b, (((0,), (0,)), ((), ())), preferred_element_type=F32)


def _rmsnorm_kernel(x_ref, g_ref, o_ref):
    x = x_ref[...]
    ms = jnp.mean(x * x, axis=-1, keepdims=True)
    o_ref[...] = (x * lax.rsqrt(ms + RMS_EPS) * g_ref[...]).astype(o_ref.dtype)


def _rmsnorm(x2d, gain, tm=512):
    t, d = x2d.shape
    return pl.pallas_call(
        _rmsnorm_kernel,
        grid=(t // tm,),
        in_specs=[pl.BlockSpec((tm, d), lambda i: (i, 0)), pl.BlockSpec((1, d), lambda i: (0, 0))],
        out_specs=pl.BlockSpec((tm, d), lambda i: (i, 0)),
        out_shape=jax.ShapeDtypeStruct((t, d), BF16),
        compiler_params=_cparams(("parallel",), 32),
        name="rmsnorm",
    )(x2d, gain.reshape(1, d).astype(F32))


def _proj_kernel(a_ref, w_ref, g_ref, o_ref, *, group_norm, scale):
    acc = _dot(a_ref[...], w_ref[...])
    if group_norm:
        for c in range(acc.shape[1] // DA_HEAD_DIM):
            blk = acc[:, c * DA_HEAD_DIM:(c + 1) * DA_HEAD_DIM]
            ms = jnp.mean(blk * blk, axis=-1, keepdims=True)
            y = blk * lax.rsqrt(ms + RMS_EPS) * g_ref[...] * scale
            o_ref[:, c * DA_HEAD_DIM:(c + 1) * DA_HEAD_DIM] = y.astype(o_ref.dtype)
    else:
        o_ref[...] = acc.astype(o_ref.dtype)


def _proj(a, w, *, out_dtype, tm, tn, gain=None, scale=1.0):
    t, k = a.shape
    n = w.shape[1]
    group_norm = gain is not None
    g = (gain if group_norm else jnp.ones((DA_HEAD_DIM,), F32)).reshape(1, DA_HEAD_DIM).astype(F32)
    return pl.pallas_call(
        functools.partial(_proj_kernel, group_norm=group_norm, scale=scale),
        grid=(n // tn, t // tm),
        in_specs=[
            pl.BlockSpec((tm, k), lambda j, i: (i, 0)),
            pl.BlockSpec((k, tn), lambda j, i: (0, j)),
            pl.BlockSpec((1, DA_HEAD_DIM), lambda j, i: (0, 0)),
        ],
        out_specs=pl.BlockSpec((tm, tn), lambda j, i: (i, j)),
        out_shape=jax.ShapeDtypeStruct((t, n), out_dtype),
        compiler_params=_cparams(("parallel", "parallel"), 48),
        name="proj",
    )(a, w, g)


def _da_kernel(slope_ref, q_ref, k_ref, v_ref, lamv_ref, subg_ref, o_ref,
               m_ref, l_ref, acc_ref, *, tq, lambda_init):
    h = pl.program_id(1)
    i = pl.program_id(2)
    slope = slope_ref[h]
    t0 = i * tq
    hd = DA_HEAD_DIM

    m_ref[...] = jnp.full(m_ref.shape, NEG_BIG, F32)
    l_ref[...] = jnp.zeros(l_ref.shape, F32)
    acc_ref[...] = jnp.zeros(acc_ref.shape, F32)

    def update(c, s, v):
        m_old = m_ref[c]
        m_new = jnp.maximum(m_old, jnp.max(s, axis=-1, keepdims=True))
        alpha = jnp.exp(m_old - m_new)
        p = jnp.exp(s - m_new)
        l_ref[c] = alpha * l_ref[c] + jnp.sum(p, axis=-1, keepdims=True)
        acc_ref[c] = alpha * acc_ref[c] + _dot(p.astype(BF16), v)
        m_ref[c] = m_new

    def past_tile(j, carry):
        r0 = pl.multiple_of(j * tq, tq)
        col = r0 + lax.broadcasted_iota(jnp.int32, (1, tq), 1)
        cb = slope * (col - t0).astype(F32)
        v = v_ref[pl.ds(r0, tq), :]
        for c in range(2):
            s = _dot_nt(q_ref[:, c * hd:(c + 1) * hd], k_ref[pl.ds(r0, tq), c * hd:(c + 1) * hd]) + cb
            update(c, s, v)
        return carry

    lax.fori_loop(0, i, past_tile, 0)

    r0 = pl.multiple_of(i * tq, tq)
    row = lax.broadcasted_iota(jnp.int32, (tq, tq), 0)
    col = lax.broadcasted_iota(jnp.int32, (tq, tq), 1)
    bias = slope * (row - jnp.abs(row - col)).astype(F32)
    allowed = (col >> CHUNK_SHIFT) <= (row >> CHUNK_SHIFT)
    v = v_ref[pl.ds(r0, tq), :]
    for c in range(2):
        s = _dot_nt(q_ref[:, c * hd:(c + 1) * hd], k_ref[pl.ds(r0, tq), c * hd:(c + 1) * hd])
        s = jnp.where(allowed, s + bias, NEG_BIG)
        update(c, s, v)

    lamv = lamv_ref[...]
    lam = (jnp.exp(jnp.sum(lamv[0:1] * lamv[1:2], axis=-1, keepdims=True))
           - jnp.exp(jnp.sum(lamv[2:3] * lamv[3:4], axis=-1, keepdims=True)) + lambda_init)
    o = acc_ref[0] / l_ref[0] - lam * (acc_ref[1] / l_ref[1])
    ms = jnp.mean(o * o, axis=-1, keepdims=True)
    y = o * lax.rsqrt(ms + RMS_EPS) * subg_ref[...] * (1.0 - lambda_init)
    o_ref[...] = y.astype(o_ref.dtype)


def _diff_attention(qn, kn, v_src, v_col0, lam_vecs, subln_g, bsz, s_len, lambda_init, tq=256):
    t = bsz * s_len
    nq = s_len // tq
    vb = v_col0 // DA_V_DIM
    slopes = jnp.exp2(-8.0 * jnp.arange(1, DA_HEADS + 1, dtype=F32) / DA_HEADS)
    grid_spec = pltpu.PrefetchScalarGridSpec(
        num_scalar_prefetch=1,
        grid=(bsz, DA_HEADS, nq),
        in_specs=[
            pl.BlockSpec((tq, DA_V_DIM), lambda b, h, i, s: (b * nq + i, h)),
            pl.BlockSpec((s_len, DA_V_DIM), lambda b, h, i, s: (b, h)),
            pl.BlockSpec((s_len, DA_V_DIM), lambda b, h, i, s: (b, vb + h)),
            pl.BlockSpec((4, DA_HEAD_DIM), lambda b, h, i, s: (0, 0)),
            pl.BlockSpec((1, DA_V_DIM), lambda b, h, i, s: (0, 0)),
        ],
        out_specs=pl.BlockSpec((tq, DA_V_DIM), lambda b, h, i, s: (b * nq + i, h)),
        scratch_shapes=[
            pltpu.VMEM((2, tq, 1), F32),
            pltpu.VMEM((2, tq, 1), F32),
            pltpu.VMEM((2, tq, DA_V_DIM), F32),
        ],
    )
    return pl.pallas_call(
        functools.partial(_da_kernel, tq=tq, lambda_init=lambda_init),
        grid_spec=grid_spec,
        out_shape=jax.ShapeDtypeStruct((t, DA_HEADS * DA_V_DIM), BF16),
        compiler_params=_cparams(("parallel", "parallel", "arbitrary"), 32),
        name="diff_attention",
    )(slopes, qn, kn, v_src, lam_vecs.astype(F32), subln_g.reshape(1, DA_V_DIM).astype(F32))


def _gla_kernel(q_ref, k_ref, v_ref, g_ref, glr_ref, w2_ref, gb_ref, og_ref, o_ref, state_ref,
                *, tb, dk, dv):
    @pl.when(pl.program_id(2) == 0)
    def _():
        state_ref[...] = jnp.zeros(state_ref.shape, F32)

    row = lax.broadcasted_iota(jnp.int32, (CHUNK, CHUNK), 0)
    col = lax.broadcasted_iota(jnp.int32, (CHUNK, CHUNK), 1)
    lower = row >= col
    tri = lower.astype(F32)
    mid = CHUNK // 2

    def chunk(c, carry):
        r0 = pl.multiple_of(c * CHUNK, CHUNK)
        z = jnp.dot(glr_ref[pl.ds(r0, CHUNK), :], w2_ref[...], precision=HIGHEST,
                    preferred_element_type=F32) + gb_ref[...]
        log_a = (jnp.minimum(z, 0.0) - jnp.log1p(jnp.exp(-jnp.abs(z)))) * (1.0 / GLA_TAU)
        b = jnp.dot(tri, log_a, precision=HIGHEST, preferred_element_type=F32)
        b_last = b[CHUNK - 1:CHUNK, :]
        b_mid = b[mid:mid + 1, :]
        q = q_ref[pl.ds(r0, CHUNK), :].astype(F32) * (dk ** -0.5)
        k = k_ref[pl.ds(r0, CHUNK), :].astype(F32)
        v = v_ref[pl.ds(r0, CHUNK), :]
        e_fwd = jnp.exp(b - b_mid)
        e_bwd = jnp.exp(b_mid - b)
        a_lo = _dot_nt((q * e_fwd).astype(BF16), (k * e_bwd).astype(BF16))
        a_up = _dot_nt((q * e_bwd).astype(BF16), (k * e_fwd).astype(BF16))
        attn = jnp.where(lower, a_lo, a_up)
        state = state_ref[...]
        o = _dot(attn.astype(BF16), v) + _dot_nt((q * jnp.exp(b)).astype(BF16), state.astype(BF16))
        kd = (k * jnp.exp(b_last - b)).astype(BF16)
        state_ref[...] = state * jnp.exp(b_last) + _dot_tn(v, kd)
        ms = jnp.mean(o * o, axis=-1, keepdims=True)
        g = g_ref[pl.ds(r0, CHUNK), :].astype(F32)
        y = o * lax.rsqrt(ms + RMS_EPS) * og_ref[...] * (g * jax.nn.sigmoid(g))
        o_ref[pl.ds(r0, CHUNK), :] = y.astype(o_ref.dtype)
        return carry

    lax.fori_loop(0, tb // CHUNK, chunk, 0)


def _gla(src, cols, glr, w2p, gate_b, out_g, bsz, s_len, tb=512):
    t = bsz * s_len
    nb = s_len // tb
    kw = w2p.shape[1]
    dk = kw // GLA_HEADS
    vw = out_g.shape[0] * GLA_HEADS
    dv = vw // GLA_HEADS
    cq, ck, cv, cg = cols
    row_map = lambda b, h, i: (b * nb + i)
    return pl.pallas_call(
        functools.partial(_gla_kernel, tb=tb, dk=dk, dv=dv),
        grid=(bsz, GLA_HEADS, nb),
        in_specs=[
            pl.BlockSpec((tb, dk), lambda b, h, i: (row_map(b, h, i), cq // dk + h)),
            pl.BlockSpec((tb, dk), lambda b, h, i: (row_map(b, h, i), ck // dk + h)),
            pl.BlockSpec((tb, dv), lambda b, h, i: (row_map(b, h, i), cv // dv + h)),
            pl.BlockSpec((tb, dv), lambda b, h, i: (row_map(b, h, i), cg // dv + h)),
            pl.BlockSpec((tb, LANES), lambda b, h, i: (row_map(b, h, i), 0)),
            pl.BlockSpec((LANES, dk), lambda b, h, i: (0, h)),
            pl.BlockSpec((1, dk), lambda b, h, i: (0, h)),
            pl.BlockSpec((1, dv), lambda b, h, i: (0, 0)),
        ],
        out_specs=pl.BlockSpec((tb, dv), lambda b, h, i: (row_map(b, h, i), h)),
        out_shape=jax.ShapeDtypeStruct((t, vw), BF16),
        scratch_shapes=[pltpu.VMEM((dv, dk), F32)],
        compiler_params=_cparams(("parallel", "parallel", "arbitrary"), 32),
        name="gla",
    )(src, src, src, src, glr, w2p, gate_b.reshape(1, kw).astype(F32), out_g.reshape(1, dv).astype(F32))


def _merge_kernel(h_ref, ya_ref, yb_ref, wga_ref, wgb_ref, ba_ref, bb_ref, wa_ref, wb_ref, o_ref):
    h = h_ref[...]
    ga = jax.nn.sigmoid(_dot(h, wga_ref[...]) + ba_ref[...])
    gb = jax.nn.sigmoid(_dot(h, wgb_ref[...]) + bb_ref[...])
    mixed = ga * _dot(ya_ref[...], wa_ref[...]) + gb * _dot(yb_ref[...], wb_ref[...])
    o_ref[...] = mixed.astype(o_ref.dtype)


def _merge(h, y_da, y_gla, w_gate, b_gate, w_da, w_gla, tm=512, tn=512):
    t, d = h.shape
    nb = d // tn
    act = lambda: pl.BlockSpec((tm, d), lambda j, i: (i, 0))
    return pl.pallas_call(
        _merge_kernel,
        grid=(nb, t // tm),
        in_specs=[
            act(), act(), act(),
            pl.BlockSpec((d, tn), lambda j, i: (0, j)),
            pl.BlockSpec((d, tn), lambda j, i: (0, nb + j)),
            pl.BlockSpec((1, tn), lambda j, i: (0, j)),
            pl.BlockSpec((1, tn), lambda j, i: (0, nb + j)),
            pl.BlockSpec((d, tn), lambda j, i: (0, j)),
            pl.BlockSpec((d, tn), lambda j, i: (0, j)),
        ],
        out_specs=pl.BlockSpec((tm, tn), lambda j, i: (i, j)),
        out_shape=jax.ShapeDtypeStruct((t, d), BF16),
        compiler_params=_cparams(("parallel", "parallel"), 48),
        name="merge",
    )(h, y_da, y_gla, w_gate, w_gate, b_gate, b_gate, w_da, w_gla)


def _out_proj_kernel(x_ref, m_ref, w_ref, o_ref):
    o_ref[...] = x_ref[...] + _dot(m_ref[...], w_ref[...])


def _out_proj(x2d, mixed, w_out, tm=256):
    t, d = x2d.shape
    return pl.pallas_call(
        _out_proj_kernel,
        grid=(t // tm,),
        in_specs=[
            pl.BlockSpec((tm, d), lambda i: (i, 0)),
            pl.BlockSpec((tm, d), lambda i: (i, 0)),
            pl.BlockSpec((d, d), lambda i: (0, 0)),
        ],
        out_specs=pl.BlockSpec((tm, d), lambda i: (i, 0)),
        out_shape=jax.ShapeDtypeStruct((t, d), F32),
        compiler_params=_cparams(("parallel",), 48),
        name="out_proj",
    )(x2d, mixed, w_out)


def _router_kernel(x_ref, g_ref, rw_ref, rb_ref, hp_ref, idx_ref, gate_ref):
    x = x_ref[...]
    ms = jnp.mean(x * x, axis=-1, keepdims=True)
    h = x * lax.rsqrt(ms + RMS_EPS) * g_ref[...]
    half = h.shape[1] // 2
    lo = lax.bitcast_convert_type(h[:, :half].astype(BF16).astype(F32), jnp.uint32)
    hi = lax.bitcast_convert_type(h[:, half:].astype(BF16).astype(F32), jnp.uint32)
    hp_ref[...] = (lo >> 16) | (hi & jnp.uint32(0xFFFF0000))

    logits = jnp.dot(h, rw_ref[...], precision=HIGHEST, preferred_element_type=F32) + rb_ref[...]
    lane = lax.broadcasted_iota(jnp.int32, logits.shape, 1)
    vals, idxs = [], []
    for _ in range(TOP_K):
        m = jnp.max(logits, axis=-1, keepdims=True)
        idx = jnp.min(jnp.where(logits == m, lane, LANES), axis=-1, keepdims=True)
        vals.append(m)
        idxs.append(idx)
        logits = jnp.where(lane == idx, -jnp.inf, logits)
    exps = [jnp.exp(v - vals[0]) for v in vals]
    denom = exps[0] + exps[1] + exps[2] + exps[3]
    idx_out = jnp.zeros(lane.shape, jnp.int32)
    gate_out = jnp.zeros(lane.shape, F32)
    for k in range(TOP_K):
        idx_out = jnp.where(lane == k, idxs[k], idx_out)
        gate_out = jnp.where(lane == k, exps[k] / denom, gate_out)
    idx_ref[...] = idx_out
    gate_ref[...] = gate_out


def _router(x2d, gain, router_w, router_b, tm=256):
    t, d = x2d.shape
    e = router_w.shape[1]
    rw = jnp.zeros((d, LANES), F32).at[:, :e].set(router_w.astype(F32))
    rb = jnp.full((1, LANES), NEG_BIG, F32).at[0, :e].set(router_b.astype(F32))
    return pl.pallas_call(
        _router_kernel,
        grid=(t // tm,),
        in_specs=[
            pl.BlockSpec((tm, d), lambda i: (i, 0)),
            pl.BlockSpec((1, d), lambda i: (0, 0)),
            pl.BlockSpec((d, LANES), lambda i: (0, 0)),
            pl.BlockSpec((1, LANES), lambda i: (0, 0)),
        ],
        out_specs=[
            pl.BlockSpec((tm, d // 2), lambda i: (i, 0)),
            pl.BlockSpec((tm, LANES), lambda i: (i, 0)),
            pl.BlockSpec((tm, LANES), lambda i: (i, 0)),
        ],
        out_shape=[
            jax.ShapeDtypeStruct((t, d // 2), jnp.uint32),
            jax.ShapeDtypeStruct((t, LANES), jnp.int32),
            jax.ShapeDtypeStruct((t, LANES), F32),
        ],
        compiler_params=_cparams(("parallel",), 32),
        name="router",
    )(x2d, gain.reshape(1, d).astype(F32), rw, rb)


def _gather_kernel(idx_ref, src_ref, o_ref, sem, *, rows):
    def copy(r):
        return pltpu.make_async_copy(src_ref.at[pl.ds(idx_ref[0, 0, r], 1)], o_ref.at[pl.ds(r, 1)], sem)

    def issue(r, carry):
        copy(r).start()
        return carry

    def wait(r, carry):
        copy(r).wait()
        return carry

    lax.fori_loop(0, rows, issue, 0)
    lax.fori_loop(0, rows, wait, 0)


def _gather_rows(src, row_idx, rows=256):
    n = row_idx.shape[0]
    w = src.shape[1]
    return pl.pallas_call(
        functools.partial(_gather_kernel, rows=rows),
        grid=(n // rows,),
        in_specs=[
            pl.BlockSpec((1, 1, rows), lambda i: (i, 0, 0), memory_space=pltpu.SMEM),
            pl.BlockSpec(memory_space=pl.ANY),
        ],
        out_specs=pl.BlockSpec((rows, w), lambda i: (i, 0)),
        out_shape=jax.ShapeDtypeStruct((n, w), src.dtype),
        scratch_shapes=[pltpu.SemaphoreType.DMA(())],
        compiler_params=_cparams(("arbitrary",), 32),
        name="moe_gather",
    )(row_idx.reshape(n // rows, 1, rows), src)


def _expert_kernel(te_ref, na_ref, xs_ref, wg_ref, wu_ref, bg_ref, bu_ref, wd_ref, bd_ref, o_ref, xb_ref):
    i = pl.program_id(0)
    f = pl.program_id(1)
    active = i < na_ref[0]
    half = xs_ref.shape[1]

    @pl.when(jnp.logical_and(active, f == 0))
    def _():
        xu = xs_ref[...]
        xb_ref[:, :half] = lax.bitcast_convert_type(xu << 16, F32).astype(BF16)
        xb_ref[:, half:] = lax.bitcast_convert_type(xu & jnp.uint32(0xFFFF0000), F32).astype(BF16)
        o_ref[...] = jnp.broadcast_to(bd_ref[0], o_ref.shape)

    @pl.when(jnp.logical_and(jnp.logical_not(active), f == 0))
    def _():
        o_ref[...] = jnp.zeros(o_ref.shape, F32)

    @pl.when(active)
    def _():
        x = xb_ref[...]
        gate = jnp.minimum(_dot(x, wg_ref[0]) + bg_ref[0], SWIGLU_LIMIT)
        up = jnp.clip(_dot(x, wu_ref[0]) + bu_ref[0], -SWIGLU_LIMIT, SWIGLU_LIMIT)
        act = (up + 1.0) * gate * jax.nn.sigmoid(SWIGLU_ALPHA * gate)
        o_ref[...] += _dot(act.astype(BF16), wd_ref[0])


def _experts(xs, tile_expert, n_active, wg, wu, bg, bu, wd, bd, rt, tf=512):
    n_rows, half = xs.shape
    e, d, fdim = wg.shape
    nt = n_rows // rt
    nf = fdim // tf

    def wsel(i, f, te, na):
        on = i < na[0]
        last = jnp.maximum(na[0] - 1, 0)
        return jnp.where(on, te[i], te[last]), jnp.where(on, f, nf - 1)

    def w_gu_map(i, f, te, na):
        ee, ff = wsel(i, f, te, na)
        return (ee, 0, ff)

    def w_d_map(i, f, te, na):
        ee, ff = wsel(i, f, te, na)
        return (ee, ff, 0)

    def b_d_map(i, f, te, na):
        ee, _ = wsel(i, f, te, na)
        return (ee, 0, 0)

    grid_spec = pltpu.PrefetchScalarGridSpec(
        num_scalar_prefetch=2,
        grid=(nt, nf),
        in_specs=[
            pl.BlockSpec((rt, half), lambda i, f, te, na: (i, 0)),
            pl.BlockSpec((1, d, tf), w_gu_map),
            pl.BlockSpec((1, d, tf), w_gu_map),
            pl.BlockSpec((1, 1, tf), w_gu_map),
            pl.BlockSpec((1, 1, tf), w_gu_map),
            pl.BlockSpec((1, tf, d), w_d_map),
            pl.BlockSpec((1, 1, d), b_d_map),
        ],
        out_specs=pl.BlockSpec((rt, d), lambda i, f, te, na: (i, 0)),
        scratch_shapes=[pltpu.VMEM((rt, d), BF16)],
    )
    return pl.pallas_call(
        _expert_kernel,
        grid_spec=grid_spec,
        out_shape=jax.ShapeDtypeStruct((n_rows, d), F32),
        compiler_params=_cparams(("arbitrary", "arbitrary"), 48),
        name="moe_experts",
    )(tile_expert, n_active, xs, wg, wu, bg, bu, wd, bd)


def _combine_kernel(pr_ref, gate_ref, x_ref, ys_ref, o_ref, buf_ref, sem, *, tc):
    def copy(p):
        return pltpu.make_async_copy(ys_ref.at[pl.ds(pr_ref[0, 0, p], 1)],
                                     buf_ref.at[p & (TOP_K - 1), pl.ds(p >> 2, 1)], sem)

    def issue(p, carry):
        copy(p).start()
        return carry

    def wait(p, carry):
        copy(p).wait()
        return carry

    lax.fori_loop(0, tc * TOP_K, issue, 0)
    lax.fori_loop(0, tc * TOP_K, wait, 0)
    gates = gate_ref[...]
    acc = x_ref[...]
    for k in range(TOP_K):
        acc = acc + gates[:, k:k + 1] * buf_ref[k]
    o_ref[...] = acc


def _combine(x2d, gates, pair_row, ys, tc=128):
    t, d = x2d.shape
    return pl.pallas_call(
        functools.partial(_combine_kernel, tc=tc),
        grid=(t // tc,),
        in_specs=[
            pl.BlockSpec((1, 1, tc * TOP_K), lambda i: (i, 0, 0), memory_space=pltpu.SMEM),
            pl.BlockSpec((tc, LANES), lambda i: (i, 0)),
            pl.BlockSpec((tc, d), lambda i: (i, 0)),
            pl.BlockSpec(memory_space=pl.ANY),
        ],
        out_specs=pl.BlockSpec((tc, d), lambda i: (i, 0)),
        out_shape=jax.ShapeDtypeStruct((t, d), F32),
        scratch_shapes=[pltpu.VMEM((TOP_K, tc, d), F32), pltpu.SemaphoreType.DMA(())],
        compiler_params=_cparams(("arbitrary",), 32),
        name="moe_combine",
    )(pair_row.reshape(t // tc, 1, tc * TOP_K), gates, x2d, ys)


def _moe(x2d, gain, router_w, router_b, w_gu, b_gu, w_dn, b_dn, rt=512):
    t, d = x2d.shape
    e = router_w.shape[1]
    hp, idx, gates = _router(x2d, gain, router_w, router_b)

    n_pairs = t * TOP_K
    flat_e = idx[:, :TOP_K].reshape(n_pairs)
    onehot = (flat_e[:, None] == jnp.arange(e, dtype=jnp.int32)[None, :]).astype(jnp.int32)
    csum = jnp.cumsum(onehot, axis=0)
    rank = jnp.sum(onehot * csum, axis=1) - 1
    counts = csum[-1]
    padded = (counts + rt - 1) // rt * rt
    pad_end = jnp.cumsum(padded)
    pad_start = pad_end - padded
    dest = (jnp.sum(onehot * pad_start[None, :], axis=1) + rank).astype(jnp.int32)
    n_tiles = n_pairs // rt + e
    n_rows = n_tiles * rt
    row_tok = jnp.full((n_rows,), t, jnp.int32).at[dest].set(jnp.arange(n_pairs, dtype=jnp.int32) // TOP_K)
    tile_start = jnp.arange(n_tiles, dtype=jnp.int32) * rt
    tile_expert = jnp.minimum(jnp.searchsorted(pad_end, tile_start, side="right"), e - 1).astype(jnp.int32)
    n_active = (pad_end[-1] // rt).astype(jnp.int32).reshape(1)

    hp_pad = jnp.concatenate([hp, jnp.zeros((8, hp.shape[1]), hp.dtype)], axis=0)
    xs = _gather_rows(hp_pad, row_tok)

    fdim = w_dn.shape[1]
    wg = w_gu[:, :, 0::2].astype(BF16)
    wu = w_gu[:, :, 1::2].astype(BF16)
    bg = b_gu[:, 0::2].reshape(e, 1, fdim).astype(F32)
    bu = b_gu[:, 1::2].reshape(e, 1, fdim).astype(F32)
    ys = _experts(xs, tile_expert, n_active, wg, wu, bg, bu, w_dn.astype(BF16),
                  b_dn.reshape(e, 1, d).astype(F32), rt)
    return _combine(x2d, gates, dest, ys)


def _ple_kernel(x_ref, p_ref, gn_ref, wg_ref, wu_ref, gp_ref, o_ref):
    x = x_ref[...]
    ms = jnp.mean(x * x, axis=-1, keepdims=True)
    h = (x * lax.rsqrt(ms + RMS_EPS) * gn_ref[...]).astype(BF16)
    gate = jax.nn.sigmoid(_dot(h, wg_ref[...]))
    up = _dot(p_ref[...].astype(BF16), wu_ref[...])
    ms_u = jnp.mean(up * up, axis=-1, keepdims=True)
    o_ref[...] = x + up * lax.rsqrt(ms_u + RMS_EPS) * gp_ref[...] * gate


def _ple(x2d, p2d, g_norm, w_gate, w_up, g_post, tm=256):
    t, d = x2d.shape
    pd = p2d.shape[1]
    return pl.pallas_call(
        _ple_kernel,
        grid=(t // tm,),
        in_specs=[
            pl.BlockSpec((tm, d), lambda i: (i, 0)),
            pl.BlockSpec((tm, pd), lambda i: (i, 0)),
            pl.BlockSpec((1, d), lambda i: (0, 0)),
            pl.BlockSpec((d, d), lambda i: (0, 0)),
            pl.BlockSpec((pd, d), lambda i: (0, 0)),
            pl.BlockSpec((1, d), lambda i: (0, 0)),
        ],
        out_specs=pl.BlockSpec((tm, d), lambda i: (i, 0)),
        out_shape=jax.ShapeDtypeStruct((t, d), F32),
        compiler_params=_cparams(("parallel",), 48),
        name="ple",
    )(x2d, p2d, g_norm.reshape(1, d).astype(F32), w_gate, w_up, g_post.reshape(1, d).astype(F32))


def kernel(x, p, w_in, da_q_norm, da_k_norm, da_lambda_q1, da_lambda_k1, da_lambda_q2, da_lambda_k2, da_subln, gla_gate_w2, gla_gate_b, gla_out_norm, w_branch_da, w_branch_gla, w_merge_gate, b_merge_gate, w_out, norm_mix, norm_ffn, router_w, router_b, w_gate_up, b_gate_up, w_down, b_down, norm_ple, w_ple_gate, w_ple_up, norm_ple_post):
    bsz, s_len, d = x.shape
    t = bsz * s_len
    depth = w_in.shape[0]
    qk_w = DA_HEADS * 2 * DA_HEAD_DIM
    v_w = DA_HEADS * DA_V_DIM
    gk_w = gla_gate_w2.shape[2]
    gv_w = w_branch_gla.shape[1]
    rest_w = v_w + 2 * gk_w + 2 * gv_w
    x2d = x.reshape(t, d)
    for i in range(depth):
        lambda_init = 0.8 - 0.6 * math.exp(-0.3 * i)
        h = _rmsnorm(x2d, norm_mix[i])
        w = w_in[i]
        qn = _proj(h, w[:, :qk_w].astype(BF16), out_dtype=BF16, tm=512, tn=1024,
                   gain=da_q_norm[i], scale=DA_HEAD_DIM ** -0.5)
        kn = _proj(h, w[:, qk_w:2 * qk_w].astype(BF16), out_dtype=BF16, tm=512, tn=1024, gain=da_k_norm[i])
        rest = _proj(h, w[:, 2 * qk_w:2 * qk_w + rest_w].astype(BF16), out_dtype=BF16, tm=512, tn=1024)
        w_lr = jnp.zeros((d, LANES), BF16).at[:, :GLA_GATE_RANK].set(w[:, 2 * qk_w + rest_w:].astype(BF16))
        glr = _proj(h, w_lr, out_dtype=F32, tm=512, tn=LANES)

        lam_vecs = jnp.stack([da_lambda_q1[i], da_lambda_k1[i], da_lambda_q2[i], da_lambda_k2[i]])
        y_da = _diff_attention(qn, kn, rest, 0, lam_vecs, da_subln[i], bsz, s_len, lambda_init)

        w2p = jnp.zeros((LANES, gk_w), F32).at[:GLA_GATE_RANK].set(gla_gate_w2[i].astype(F32))
        y_gla = _gla(rest, (v_w, v_w + gk_w, v_w + 2 * gk_w, v_w + 2 * gk_w + gv_w), glr, w2p,
                     gla_gate_b[i], gla_out_norm[i], bsz, s_len)

        mixed = _merge(h, y_da, y_gla, w_merge_gate[i].astype(BF16), b_merge_gate[i].reshape(1, 2 * d).astype(F32),
                       w_branch_da[i].astype(BF16), w_branch_gla[i].astype(BF16))
        x2d = _out_proj(x2d, mixed, w_out[i].astype(BF16))

        x2d = _moe(x2d, norm_ffn[i], router_w[i], router_b[i], w_gate_up[i], b_gate_up[i], w_down[i], b_down[i])

        x2d = _ple(x2d, p[i].reshape(t, p.shape[-1]), norm_ple[i], w_ple_gate[i].astype(BF16),
                   w_ple_up[i].astype(BF16), norm_ple_post[i])
    return x2d.reshape(bsz, s_len, d)
```

```python
import functools
import math

import jax
import jax.numpy as jnp
from jax import lax
from jax.experimental import pallas as pl
from jax.experimental.pallas import tpu as pltpu

F32 = jnp.float32
BF16 = jnp.bfloat16
HIGHEST = lax.Precision.HIGHEST

CHUNK = 64
CHUNK_SHIFT = CHUNK.bit_length() - 1
RMS_EPS = 1e-6
DA_HEADS = 8
DA_HEAD_DIM = 128
DA_V_DIM = 2 * DA_HEAD_DIM
GLA_HEADS = 4
GLA_GATE_RANK = 16
GLA_TAU = 16.0
N_EXPERTS = 32
TOP_K = 4
SWIGLU_LIMIT = 7.0
SWIGLU_ALPHA = 1.702

LANES = 128
NEG_BIG = -1e30

MIB = 1024 * 1024


def _cparams(sem, vmem_mib):
    return pltpu.CompilerParams(dimension_semantics=sem, vmem_limit_bytes=vmem_mib * MIB)


def _dot(a, b):
    return jnp.dot(a, b, preferred_element_type=F32)


def _dot_nt(a, b):
    return lax.dot_general(a, b, (((1,), (1,)), ((), ())), preferred_element_type=F32)


def _dot_tn(a, b):
    return lax.dot_general(a, b, (((0,), (0,)), ((), ())), preferred_element_type=F32)


def _rmsnorm_kernel(x_ref, g_ref, o_ref):
    x = x_ref[...]
    ms = jnp.mean(x * x, axis=-1, keepdims=True)
    o_ref[...] = (x * lax.rsqrt(ms + RMS_EPS) * g_ref[...]).astype(o_ref.dtype)


def _rmsnorm(x2d, gain, tm=512):
    t, d = x2d.shape
    return pl.pallas_call(
        _rmsnorm_kernel,
        grid=(t // tm,),
        in_specs=[pl.BlockSpec((tm, d), lambda i: (i, 0)), pl.BlockSpec((1, d), lambda i: (0, 0))],
        out_specs=pl.BlockSpec((tm, d), lambda i: (i, 0)),
        out_shape=jax.ShapeDtypeStruct((t, d), BF16),
        compiler_params=_cparams(("parallel",), 32),
        name="rmsnorm",
    )(x2d, gain.reshape(1, d).astype(F32))


def _proj_kernel(a_ref, w_ref, g_ref, o_ref, *, group_norm, scale):
    acc = _dot(a_ref[...], w_ref[...])
    if group_norm:
        for c in range(acc.shape[1] // DA_HEAD_DIM):
            blk = acc[:, c * DA_HEAD_DIM:(c + 1) * DA_HEAD_DIM]
            ms = jnp.mean(blk * blk, axis=-1, keepdims=True)
            y = blk * lax.rsqrt(ms + RMS_EPS) * g_ref[...] * scale
            o_ref[:, c * DA_HEAD_DIM:(c + 1) * DA_HEAD_DIM] = y.astype(o_ref.dtype)
    else:
        o_ref[...] = acc.astype(o_ref.dtype)


def _proj(a, w, *, out_dtype, tm, tn, gain=None, scale=1.0):
    t, k = a.shape
    n = w.shape[1]
    group_norm = gain is not None
    g = (gain if group_norm else jnp.ones((DA_HEAD_DIM,), F32)).reshape(1, DA_HEAD_DIM).astype(F32)
    return pl.pallas_call(
        functools.partial(_proj_kernel, group_norm=group_norm, scale=scale),
        grid=(n // tn, t // tm),
        in_specs=[
            pl.BlockSpec((tm, k), lambda j, i: (i, 0)),
            pl.BlockSpec((k, tn), lambda j, i: (0, j)),
            pl.BlockSpec((1, DA_HEAD_DIM), lambda j, i: (0, 0)),
        ],
        out_specs=pl.BlockSpec((tm, tn), lambda j, i: (i, j)),
        out_shape=jax.ShapeDtypeStruct((t, n), out_dtype),
        compiler_params=_cparams(("parallel", "parallel"), 48),
        name="proj",
    )(a, w, g)


def _da_kernel(slope_ref, q_ref, k_ref, v_ref, lamv_ref, subg_ref, o_ref,
               m_ref, l_ref, acc_ref, *, tq, lambda_init):
    h = pl.program_id(1)
    i = pl.program_id(2)
    slope = slope_ref[h]
    t0 = i * tq
    hd = DA_HEAD_DIM

    m_ref[...] = jnp.full(m_ref.shape, NEG_BIG, F32)
    l_ref[...] = jnp.zeros(l_ref.shape, F32)
    acc_ref[...] = jnp.zeros(acc_ref.shape, F32)

    def update(c, s, v):
        m_old = m_ref[c]
        m_new = jnp.maximum(m_old, jnp.max(s, axis=-1, keepdims=True))
        alpha = jnp.exp(m_old - m_new)
        p = jnp.exp(s - m_new)
        l_ref[c] = alpha * l_ref[c] + jnp.sum(p, axis=-1, keepdims=True)
        acc_ref[c] = alpha * acc_ref[c] + _dot(p.astype(BF16), v)
        m_ref[c] = m_new

    def past_tile(j, carry):
        r0 = pl.multiple_of(j * tq, tq)
        col = r0 + lax.broadcasted_iota(jnp.int32, (1, tq), 1)
        cb = slope * (col - t0).astype(F32)
        v = v_ref[pl.ds(r0, tq), :]
        for c in range(2):
            s = _dot_nt(q_ref[:, c * hd:(c + 1) * hd], k_ref[pl.ds(r0, tq), c * hd:(c + 1) * hd]) + cb
            update(c, s, v)
        return carry

    lax.fori_loop(0, i, past_tile, 0)

    r0 = pl.multiple_of(i * tq, tq)
    row = lax.broadcasted_iota(jnp.int32, (tq, tq), 0)
    col = lax.broadcasted_iota(jnp.int32, (tq, tq), 1)
    bias = slope * (row - jnp.abs(row - col)).astype(F32)
    allowed = (col >> CHUNK_SHIFT) <= (row >> CHUNK_SHIFT)
    v = v_ref[pl.ds(r0, tq), :]
    for c in range(2):
        s = _dot_nt(q_ref[:, c * hd:(c + 1) * hd], k_ref[pl.ds(r0, tq), c * hd:(c + 1) * hd])
        s = jnp.where(allowed, s + bias, NEG_BIG)
        update(c, s, v)

    lamv = lamv_ref[...]
    lam = (jnp.exp(jnp.sum(lamv[0:1] * lamv[1:2], axis=-1, keepdims=True))
           - jnp.exp(jnp.sum(lamv[2:3] * lamv[3:4], axis=-1, keepdims=True)) + lambda_init)
    o = acc_ref[0] / l_ref[0] - lam * (acc_ref[1] / l_ref[1])
    ms = jnp.mean(o * o, axis=-1, keepdims=True)
    y = o * lax.rsqrt(ms + RMS_EPS) * subg_ref[...] * (1.0 - lambda_init)
    o_ref[...] = y.astype(o_ref.dtype)


def _diff_attention(qn, kn, v_src, v_col0, lam_vecs, subln_g, bsz, s_len, lambda_init, tq=256):
    t = bsz * s_len
    nq = s_len // tq
    vb = v_col0 // DA_V_DIM
    slopes = jnp.exp2(-8.0 * jnp.arange(1, DA_HEADS + 1, dtype=F32) / DA_HEADS)
    grid_spec = pltpu.PrefetchScalarGridSpec(
        num_scalar_prefetch=1,
        grid=(bsz, DA_HEADS, nq),
        in_specs=[
            pl.BlockSpec((tq, DA_V_DIM), lambda b, h, i, s: (b * nq + i, h)),
            pl.BlockSpec((s_len, DA_V_DIM), lambda b, h, i, s: (b, h)),
            pl.BlockSpec((s_len, DA_V_DIM), lambda b, h, i, s: (b, vb + h)),
            pl.BlockSpec((4, DA_HEAD_DIM), lambda b, h, i, s: (0, 0)),
            pl.BlockSpec((1, DA_V_DIM), lambda b, h, i, s: (0, 0)),
        ],
        out_specs=pl.BlockSpec((tq, DA_V_DIM), lambda b, h, i, s: (b * nq + i, h)),
        scratch_shapes=[
            pltpu.VMEM((2, tq, 1), F32),
            pltpu.VMEM((2, tq, 1), F32),
            pltpu.VMEM((2, tq, DA_V_DIM), F32),
        ],
    )
    return pl.pallas_call(
        functools.partial(_da_kernel, tq=tq, lambda_init=lambda_init),
        grid_spec=grid_spec,
        out_shape=jax.ShapeDtypeStruct((t, DA_HEADS * DA_V_DIM), BF16),
        compiler_params=_cparams(("parallel", "parallel", "arbitrary"), 32),
        name="diff_attention",
    )(slopes, qn, kn, v_src, lam_vecs.astype(F32), subln_g.reshape(1, DA_V_DIM).astype(F32))


def _gla_kernel(q_ref, k_ref, v_ref, g_ref, glr_ref, w2_ref, gb_ref, og_ref, o_ref, state_ref,
                *, tb, dk, dv):
    @pl.when(pl.program_id(2) == 0)
    def _():
        state_ref[...] = jnp.zeros(state_ref.shape, F32)

    row = lax.broadcasted_iota(jnp.int32, (CHUNK, CHUNK), 0)
    col = lax.broadcasted_iota(jnp.int32, (CHUNK, CHUNK), 1)
    lower = row >= col
    tri = lower.astype(F32)
    mid = CHUNK // 2

    def chunk(c, carry):
        r0 = pl.multiple_of(c * CHUNK, CHUNK)
        z = jnp.dot(glr_ref[pl.ds(r0, CHUNK), :], w2_ref[...], precision=HIGHEST,
                    preferred_element_type=F32) + gb_ref[...]
        log_a = (jnp.minimum(z, 0.0) - jnp.log1p(jnp.exp(-jnp.abs(z)))) * (1.0 / GLA_TAU)
        b = jnp.dot(tri, log_a, precision=HIGHEST, preferred_element_type=F32)
        b_last = b[CHUNK - 1:CHUNK, :]
        b_mid = b[mid:mid + 1, :]
        q = q_ref[pl.ds(r0, CHUNK), :].astype(F32) * (dk ** -0.5)
        k = k_ref[pl.ds(r0, CHUNK), :].astype(F32)
        v = v_ref[pl.ds(r0, CHUNK), :]
        e_fwd = jnp.exp(b - b_mid)
        e_bwd = jnp.exp(b_mid - b)
        a_lo = _dot_nt((q * e_fwd).astype(BF16), (k * e_bwd).astype(BF16))
        a_up = _dot_nt((q * e_bwd).astype(BF16), (k * e_fwd).astype(BF16))
        attn = jnp.where(lower, a_lo, a_up)
        state = state_ref[...]
        o = _dot(attn.astype(BF16), v) + _dot_nt((q * jnp.exp(b)).astype(BF16), state.astype(BF16))
        kd = (k * jnp.exp(b_last - b)).astype(BF16)
        state_ref[...] = state * jnp.exp(b_last) + _dot_tn(v, kd)
        ms = jnp.mean(o * o, axis=-1, keepdims=True)
        g = g_ref[pl.ds(r0, CHUNK), :].astype(F32)
        y = o * lax.rsqrt(ms + RMS_EPS) * og_ref[...] * (g * jax.nn.sigmoid(g))
        o_ref[pl.ds(r0, CHUNK), :] = y.astype(o_ref.dtype)
        return carry

    lax.fori_loop(0, tb // CHUNK, chunk, 0)


def _gla(src, cols, glr, w2p, gate_b, out_g, bsz, s_len, tb=512):
    t = bsz * s_len
    nb = s_len // tb
    kw = w2p.shape[1]
    dk = kw // GLA_HEADS
    vw = out_g.shape[0] * GLA_HEADS
    dv = vw // GLA_HEADS
    cq, ck, cv, cg = cols
    row_map = lambda b, h, i: (b * nb + i)
    return pl.pallas_call(
        functools.partial(_gla_kernel, tb=tb, dk=dk, dv=dv),
        grid=(bsz, GLA_HEADS, nb),
        in_specs=[
            pl.BlockSpec((tb, dk), lambda b, h, i: (row_map(b, h, i), cq // dk + h)),
            pl.BlockSpec((tb, dk), lambda b, h, i: (row_map(b, h, i), ck // dk + h)),
            pl.BlockSpec((tb, dv), lambda b, h, i: (row_map(b, h, i), cv // dv + h)),
            pl.BlockSpec((tb, dv), lambda b, h, i: (row_map(b, h, i), cg // dv + h)),
            pl.BlockSpec((tb, LANES), lambda b, h, i: (row_map(b, h, i), 0)),
            pl.BlockSpec((LANES, dk), lambda b, h, i: (0, h)),
            pl.BlockSpec((1, dk), lambda b, h, i: (0, h)),
            pl.BlockSpec((1, dv), lambda b, h, i: (0, 0)),
        ],
        out_specs=pl.BlockSpec((tb, dv), lambda b, h, i: (row_map(b, h, i), h)),
        out_shape=jax.ShapeDtypeStruct((t, vw), BF16),
        scratch_shapes=[pltpu.VMEM((dv, dk), F32)],
        compiler_params=_cparams(("parallel", "parallel", "arbitrary"), 32),
        name="gla",
    )(src, src, src, src, glr, w2p, gate_b.reshape(1, kw).astype(F32), out_g.reshape(1, dv).astype(F32))


def _merge_kernel(h_ref, ya_ref, yb_ref, wga_ref, wgb_ref, ba_ref, bb_ref, wa_ref, wb_ref, o_ref):
    h = h_ref[...]
    ga = jax.nn.sigmoid(_dot(h, wga_ref[...]) + ba_ref[...])
    gb = jax.nn.sigmoid(_dot(h, wgb_ref[...]) + bb_ref[...])
    mixed = ga * _dot(ya_ref[...], wa_ref[...]) + gb * _dot(yb_ref[...], wb_ref[...])
    o_ref[...] = mixed.astype(o_ref.dtype)


def _merge(h, y_da, y_gla, w_gate, b_gate, w_da, w_gla, tm=512, tn=512):
    t, d = h.shape
    nb = d // tn
    act = lambda: pl.BlockSpec((tm, d), lambda j, i: (i, 0))
    return pl.pallas_call(
        _merge_kernel,
        grid=(nb, t // tm),
        in_specs=[
            act(), act(), act(),
            pl.BlockSpec((d, tn), lambda j, i: (0, j)),
            pl.BlockSpec((d, tn), lambda j, i: (0, nb + j)),
            pl.BlockSpec((1, tn), lambda j, i: (0, j)),
            pl.BlockSpec((1, tn), lambda j, i: (0, nb + j)),
            pl.BlockSpec((d, tn), lambda j, i: (0, j)),
            pl.BlockSpec((d, tn), lambda j, i: (0, j)),
        ],
        out_specs=pl.BlockSpec((tm, tn), lambda j, i: (i, j)),
        out_shape=jax.ShapeDtypeStruct((t, d), BF16),
        compiler_params=_cparams(("parallel", "parallel"), 48),
        name="merge",
    )(h, y_da, y_gla, w_gate, w_gate, b_gate, b_gate, w_da, w_gla)


def _out_proj_kernel(x_ref, m_ref, w_ref, o_ref):
    o_ref[...] = x_ref[...] + _dot(m_ref[...], w_ref[...])


def _out_proj(x2d, mixed, w_out, tm=256):
    t, d = x2d.shape
    return pl.pallas_call(
        _out_proj_kernel,
        grid=(t // tm,),
        in_specs=[
            pl.BlockSpec((tm, d), lambda i: (i, 0)),
            pl.BlockSpec((tm, d), lambda i: (i, 0)),
            pl.BlockSpec((d, d), lambda i: (0, 0)),
        ],
        out_specs=pl.BlockSpec((tm, d), lambda i: (i, 0)),
        out_shape=jax.ShapeDtypeStruct((t, d), F32),
        compiler_params=_cparams(("parallel",), 48),
        name="out_proj",
    )(x2d, mixed, w_out)


def _router_kernel(x_ref, g_ref, rw_ref, rb_ref, hp_ref, idx_ref, gate_ref):
    x = x_ref[...]
    ms = jnp.mean(x * x, axis=-1, keepdims=True)
    h = x * lax.rsqrt(ms + RMS_EPS) * g_ref[...]
    half = h.shape[1] // 2
    lo = lax.bitcast_convert_type(h[:, :half].astype(BF16).astype(F32), jnp.uint32)
    hi = lax.bitcast_convert_type(h[:, half:].astype(BF16).astype(F32), jnp.uint32)
    hp_ref[...] = (lo >> 16) | (hi & jnp.uint32(0xFFFF0000))

    logits = jnp.dot(h, rw_ref[...], precision=HIGHEST, preferred_element_type=F32) + rb_ref[...]
    lane = lax.broadcasted_iota(jnp.int32, logits.shape, 1)
    vals, idxs = [], []
    for _ in range(TOP_K):
        m = jnp.max(logits, axis=-1, keepdims=True)
        idx = jnp.min(jnp.where(logits == m, lane, LANES), axis=-1, keepdims=True)
        vals.append(m)
        idxs.append(idx)
        logits = jnp.where(lane == idx, -jnp.inf, logits)
    exps = [jnp.exp(v - vals[0]) for v in vals]
    denom = exps[0] + exps[1] + exps[2] + exps[3]
    idx_out = jnp.zeros(lane.shape, jnp.int32)
    gate_out = jnp.zeros(lane.shape, F32)
    for k in range(TOP_K):
        idx_out = jnp.where(lane == k, idxs[k], idx_out)
        gate_out = jnp.where(lane == k, exps[k] / denom, gate_out)
    idx_ref[...] = idx_out
    gate_ref[...] = gate_out


def _router(x2d, gain, router_w, router_b, tm=256):
    t, d = x2d.shape
    e = router_w.shape[1]
    rw = jnp.zeros((d, LANES), F32).at[:, :e].set(router_w.astype(F32))
    rb = jnp.full((1, LANES), NEG_BIG, F32).at[0, :e].set(router_b.astype(F32))
    return pl.pallas_call(
        _router_kernel,
        grid=(t // tm,),
        in_specs=[
            pl.BlockSpec((tm, d), lambda i: (i, 0)),
            pl.BlockSpec((1, d), lambda i: (0, 0)),
            pl.BlockSpec((d, LANES), lambda i: (0, 0)),
            pl.BlockSpec((1, LANES), lambda i: (0, 0)),
        ],
        out_specs=[
            pl.BlockSpec((tm, d // 2), lambda i: (i, 0)),
            pl.BlockSpec((tm, LANES), lambda i: (i, 0)),
            pl.BlockSpec((tm, LANES), lambda i: (i, 0)),
        ],
        out_shape=[
            jax.ShapeDtypeStruct((t, d // 2), jnp.uint32),
            jax.ShapeDtypeStruct((t, LANES), jnp.int32),
            jax.ShapeDtypeStruct((t, LANES), F32),
        ],
        compiler_params=_cparams(("parallel",), 32),
        name="router",
    )(x2d, gain.reshape(1, d).astype(F32), rw, rb)


def _gather_kernel(idx_ref, src_ref, o_ref, sem, *, rows):
    def copy(r):
        return pltpu.make_async_copy(src_ref.at[pl.ds(idx_ref[0, 0, r], 1)], o_ref.at[pl.ds(r, 1)], sem)

    def issue(r, carry):
        copy(r).start()
        return carry

    def wait(r, carry):
        copy(r).wait()
        return carry

    lax.fori_loop(0, rows, issue, 0)
    lax.fori_loop(0, rows, wait, 0)


def _gather_rows(src, row_idx, rows=256):
    n = row_idx.shape[0]
    w = src.shape[1]
    return pl.pallas_call(
        functools.partial(_gather_kernel, rows=rows),
        grid=(n // rows,),
        in_specs=[
            pl.BlockSpec((1, 1, rows), lambda i: (i, 0, 0), memory_space=pltpu.SMEM),
            pl.BlockSpec(memory_space=pl.ANY),
        ],
        out_specs=pl.BlockSpec((rows, w), lambda i: (i, 0)),
        out_shape=jax.ShapeDtypeStruct((n, w), src.dtype),
        scratch_shapes=[pltpu.SemaphoreType.DMA(())],
        compiler_params=_cparams(("arbitrary",), 32),
        name="moe_gather",
    )(row_idx.reshape(n // rows, 1, rows), src)


MOE_SUB = 256
MOE_NSUB = 6
MOE_TF = 256


def _expert_kernel(ge_ref, gs_ref, gn_ref, ng_ref, tot_ref,
                   xs_ref, wgu_ref, bgu_ref, wdn_ref, bdn_ref, ys_ref,
                   xu_ref, xb_ref, acc_ref, wgu_b_ref, wdn_p_ref, wdn_b_ref, zero_ref, sem_in, sem_out,
                   *, n_sub_total):
    g = pl.program_id(0)
    f = pl.program_id(1)
    nf = pl.num_programs(1)
    active = g < ng_ref[0]
    nsub = gn_ref[g]
    start = gs_ref[g]
    half = xu_ref.shape[2]
    tf = wdn_ref.shape[1]

    def in_copy(s):
        r0 = pl.multiple_of((start + s) * MOE_SUB, MOE_SUB)
        return pltpu.make_async_copy(xs_ref.at[pl.ds(r0, MOE_SUB)], xu_ref.at[s], sem_in.at[s])

    def out_copy(s):
        r0 = pl.multiple_of((start + s) * MOE_SUB, MOE_SUB)
        return pltpu.make_async_copy(acc_ref.at[s], ys_ref.at[pl.ds(r0, MOE_SUB)], sem_out)

    @pl.when(jnp.logical_and(g == 0, f == 0))
    def _():
        zero_ref[...] = jnp.zeros(zero_ref.shape, F32)

    @pl.when(jnp.logical_and(active, f == 0))
    def _():
        def issue(s, c):
            in_copy(s).start()
            return c

        def unpack(s, c):
            in_copy(s).wait()
            xu = xu_ref[s]
            xb_ref[s, :, :half] = lax.bitcast_convert_type(xu << 16, F32).astype(BF16)
            xb_ref[s, :, half:] = lax.bitcast_convert_type(xu & jnp.uint32(0xFFFF0000), F32).astype(BF16)
            return c

        lax.fori_loop(0, nsub, issue, 0)
        lax.fori_loop(0, nsub, unpack, 0)

    @pl.when(active)
    def _():
        wgu_b_ref[...] = wgu_ref[0].astype(BF16)
        hl = LANES // 2
        for c in range(wdn_p_ref.shape[0]):
            cols = slice(c * LANES, (c + 1) * LANES)
            for m in range(tf // LANES):
                wdn_p_ref[c, pl.ds(m * LANES, hl, stride=2), :] = wdn_ref[0, m * LANES:m * LANES + hl, cols]
                wdn_p_ref[c, pl.ds(m * LANES + 1, hl, stride=2), :] = (
                    wdn_ref[0, m * LANES + hl:(m + 1) * LANES, cols])
            wdn_b_ref[:, cols] = wdn_p_ref[c].astype(BF16)
        bgu = bgu_ref[0]
        lane = lax.broadcasted_iota(jnp.int32, (MOE_SUB, LANES), 1)
        even = (lane & 1) == 0

        def sub(s, c):
            gu = _dot(xb_ref[s], wgu_b_ref[...]) + bgu
            gates, ups = [], []
            for m in range(tf // LANES):
                a = gu[:, 2 * m * LANES:(2 * m + 1) * LANES]
                b = gu[:, (2 * m + 1) * LANES:(2 * m + 2) * LANES]
                gates.append(jnp.where(even, a, pltpu.roll(b, 1, 1)))
                ups.append(jnp.where(even, pltpu.roll(a, LANES - 1, 1), b))
            gate = jnp.minimum(jnp.concatenate(gates, axis=1), SWIGLU_LIMIT)
            up = jnp.clip(jnp.concatenate(ups, axis=1), -SWIGLU_LIMIT, SWIGLU_LIMIT)
            act = (up + 1.0) * gate * jax.nn.sigmoid(SWIGLU_ALPHA * gate)
            contrib = _dot(act.astype(BF16), wdn_b_ref[...])

            @pl.when(f == 0)
            def _():
                acc_ref[s] = contrib + bdn_ref[0]

            @pl.when(f > 0)
            def _():
                acc_ref[s] += contrib

            return c

        lax.fori_loop(0, nsub, sub, 0)

    @pl.when(jnp.logical_and(active, f == nf - 1))
    def _():
        def issue(s, c):
            out_copy(s).start()
            return c

        def wait(s, c):
            out_copy(s).wait()
            return c

        lax.fori_loop(0, nsub, issue, 0)
        lax.fori_loop(0, nsub, wait, 0)

    @pl.when(jnp.logical_and(g == pl.num_programs(0) - 1, f == nf - 1))
    def _():
        def fill(s, c):
            r0 = pl.multiple_of(s * MOE_SUB, MOE_SUB)
            cp = pltpu.make_async_copy(zero_ref, ys_ref.at[pl.ds(r0, MOE_SUB)], sem_out)
            cp.start()
            cp.wait()
            return c

        lax.fori_loop(tot_ref[0], n_sub_total, fill, 0)


def _experts(xs, grp_expert, grp_start, grp_nsub, n_groups, tot_sub, w_gu, b_gu, w_dn, b_dn):
    n_rows, half = xs.shape
    e, d, f2 = w_gu.shape
    fdim = f2 // 2
    tf = MOE_TF
    nf = fdim // tf
    n_grp = grp_expert.shape[0]

    def wsel(g, f, ge, gs, gn, ng, tot):
        on = g < ng[0]
        last = jnp.maximum(ng[0] - 1, 0)
        return jnp.where(on, ge[g], ge[last]), jnp.where(on, f, nf - 1)

    def w_gu_map(g, f, *pre):
        ee, ff = wsel(g, f, *pre)
        return (ee, 0, ff)

    def w_dn_map(g, f, *pre):
        ee, ff = wsel(g, f, *pre)
        return (ee, ff, 0)

    def b_dn_map(g, f, *pre):
        ee, _ = wsel(g, f, *pre)
        return (ee, 0, 0)

    grid_spec = pltpu.PrefetchScalarGridSpec(
        num_scalar_prefetch=5,
        grid=(n_grp, nf),
        in_specs=[
            pl.BlockSpec(memory_space=pl.ANY),
            pl.BlockSpec((1, d, 2 * tf), w_gu_map),
            pl.BlockSpec((1, 1, 2 * tf), w_gu_map),
            pl.BlockSpec((1, tf, d), w_dn_map),
            pl.BlockSpec((1, 1, d), b_dn_map),
        ],
        out_specs=pl.BlockSpec(memory_space=pl.ANY),
        scratch_shapes=[
            pltpu.VMEM((MOE_NSUB, MOE_SUB, half), jnp.uint32),
            pltpu.VMEM((MOE_NSUB, MOE_SUB, d), BF16),
            pltpu.VMEM((MOE_NSUB, MOE_SUB, d), F32),
            pltpu.VMEM((d, 2 * tf), BF16),
            pltpu.VMEM((d // LANES, tf, LANES), F32),
            pltpu.VMEM((tf, d), BF16),
            pltpu.VMEM((MOE_SUB, d), F32),
            pltpu.SemaphoreType.DMA((MOE_NSUB,)),
            pltpu.SemaphoreType.DMA(()),
        ],
    )
    return pl.pallas_call(
        functools.partial(_expert_kernel, n_sub_total=n_rows // MOE_SUB),
        grid_spec=grid_spec,
        out_shape=jax.ShapeDtypeStruct((n_rows, d), F32),
        compiler_params=_cparams(("arbitrary", "arbitrary"), 56),
        name="moe_experts",
    )(grp_expert, grp_start, grp_nsub, n_groups, tot_sub, xs,
      w_gu, b_gu.reshape(e, 1, f2).astype(F32), w_dn, b_dn.reshape(e, 1, d).astype(F32))


def _combine_kernel(pr_ref, gate_ref, x_ref, ys_ref, o_ref, buf_ref, sem, *, tc):
    def copy(p):
        return pltpu.make_async_copy(ys_ref.at[pl.ds(pr_ref[0, 0, p], 1)],
                                     buf_ref.at[p & (TOP_K - 1), pl.ds(p >> 2, 1)], sem)

    def issue(p, carry):
        copy(p).start()
        return carry

    def wait(p, carry):
        copy(p).wait()
        return carry

    lax.fori_loop(0, tc * TOP_K, issue, 0)
    lax.fori_loop(0, tc * TOP_K, wait, 0)
    gates = gate_ref[...]
    acc = x_ref[...]
    for k in range(TOP_K):
        acc = acc + gates[:, k:k + 1] * buf_ref[k]
    o_ref[...] = acc


def _combine(x2d, gates, pair_row, ys, tc=128):
    t, d = x2d.shape
    return pl.pallas_call(
        functools.partial(_combine_kernel, tc=tc),
        grid=(t // tc,),
        in_specs=[
            pl.BlockSpec((1, 1, tc * TOP_K), lambda i: (i, 0, 0), memory_space=pltpu.SMEM),
            pl.BlockSpec((tc, LANES), lambda i: (i, 0)),
            pl.BlockSpec((tc, d), lambda i: (i, 0)),
            pl.BlockSpec(memory_space=pl.ANY),
        ],
        out_specs=pl.BlockSpec((tc, d), lambda i: (i, 0)),
        out_shape=jax.ShapeDtypeStruct((t, d), F32),
        scratch_shapes=[pltpu.VMEM((TOP_K, tc, d), F32), pltpu.SemaphoreType.DMA(())],
        compiler_params=_cparams(("arbitrary",), 32),
        name="moe_combine",
    )(pair_row.reshape(t // tc, 1, tc * TOP_K), gates, x2d, ys)


def _moe(x2d, gain, router_w, router_b, w_gu, b_gu, w_dn, b_dn):
    t, d = x2d.shape
    e = router_w.shape[1]
    hp, idx, gates = _router(x2d, gain, router_w, router_b)

    n_pairs = t * TOP_K
    flat_e = idx[:, :TOP_K].reshape(n_pairs)
    onehot = (flat_e[:, None] == jnp.arange(e, dtype=jnp.int32)[None, :]).astype(jnp.int32)
    csum = jnp.cumsum(onehot, axis=0)
    rank = jnp.sum(onehot * csum, axis=1) - 1
    counts = csum[-1]
    sub_e = (counts + MOE_SUB - 1) // MOE_SUB
    sub_end = jnp.cumsum(sub_e)
    sub_start = sub_end - sub_e
    dest = (jnp.sum(onehot * (sub_start * MOE_SUB)[None, :], axis=1) + rank).astype(jnp.int32)
    n_sub_total = n_pairs // MOE_SUB + e
    n_rows = n_sub_total * MOE_SUB
    row_tok = jnp.full((n_rows,), t, jnp.int32).at[dest].set(jnp.arange(n_pairs, dtype=jnp.int32) // TOP_K)

    grp_e = (sub_e + MOE_NSUB - 1) // MOE_NSUB
    grp_end = jnp.cumsum(grp_e)
    n_grp_max = (n_sub_total + MOE_NSUB - 1) // MOE_NSUB + e
    gidx = jnp.arange(n_grp_max, dtype=jnp.int32)
    g_exp = jnp.minimum(jnp.searchsorted(grp_end, gidx, side="right"), e - 1).astype(jnp.int32)
    g_local = gidx - (grp_end - grp_e)[g_exp]
    g_on = gidx < grp_end[-1]
    g_start = jnp.where(g_on, sub_start[g_exp] + g_local * MOE_NSUB, 0).astype(jnp.int32)
    g_nsub = jnp.where(g_on, jnp.minimum(MOE_NSUB, sub_e[g_exp] - g_local * MOE_NSUB), 0).astype(jnp.int32)

    hp_pad = jnp.concatenate([hp, jnp.zeros((8, hp.shape[1]), hp.dtype)], axis=0)
    xs = _gather_rows(hp_pad, row_tok)
    ys = _experts(xs, g_exp, g_start, g_nsub, grp_end[-1].astype(jnp.int32).reshape(1),
                  sub_end[-1].astype(jnp.int32).reshape(1), w_gu, b_gu, w_dn, b_dn)
    return _combine(x2d, gates, dest, ys)


def _ple_kernel(x_ref, p_ref, gn_ref, wg_ref, wu_ref, gp_ref, o_ref):
    x = x_ref[...]
    ms = jnp.mean(x * x, axis=-1, keepdims=True)
    h = (x * lax.rsqrt(ms + RMS_EPS) * gn_ref[...]).astype(BF16)
    gate = jax.nn.sigmoid(_dot(h, wg_ref[...]))
    up = _dot(p_ref[...].astype(BF16), wu_ref[...])
    ms_u = jnp.mean(up * up, axis=-1, keepdims=True)
    o_ref[...] = x + up * lax.rsqrt(ms_u + RMS_EPS) * gp_ref[...] * gate


def _ple(x2d, p2d, g_norm, w_gate, w_up, g_post, tm=256):
    t, d = x2d.shape
    pd = p2d.shape[1]
    return pl.pallas_call(
        _ple_kernel,
        grid=(t // tm,),
        in_specs=[
            pl.BlockSpec((tm, d), lambda i: (i, 0)),
            pl.BlockSpec((tm, pd), lambda i: (i, 0)),
            pl.BlockSpec((1, d), lambda i: (0, 0)),
            pl.BlockSpec((d, d), lambda i: (0, 0)),
            pl.BlockSpec((pd, d), lambda i: (0, 0)),
            pl.BlockSpec((1, d), lambda i: (0, 0)),
        ],
        out_specs=pl.BlockSpec((tm, d), lambda i: (i, 0)),
        out_shape=jax.ShapeDtypeStruct((t, d), F32),
        compiler_params=_cparams(("parallel",), 48),
        name="ple",
    )(x2d, p2d, g_norm.reshape(1, d).astype(F32), w_gate, w_up, g_post.reshape(1, d).astype(F32))


def kernel(x, p, w_in, da_q_norm, da_k_norm, da_lambda_q1, da_lambda_k1, da_lambda_q2, da_lambda_k2, da_subln, gla_gate_w2, gla_gate_b, gla_out_norm, w_branch_da, w_branch_gla, w_merge_gate, b_merge_gate, w_out, norm_mix, norm_ffn, router_w, router_b, w_gate_up, b_gate_up, w_down, b_down, norm_ple, w_ple_gate, w_ple_up, norm_ple_post):
    bsz, s_len, d = x.shape
    t = bsz * s_len
    depth = w_in.shape[0]
    qk_w = DA_HEADS * 2 * DA_HEAD_DIM
    v_w = DA_HEADS * DA_V_DIM
    gk_w = gla_gate_w2.shape[2]
    gv_w = w_branch_gla.shape[1]
    rest_w = v_w + 2 * gk_w + 2 * gv_w
    x2d = x.reshape(t, d)
    for i in range(depth):
        lambda_init = 0.8 - 0.6 * math.exp(-0.3 * i)
        h = _rmsnorm(x2d, norm_mix[i])
        w = w_in[i]
        qn = _proj(h, w[:, :qk_w].astype(BF16), out_dtype=BF16, tm=512, tn=1024,
                   gain=da_q_norm[i], scale=DA_HEAD_DIM ** -0.5)
        kn = _proj(h, w[:, qk_w:2 * qk_w].astype(BF16), out_dtype=BF16, tm=512, tn=1024, gain=da_k_norm[i])
        rest = _proj(h, w[:, 2 * qk_w:2 * qk_w + rest_w].astype(BF16), out_dtype=BF16, tm=512, tn=1024)
        w_lr = jnp.zeros((d, LANES), BF16).at[:, :GLA_GATE_RANK].set(w[:, 2 * qk_w + rest_w:].astype(BF16))
        glr = _proj(h, w_lr, out_dtype=F32, tm=512, tn=LANES)

        lam_vecs = jnp.stack([da_lambda_q1[i], da_lambda_k1[i], da_lambda_q2[i], da_lambda_k2[i]])
        y_da = _diff_attention(qn, kn, rest, 0, lam_vecs, da_subln[i], bsz, s_len, lambda_init)

        w2p = jnp.zeros((LANES, gk_w), F32).at[:GLA_GATE_RANK].set(gla_gate_w2[i].astype(F32))
        y_gla = _gla(rest, (v_w, v_w + gk_w, v_w + 2 * gk_w, v_w + 2 * gk_w + gv_w), glr, w2p,
                     gla_gate_b[i], gla_out_norm[i], bsz, s_len)

        mixed = _merge(h, y_da, y_gla, w_merge_gate[i].astype(BF16), b_merge_gate[i].reshape(1, 2 * d).astype(F32),
                       w_branch_da[i].astype(BF16), w_branch_gla[i].astype(BF16))
        x2d = _out_proj(x2d, mixed, w_out[i].astype(BF16))

        x2d = _moe(x2d, norm_ffn[i], router_w[i], router_b[i], w_gate_up[i], b_gate_up[i], w_down[i], b_down[i])

        x2d = _ple(x2d, p[i].reshape(t, p.shape[-1]), norm_ple[i], w_ple_gate[i].astype(BF16),
                   w_ple_up[i].astype(BF16), norm_ple_post[i])
    return x2d.reshape(bsz, s_len, d)
```

```python
import functools
import math

import jax
import jax.numpy as jnp
from jax import lax
from jax.experimental import pallas as pl
from jax.experimental.pallas import tpu as pltpu

F32 = jnp.float32
BF16 = jnp.bfloat16
HIGHEST = lax.Precision.HIGHEST

CHUNK = 64
CHUNK_SHIFT = CHUNK.bit_length() - 1
RMS_EPS = 1e-6
DA_HEADS = 8
DA_HEAD_DIM = 128
DA_V_DIM = 2 * DA_HEAD_DIM
GLA_HEADS = 4
GLA_GATE_RANK = 16
GLA_TAU = 16.0
N_EXPERTS = 32
TOP_K = 4
SWIGLU_LIMIT = 7.0
SWIGLU_ALPHA = 1.702

DA_TQ = 512

LANES = 128
NEG_BIG = -1e30

MIB = 1024 * 1024


def _cparams(sem, vmem_mib):
    return pltpu.CompilerParams(dimension_semantics=sem, vmem_limit_bytes=vmem_mib * MIB)


def _dot(a, b):
    return jnp.dot(a, b, preferred_element_type=F32)


def _dot_nt(a, b):
    return lax.dot_general(a, b, (((1,), (1,)), ((), ())), preferred_element_type=F32)


def _dot_tn(a, b):
    return lax.dot_general(a, b, (((0,), (0,)), ((), ())), preferred_element_type=F32)


def _rmsnorm_kernel(x_ref, g_ref, o_ref):
    x = x_ref[...]
    ms = jnp.mean(x * x, axis=-1, keepdims=True)
    o_ref[...] = (x * lax.rsqrt(ms + RMS_EPS) * g_ref[...]).astype(o_ref.dtype)


def _rmsnorm(x2d, gain, tm=512):
    t, d = x2d.shape
    return pl.pallas_call(
        _rmsnorm_kernel,
        grid=(t // tm,),
        in_specs=[pl.BlockSpec((tm, d), lambda i: (i, 0)), pl.BlockSpec((1, d), lambda i: (0, 0))],
        out_specs=pl.BlockSpec((tm, d), lambda i: (i, 0)),
        out_shape=jax.ShapeDtypeStruct((t, d), BF16),
        compiler_params=_cparams(("parallel",), 32),
        name="rmsnorm",
    )(x2d, gain.reshape(1, d).astype(F32))


def _proj_kernel(a_ref, w_ref, g_ref, o_ref, *, group_norm, scale):
    acc = _dot(a_ref[...], w_ref[...])
    if group_norm:
        for c in range(acc.shape[1] // DA_HEAD_DIM):
            blk = acc[:, c * DA_HEAD_DIM:(c + 1) * DA_HEAD_DIM]
            ms = jnp.mean(blk * blk, axis=-1, keepdims=True)
            y = blk * lax.rsqrt(ms + RMS_EPS) * g_ref[...] * scale
            o_ref[:, c * DA_HEAD_DIM:(c + 1) * DA_HEAD_DIM] = y.astype(o_ref.dtype)
    else:
        o_ref[...] = acc.astype(o_ref.dtype)


def _proj(a, w, *, out_dtype, tm, tn, gain=None, scale=1.0):
    t, k = a.shape
    n = w.shape[1]
    group_norm = gain is not None
    g = (gain if group_norm else jnp.ones((DA_HEAD_DIM,), F32)).reshape(1, DA_HEAD_DIM).astype(F32)
    return pl.pallas_call(
        functools.partial(_proj_kernel, group_norm=group_norm, scale=scale),
        grid=(n // tn, t // tm),
        in_specs=[
            pl.BlockSpec((tm, k), lambda j, i: (i, 0)),
            pl.BlockSpec((k, tn), lambda j, i: (0, j)),
            pl.BlockSpec((1, DA_HEAD_DIM), lambda j, i: (0, 0)),
        ],
        out_specs=pl.BlockSpec((tm, tn), lambda j, i: (i, j)),
        out_shape=jax.ShapeDtypeStruct((t, n), out_dtype),
        compiler_params=_cparams(("parallel", "parallel"), 48),
        name="proj",
    )(a, w, g)


def _proj_t_kernel(wt_ref, a_ref, o_ref):
    o_ref[0] = _dot_nt(wt_ref[...], a_ref[...]).astype(o_ref.dtype)


def _proj_t(a, wt, *, tm, tn):
    t, k = a.shape
    n = wt.shape[0]
    return pl.pallas_call(
        _proj_t_kernel,
        grid=(n // tn, t // tm),
        in_specs=[
            pl.BlockSpec((tn, k), lambda j, i: (j, 0)),
            pl.BlockSpec((tm, k), lambda j, i: (i, 0)),
        ],
        out_specs=pl.BlockSpec((1, tn, tm), lambda j, i: (i, j, 0)),
        out_shape=jax.ShapeDtypeStruct((t // tm, n, tm), BF16),
        compiler_params=_cparams(("parallel", "parallel"), 48),
        name="proj_t",
    )(wt, a)


def _da_kernel(slope_ref, q_ref, k_ref, vt_ref, lamv_ref, subg_ref, o_ref,
               kaug_ref, qaug_ref, m_ref, l_ref, acc_ref, *, tq, s_len, lambda_init):
    h = pl.program_id(1)
    i = pl.program_id(2)
    slope = slope_ref[h]
    hd = DA_HEAD_DIM

    @pl.when(i == 0)
    def _():
        pos = lax.broadcasted_iota(jnp.int32, (s_len, hd), 0)
        lane = lax.broadcasted_iota(jnp.int32, (s_len, hd), 1)
        piece = jnp.where(lane == 0, (pos >> 4) << 4, jnp.where(lane == 1, pos & 15, 0))
        piece = piece.astype(F32).astype(BF16)
        for c in range(2):
            kaug_ref[c, :, :hd] = k_ref[:, c * hd:(c + 1) * hd]
            kaug_ref[c, :, hd:] = piece

    lane_q = lax.broadcasted_iota(jnp.int32, (tq, hd), 1)
    slope_cols = jnp.where(lane_q < 2, slope, 0.0).astype(BF16)
    for c in range(2):
        qaug_ref[c, :, :hd] = q_ref[:, c * hd:(c + 1) * hd]
        qaug_ref[c, :, hd:] = slope_cols

    m_ref[...] = jnp.full(m_ref.shape, NEG_BIG, F32)
    l_ref[...] = jnp.zeros(l_ref.shape, F32)
    acc_ref[...] = jnp.zeros(acc_ref.shape, F32)

    def update(c, st, vt):
        m_old = m_ref[c]
        m_new = jnp.maximum(m_old, jnp.max(st, axis=0, keepdims=True))
        alpha = jnp.exp(m_old - m_new)
        p = jnp.exp(st - m_new)
        l_ref[c] = alpha * l_ref[c] + jnp.sum(p, axis=0, keepdims=True)
        acc_ref[c] = alpha * acc_ref[c] + _dot(vt, p.astype(BF16))
        m_ref[c] = m_new

    def past_tile(j, carry):
        r0 = pl.multiple_of(j * tq, tq)
        vt = vt_ref[j]
        for c in range(2):
            update(c, _dot_nt(kaug_ref[c, pl.ds(r0, tq), :], qaug_ref[c]), vt)
        return carry

    lax.fori_loop(0, i, past_tile, 0)

    r0 = pl.multiple_of(i * tq, tq)
    krow = lax.broadcasted_iota(jnp.int32, (tq, tq), 0)
    qcol = lax.broadcasted_iota(jnp.int32, (tq, tq), 1)
    corr = (-2.0 * slope) * jnp.maximum(krow - qcol, 0).astype(F32)
    allowed = (krow >> CHUNK_SHIFT) <= (qcol >> CHUNK_SHIFT)
    vt = vt_ref[i]
    for c in range(2):
        st = _dot_nt(kaug_ref[c, pl.ds(r0, tq), :], qaug_ref[c])
        update(c, jnp.where(allowed, st + corr, NEG_BIG), vt)

    lamv = lamv_ref[...]
    lam = (jnp.exp(jnp.sum(lamv[0:1] * lamv[1:2], axis=-1, keepdims=True))
           - jnp.exp(jnp.sum(lamv[2:3] * lamv[3:4], axis=-1, keepdims=True)) + lambda_init)
    ot = acc_ref[0] / l_ref[0] - lam * (acc_ref[1] / l_ref[1])
    ms = jnp.mean(ot * ot, axis=0, keepdims=True)
    o = (ot * lax.rsqrt(ms + RMS_EPS)).T
    o_ref[...] = (o * subg_ref[...] * (1.0 - lambda_init)).astype(o_ref.dtype)


def _diff_attention(qn, kn, vt, lam_vecs, subln_g, bsz, s_len, lambda_init, tq):
    assert s_len <= 4096 and s_len % tq == 0 and tq % CHUNK == 0
    t = bsz * s_len
    nq = s_len // tq
    slopes = jnp.exp2(-8.0 * jnp.arange(1, DA_HEADS + 1, dtype=F32) / DA_HEADS)
    grid_spec = pltpu.PrefetchScalarGridSpec(
        num_scalar_prefetch=1,
        grid=(bsz, DA_HEADS, nq),
        in_specs=[
            pl.BlockSpec((tq, DA_V_DIM), lambda b, h, i, s: (b * nq + i, h)),
            pl.BlockSpec((s_len, DA_V_DIM), lambda b, h, i, s: (b, h)),
            pl.BlockSpec((nq, DA_V_DIM, tq), lambda b, h, i, s: (b, h, 0)),
            pl.BlockSpec((4, DA_HEAD_DIM), lambda b, h, i, s: (0, 0)),
            pl.BlockSpec((1, DA_V_DIM), lambda b, h, i, s: (0, 0)),
        ],
        out_specs=pl.BlockSpec((tq, DA_V_DIM), lambda b, h, i, s: (b * nq + i, h)),
        scratch_shapes=[
            pltpu.VMEM((2, s_len, 2 * DA_HEAD_DIM), BF16),
            pltpu.VMEM((2, tq, 2 * DA_HEAD_DIM), BF16),
            pltpu.VMEM((2, 1, tq), F32),
            pltpu.VMEM((2, 1, tq), F32),
            pltpu.VMEM((2, DA_V_DIM, tq), F32),
        ],
    )
    return pl.pallas_call(
        functools.partial(_da_kernel, tq=tq, s_len=s_len, lambda_init=lambda_init),
        grid_spec=grid_spec,
        out_shape=jax.ShapeDtypeStruct((t, DA_HEADS * DA_V_DIM), BF16),
        compiler_params=_cparams(("parallel", "parallel", "arbitrary"), 48),
        name="diff_attention",
    )(slopes, qn, kn, vt, lam_vecs.astype(F32), subln_g.reshape(1, DA_V_DIM).astype(F32))


def _gla_kernel(q_ref, k_ref, v_ref, g_ref, glr_ref, w2_ref, gb_ref, og_ref, o_ref, state_ref,
                *, tb, dk, dv):
    @pl.when(pl.program_id(2) == 0)
    def _():
        state_ref[...] = jnp.zeros(state_ref.shape, F32)

    row = lax.broadcasted_iota(jnp.int32, (CHUNK, CHUNK), 0)
    col = lax.broadcasted_iota(jnp.int32, (CHUNK, CHUNK), 1)
    lower = row >= col
    tri = lower.astype(F32)
    mid = CHUNK // 2

    def chunk(c, carry):
        r0 = pl.multiple_of(c * CHUNK, CHUNK)
        z = jnp.dot(glr_ref[pl.ds(r0, CHUNK), :], w2_ref[...], precision=HIGHEST,
                    preferred_element_type=F32) + gb_ref[...]
        log_a = (jnp.minimum(z, 0.0) - jnp.log1p(jnp.exp(-jnp.abs(z)))) * (1.0 / GLA_TAU)
        b = jnp.dot(tri, log_a, precision=HIGHEST, preferred_element_type=F32)
        b_last = b[CHUNK - 1:CHUNK, :]
        b_mid = b[mid:mid + 1, :]
        q = q_ref[pl.ds(r0, CHUNK), :].astype(F32) * (dk ** -0.5)
        k = k_ref[pl.ds(r0, CHUNK), :].astype(F32)
        v = v_ref[pl.ds(r0, CHUNK), :]
        e_fwd = jnp.exp(b - b_mid)
        e_bwd = jnp.exp(b_mid - b)
        a_lo = _dot_nt((q * e_fwd).astype(BF16), (k * e_bwd).astype(BF16))
        a_up = _dot_nt((q * e_bwd).astype(BF16), (k * e_fwd).astype(BF16))
        attn = jnp.where(lower, a_lo, a_up)
        state = state_ref[...]
        o = _dot(attn.astype(BF16), v) + _dot_nt((q * jnp.exp(b)).astype(BF16), state.astype(BF16))
        kd = (k * jnp.exp(b_last - b)).astype(BF16)
        state_ref[...] = state * jnp.exp(b_last) + _dot_tn(v, kd)
        ms = jnp.mean(o * o, axis=-1, keepdims=True)
        g = g_ref[pl.ds(r0, CHUNK), :].astype(F32)
        y = o * lax.rsqrt(ms + RMS_EPS) * og_ref[...] * (g * jax.nn.sigmoid(g))
        o_ref[pl.ds(r0, CHUNK), :] = y.astype(o_ref.dtype)
        return carry

    lax.fori_loop(0, tb // CHUNK, chunk, 0)


def _gla(src, cols, glr, w2p, gate_b, out_g, bsz, s_len, tb=512):
    t = bsz * s_len
    nb = s_len // tb
    kw = w2p.shape[1]
    dk = kw // GLA_HEADS
    vw = out_g.shape[0] * GLA_HEADS
    dv = vw // GLA_HEADS
    cq, ck, cv, cg = cols
    row_map = lambda b, h, i: (b * nb + i)
    return pl.pallas_call(
        functools.partial(_gla_kernel, tb=tb, dk=dk, dv=dv),
        grid=(bsz, GLA_HEADS, nb),
        in_specs=[
            pl.BlockSpec((tb, dk), lambda b, h, i: (row_map(b, h, i), cq // dk + h)),
            pl.BlockSpec((tb, dk), lambda b, h, i: (row_map(b, h, i), ck // dk + h)),
            pl.BlockSpec((tb, dv), lambda b, h, i: (row_map(b, h, i), cv // dv + h)),
            pl.BlockSpec((tb, dv), lambda b, h, i: (row_map(b, h, i), cg // dv + h)),
            pl.BlockSpec((tb, LANES), lambda b, h, i: (row_map(b, h, i), 0)),
            pl.BlockSpec((LANES, dk), lambda b, h, i: (0, h)),
            pl.BlockSpec((1, dk), lambda b, h, i: (0, h)),
            pl.BlockSpec((1, dv), lambda b, h, i: (0, 0)),
        ],
        out_specs=pl.BlockSpec((tb, dv), lambda b, h, i: (row_map(b, h, i), h)),
        out_shape=jax.ShapeDtypeStruct((t, vw), BF16),
        scratch_shapes=[pltpu.VMEM((dv, dk), F32)],
        compiler_params=_cparams(("parallel", "parallel", "arbitrary"), 32),
        name="gla",
    )(src, src, src, src, glr, w2p, gate_b.reshape(1, kw).astype(F32), out_g.reshape(1, dv).astype(F32))


def _merge_kernel(h_ref, ya_ref, yb_ref, wga_ref, wgb_ref, ba_ref, bb_ref, wa_ref, wb_ref, o_ref):
    h = h_ref[...]
    ga = jax.nn.sigmoid(_dot(h, wga_ref[...]) + ba_ref[...])
    gb = jax.nn.sigmoid(_dot(h, wgb_ref[...]) + bb_ref[...])
    mixed = ga * _dot(ya_ref[...], wa_ref[...]) + gb * _dot(yb_ref[...], wb_ref[...])
    o_ref[...] = mixed.astype(o_ref.dtype)


def _merge(h, y_da, y_gla, w_gate, b_gate, w_da, w_gla, tm=512, tn=512):
    t, d = h.shape
    nb = d // tn
    act = lambda: pl.BlockSpec((tm, d), lambda j, i: (i, 0))
    return pl.pallas_call(
        _merge_kernel,
        grid=(nb, t // tm),
        in_specs=[
            act(), act(), act(),
            pl.BlockSpec((d, tn), lambda j, i: (0, j)),
            pl.BlockSpec((d, tn), lambda j, i: (0, nb + j)),
            pl.BlockSpec((1, tn), lambda j, i: (0, j)),
            pl.BlockSpec((1, tn), lambda j, i: (0, nb + j)),
            pl.BlockSpec((d, tn), lambda j, i: (0, j)),
            pl.BlockSpec((d, tn), lambda j, i: (0, j)),
        ],
        out_specs=pl.BlockSpec((tm, tn), lambda j, i: (i, j)),
        out_shape=jax.ShapeDtypeStruct((t, d), BF16),
        compiler_params=_cparams(("parallel", "parallel"), 48),
        name="merge",
    )(h, y_da, y_gla, w_gate, w_gate, b_gate, b_gate, w_da, w_gla)


def _out_proj_kernel(x_ref, m_ref, w_ref, o_ref):
    o_ref[...] = x_ref[...] + _dot(m_ref[...], w_ref[...])


def _out_proj(x2d, mixed, w_out, tm=256):
    t, d = x2d.shape
    return pl.pallas_call(
        _out_proj_kernel,
        grid=(t // tm,),
        in_specs=[
            pl.BlockSpec((tm, d), lambda i: (i, 0)),
            pl.BlockSpec((tm, d), lambda i: (i, 0)),
            pl.BlockSpec((d, d), lambda i: (0, 0)),
        ],
        out_specs=pl.BlockSpec((tm, d), lambda i: (i, 0)),
        out_shape=jax.ShapeDtypeStruct((t, d), F32),
        compiler_params=_cparams(("parallel",), 48),
        name="out_proj",
    )(x2d, mixed, w_out)


ROW_SUBLANES = 8


def _pack_rows(val, store):
    half = val.shape[1] // 2
    assert half == ROW_SUBLANES * LANES
    lo = lax.bitcast_convert_type(val[:, :half].astype(BF16).astype(F32), jnp.uint32)
    hi = lax.bitcast_convert_type(val[:, half:].astype(BF16).astype(F32), jnp.uint32)
    packed = (lo >> 16) | (hi & jnp.uint32(0xFFFF0000))
    for c in range(ROW_SUBLANES):
        store(c, packed[:, c * LANES:(c + 1) * LANES])


def _unpack_chunk(chunk):
    lo = lax.bitcast_convert_type(chunk << 16, F32)
    hi = lax.bitcast_convert_type(chunk & jnp.uint32(0xFFFF0000), F32)
    return lo, hi


def _router_kernel(x_ref, g_ref, rw_ref, rb_ref, hp_ref, idx_ref, gate_ref):
    x = x_ref[...]
    tm = x.shape[0]
    ms = jnp.mean(x * x, axis=-1, keepdims=True)
    h = x * lax.rsqrt(ms + RMS_EPS) * g_ref[...]

    def store(c, chunk):
        hp_ref[pl.ds(c, tm, stride=ROW_SUBLANES), :] = chunk

    _pack_rows(h, store)

    logits = jnp.dot(h, rw_ref[...], precision=HIGHEST, preferred_element_type=F32) + rb_ref[...]
    lane = lax.broadcasted_iota(jnp.int32, logits.shape, 1)
    vals, idxs = [], []
    for _ in range(TOP_K):
        m = jnp.max(logits, axis=-1, keepdims=True)
        idx = jnp.min(jnp.where(logits == m, lane, LANES), axis=-1, keepdims=True)
        vals.append(m)
        idxs.append(idx)
        logits = jnp.where(lane == idx, -jnp.inf, logits)
    exps = [jnp.exp(v - vals[0]) for v in vals]
    denom = exps[0] + exps[1] + exps[2] + exps[3]
    idx_out = jnp.zeros(lane.shape, jnp.int32)
    gate_out = jnp.zeros(lane.shape, F32)
    for k in range(TOP_K):
        idx_out = jnp.where(lane == k, idxs[k], idx_out)
        gate_out = jnp.where(lane == k, exps[k] / denom, gate_out)
    idx_ref[...] = idx_out
    gate_ref[...] = gate_out


def _router(x2d, gain, router_w, router_b, tm=256):
    t, d = x2d.shape
    e = router_w.shape[1]
    rw = jnp.zeros((d, LANES), F32).at[:, :e].set(router_w.astype(F32))
    rb = jnp.full((1, LANES), NEG_BIG, F32).at[0, :e].set(router_b.astype(F32))
    return pl.pallas_call(
        _router_kernel,
        grid=(t // tm,),
        in_specs=[
            pl.BlockSpec((tm, d), lambda i: (i, 0)),
            pl.BlockSpec((1, d), lambda i: (0, 0)),
            pl.BlockSpec((d, LANES), lambda i: (0, 0)),
            pl.BlockSpec((1, LANES), lambda i: (0, 0)),
        ],
        out_specs=[
            pl.BlockSpec((tm * ROW_SUBLANES, LANES), lambda i: (i, 0)),
            pl.BlockSpec((tm, LANES), lambda i: (i, 0)),
            pl.BlockSpec((tm, LANES), lambda i: (i, 0)),
        ],
        out_shape=[
            jax.ShapeDtypeStruct((t * ROW_SUBLANES, LANES), jnp.uint32),
            jax.ShapeDtypeStruct((t, LANES), jnp.int32),
            jax.ShapeDtypeStruct((t, LANES), F32),
        ],
        compiler_params=_cparams(("parallel",), 32),
        name="router",
    )(x2d, gain.reshape(1, d).astype(F32), rw, rb)


def _gather_kernel(idx_ref, src_ref, o_ref, sem, *, rows):
    def copy(r):
        return pltpu.make_async_copy(src_ref.at[idx_ref[0, 0, r]], o_ref.at[r], sem)

    def issue(r, carry):
        copy(r).start()
        return carry

    def wait(r, carry):
        copy(r).wait()
        return carry

    lax.fori_loop(0, rows, issue, 0, unroll=8)
    lax.fori_loop(0, rows, wait, 0, unroll=8)


def _gather_rows(src, row_idx, rows=256):
    n = row_idx.shape[0]
    tile = src.shape[1:]
    return pl.pallas_call(
        functools.partial(_gather_kernel, rows=rows),
        grid=(n // rows,),
        in_specs=[
            pl.BlockSpec((1, 1, rows), lambda i: (i, 0, 0), memory_space=pltpu.SMEM),
            pl.BlockSpec(memory_space=pl.ANY),
        ],
        out_specs=pl.BlockSpec((rows,) + tile, lambda i: (i, 0, 0)),
        out_shape=jax.ShapeDtypeStruct((n,) + tile, src.dtype),
        scratch_shapes=[pltpu.SemaphoreType.DMA(())],
        compiler_params=_cparams(("arbitrary",), 32),
        name="moe_gather",
    )(row_idx.reshape(n // rows, 1, rows), src)


MOE_SUB = 256
MOE_NSUB = 6
MOE_TF = 256


def _expert_kernel(ge_ref, gs_ref, gn_ref, ng_ref, tot_ref,
                   xs_ref, wgu_ref, bgu_ref, wdn_ref, bdn_ref, ys_ref,
                   xu_ref, xb_ref, acc_ref, wgu_b_ref, wdn_p_ref, wdn_b_ref, zero_ref, sem_in, sem_out,
                   *, n_sub_total):
    g = pl.program_id(0)
    f = pl.program_id(1)
    nf = pl.num_programs(1)
    active = g < ng_ref[0]
    nsub = gn_ref[g]
    start = gs_ref[g]
    half = xb_ref.shape[2] // 2
    tf = wdn_ref.shape[1]
    sub_words = MOE_SUB * ROW_SUBLANES

    def in_copy(s):
        r0 = pl.multiple_of((start + s) * sub_words, sub_words)
        return pltpu.make_async_copy(xs_ref.at[pl.ds(r0, sub_words)], xu_ref.at[s], sem_in.at[s])

    def out_copy(s):
        r0 = pl.multiple_of((start + s) * sub_words, sub_words)
        return pltpu.make_async_copy(xu_ref.at[s], ys_ref.at[pl.ds(r0, sub_words)], sem_out)

    @pl.when(jnp.logical_and(g == 0, f == 0))
    def _():
        zero_ref[...] = jnp.zeros(zero_ref.shape, zero_ref.dtype)

    @pl.when(jnp.logical_and(active, f == 0))
    def _():
        def issue(s, c):
            in_copy(s).start()
            return c

        def unpack(s, c):
            in_copy(s).wait()
            for ch in range(ROW_SUBLANES):
                lo, hi = _unpack_chunk(xu_ref[s, pl.ds(ch, MOE_SUB, stride=ROW_SUBLANES), :])
                xb_ref[s, :, ch * LANES:(ch + 1) * LANES] = lo.astype(BF16)
                xb_ref[s, :, half + ch * LANES:half + (ch + 1) * LANES] = hi.astype(BF16)
            acc_ref[s] = jnp.broadcast_to(bdn_ref[0], acc_ref.shape[1:])
            return c

        lax.fori_loop(0, nsub, issue, 0)
        lax.fori_loop(0, nsub, unpack, 0)

    @pl.when(active)
    def _():
        wgu_b_ref[...] = wgu_ref[0].astype(BF16)
        hl = LANES // 2
        for c in range(wdn_p_ref.shape[0]):
            cols = slice(c * LANES, (c + 1) * LANES)
            for m in range(tf // LANES):
                wdn_p_ref[c, pl.ds(m * LANES, hl, stride=2), :] = wdn_ref[0, m * LANES:m * LANES + hl, cols]
                wdn_p_ref[c, pl.ds(m * LANES + 1, hl, stride=2), :] = (
                    wdn_ref[0, m * LANES + hl:(m + 1) * LANES, cols])
            wdn_b_ref[:, cols] = wdn_p_ref[c].astype(BF16)
        bgu = bgu_ref[0]
        lane = lax.broadcasted_iota(jnp.int32, (MOE_SUB, LANES), 1)
        even = (lane & 1) == 0

        def gate_up(s):
            gu = _dot(xb_ref[s], wgu_b_ref[...]) + bgu
            gates, ups = [], []
            for m in range(tf // LANES):
                a = gu[:, 2 * m * LANES:(2 * m + 1) * LANES]
                b = gu[:, (2 * m + 1) * LANES:(2 * m + 2) * LANES]
                gates.append(jnp.where(even, a, pltpu.roll(b, 1, 1)))
                ups.append(jnp.where(even, pltpu.roll(a, LANES - 1, 1), b))
            gate = jnp.minimum(jnp.concatenate(gates, axis=1), SWIGLU_LIMIT)
            up = jnp.clip(jnp.concatenate(ups, axis=1), -SWIGLU_LIMIT, SWIGLU_LIMIT)
            return ((up + 1.0) * gate * jax.nn.sigmoid(SWIGLU_ALPHA * gate)).astype(BF16)

        def down(s, act):
            acc_ref[s] += _dot(act, wdn_b_ref[...])

        def body(s, act_prev):
            act = gate_up(s)
            down(s - 1, act_prev)
            return act

        act_last = lax.fori_loop(1, nsub, body, gate_up(0))
        down(nsub - 1, act_last)

    @pl.when(jnp.logical_and(active, f == nf - 1))
    def _():
        def issue(s, c):
            def store(ch, chunk):
                xu_ref[s, pl.ds(ch, MOE_SUB, stride=ROW_SUBLANES), :] = chunk

            _pack_rows(acc_ref[s], store)
            out_copy(s).start()
            return c

        def wait(s, c):
            out_copy(s).wait()
            return c

        lax.fori_loop(0, nsub, issue, 0)
        lax.fori_loop(0, nsub, wait, 0)

    @pl.when(jnp.logical_and(g == pl.num_programs(0) - 1, f == nf - 1))
    def _():
        def fill(s, c):
            r0 = pl.multiple_of(s * sub_words, sub_words)
            cp = pltpu.make_async_copy(zero_ref, ys_ref.at[pl.ds(r0, sub_words)], sem_out)
            cp.start()
            cp.wait()
            return c

        lax.fori_loop(tot_ref[0], n_sub_total, fill, 0)


def _experts(xs, grp_expert, grp_start, grp_nsub, n_groups, tot_sub, w_gu, b_gu, w_dn, b_dn):
    n_rows = xs.shape[0] // ROW_SUBLANES
    sub_words = MOE_SUB * ROW_SUBLANES
    e, d, f2 = w_gu.shape
    fdim = f2 // 2
    tf = MOE_TF
    nf = fdim // tf
    n_grp = grp_expert.shape[0]

    def wsel(g, f, ge, gs, gn, ng, tot):
        on = g < ng[0]
        last = jnp.maximum(ng[0] - 1, 0)
        return jnp.where(on, ge[g], ge[last]), jnp.where(on, f, nf - 1)

    def w_gu_map(g, f, *pre):
        ee, ff = wsel(g, f, *pre)
        return (ee, 0, ff)

    def w_dn_map(g, f, *pre):
        ee, ff = wsel(g, f, *pre)
        return (ee, ff, 0)

    def b_dn_map(g, f, *pre):
        ee, _ = wsel(g, f, *pre)
        return (ee, 0, 0)

    grid_spec = pltpu.PrefetchScalarGridSpec(
        num_scalar_prefetch=5,
        grid=(n_grp, nf),
        in_specs=[
            pl.BlockSpec(memory_space=pl.ANY),
            pl.BlockSpec((1, d, 2 * tf), w_gu_map),
            pl.BlockSpec((1, 1, 2 * tf), w_gu_map),
            pl.BlockSpec((1, tf, d), w_dn_map),
            pl.BlockSpec((1, 1, d), b_dn_map),
        ],
        out_specs=pl.BlockSpec(memory_space=pl.ANY),
        scratch_shapes=[
            pltpu.VMEM((MOE_NSUB, sub_words, LANES), jnp.uint32),
            pltpu.VMEM((MOE_NSUB, MOE_SUB, d), BF16),
            pltpu.VMEM((MOE_NSUB, MOE_SUB, d), F32),
            pltpu.VMEM((d, 2 * tf), BF16),
            pltpu.VMEM((d // LANES, tf, LANES), F32),
            pltpu.VMEM((tf, d), BF16),
            pltpu.VMEM((sub_words, LANES), jnp.uint32),
            pltpu.SemaphoreType.DMA((MOE_NSUB,)),
            pltpu.SemaphoreType.DMA(()),
        ],
    )
    return pl.pallas_call(
        functools.partial(_expert_kernel, n_sub_total=n_rows // MOE_SUB),
        grid_spec=grid_spec,
        out_shape=jax.ShapeDtypeStruct(xs.shape, jnp.uint32),
        compiler_params=_cparams(("arbitrary", "arbitrary"), 56),
        name="moe_experts",
    )(grp_expert, grp_start, grp_nsub, n_groups, tot_sub, xs,
      w_gu, b_gu.reshape(e, 1, f2).astype(F32), w_dn, b_dn.reshape(e, 1, d).astype(F32))


def _combine_kernel(pr_ref, gate_ref, x_ref, ys_ref, o_ref, buf_ref, sem, *, tc):
    def copy(p):
        tok = p >> 2
        return pltpu.make_async_copy(
            ys_ref.at[pr_ref[0, 0, p]],
            buf_ref.at[p & (TOP_K - 1), pl.ds(pl.multiple_of(tok * ROW_SUBLANES, ROW_SUBLANES), ROW_SUBLANES)],
            sem)

    def issue(p, carry):
        copy(p).start()
        return carry

    def wait(p, carry):
        copy(p).wait()
        return carry

    lax.fori_loop(0, tc * TOP_K, issue, 0, unroll=8)
    lax.fori_loop(0, tc * TOP_K, wait, 0, unroll=8)
    gates = gate_ref[...]
    half = x_ref.shape[1] // 2
    for ch in range(ROW_SUBLANES):
        lo_cols = slice(ch * LANES, (ch + 1) * LANES)
        hi_cols = slice(half + ch * LANES, half + (ch + 1) * LANES)
        acc_lo = x_ref[:, lo_cols]
        acc_hi = x_ref[:, hi_cols]
        for k in range(TOP_K):
            lo, hi = _unpack_chunk(buf_ref[k, pl.ds(ch, tc, stride=ROW_SUBLANES), :])
            acc_lo = acc_lo + gates[:, k:k + 1] * lo
            acc_hi = acc_hi + gates[:, k:k + 1] * hi
        o_ref[:, lo_cols] = acc_lo
        o_ref[:, hi_cols] = acc_hi


def _combine(x2d, gates, pair_row, ys, tc=128):
    t, d = x2d.shape
    return pl.pallas_call(
        functools.partial(_combine_kernel, tc=tc),
        grid=(t // tc,),
        in_specs=[
            pl.BlockSpec((1, 1, tc * TOP_K), lambda i: (i, 0, 0), memory_space=pltpu.SMEM),
            pl.BlockSpec((tc, LANES), lambda i: (i, 0)),
            pl.BlockSpec((tc, d), lambda i: (i, 0)),
            pl.BlockSpec(memory_space=pl.ANY),
        ],
        out_specs=pl.BlockSpec((tc, d), lambda i: (i, 0)),
        out_shape=jax.ShapeDtypeStruct((t, d), F32),
        scratch_shapes=[pltpu.VMEM((TOP_K, tc * ROW_SUBLANES, LANES), jnp.uint32), pltpu.SemaphoreType.DMA(())],
        compiler_params=_cparams(("arbitrary",), 32),
        name="moe_combine",
    )(pair_row.reshape(t // tc, 1, tc * TOP_K), gates, x2d, ys)


def _moe(x2d, gain, router_w, router_b, w_gu, b_gu, w_dn, b_dn):
    t, d = x2d.shape
    e = router_w.shape[1]
    hp, idx, gates = _router(x2d, gain, router_w, router_b)

    n_pairs = t * TOP_K
    flat_e = idx[:, :TOP_K].reshape(n_pairs)
    onehot = (flat_e[:, None] == jnp.arange(e, dtype=jnp.int32)[None, :]).astype(jnp.int32)
    csum = jnp.cumsum(onehot, axis=0)
    rank = jnp.sum(onehot * csum, axis=1) - 1
    counts = csum[-1]
    sub_e = (counts + MOE_SUB - 1) // MOE_SUB
    sub_end = jnp.cumsum(sub_e)
    sub_start = sub_end - sub_e
    dest = (jnp.sum(onehot * (sub_start * MOE_SUB)[None, :], axis=1) + rank).astype(jnp.int32)
    n_sub_total = n_pairs // MOE_SUB + e
    n_rows = n_sub_total * MOE_SUB
    row_tok = jnp.full((n_rows,), t, jnp.int32).at[dest].set(jnp.arange(n_pairs, dtype=jnp.int32) // TOP_K)

    grp_e = (sub_e + MOE_NSUB - 1) // MOE_NSUB
    grp_end = jnp.cumsum(grp_e)
    n_grp_max = (n_sub_total + MOE_NSUB - 1) // MOE_NSUB + e
    gidx = jnp.arange(n_grp_max, dtype=jnp.int32)
    g_exp = jnp.minimum(jnp.searchsorted(grp_end, gidx, side="right"), e - 1).astype(jnp.int32)
    g_local = gidx - (grp_end - grp_e)[g_exp]
    g_on = gidx < grp_end[-1]
    g_start = jnp.where(g_on, sub_start[g_exp] + g_local * MOE_NSUB, 0).astype(jnp.int32)
    g_nsub = jnp.where(g_on, jnp.minimum(MOE_NSUB, sub_e[g_exp] - g_local * MOE_NSUB), 0).astype(jnp.int32)

    hp_pad = jnp.concatenate([hp, jnp.zeros((ROW_SUBLANES, LANES), hp.dtype)], axis=0)
    xs = _gather_rows(hp_pad.reshape(t + 1, ROW_SUBLANES, LANES), row_tok)
    ys = _experts(xs.reshape(n_rows * ROW_SUBLANES, LANES), g_exp, g_start, g_nsub,
                  grp_end[-1].astype(jnp.int32).reshape(1), sub_end[-1].astype(jnp.int32).reshape(1),
                  w_gu, b_gu, w_dn, b_dn)
    return _combine(x2d, gates, dest, ys.reshape(n_rows, ROW_SUBLANES, LANES))


def _ple_kernel(x_ref, p_ref, gn_ref, wg_ref, wu_ref, gp_ref, o_ref):
    x = x_ref[...]
    ms = jnp.mean(x * x, axis=-1, keepdims=True)
    h = (x * lax.rsqrt(ms + RMS_EPS) * gn_ref[...]).astype(BF16)
    gate = jax.nn.sigmoid(_dot(h, wg_ref[...]))
    up = _dot(p_ref[...].astype(BF16), wu_ref[...])
    ms_u = jnp.mean(up * up, axis=-1, keepdims=True)
    o_ref[...] = x + up * lax.rsqrt(ms_u + RMS_EPS) * gp_ref[...] * gate


def _ple(x2d, p2d, g_norm, w_gate, w_up, g_post, tm=256):
    t, d = x2d.shape
    pd = p2d.shape[1]
    return pl.pallas_call(
        _ple_kernel,
        grid=(t // tm,),
        in_specs=[
            pl.BlockSpec((tm, d), lambda i: (i, 0)),
            pl.BlockSpec((tm, pd), lambda i: (i, 0)),
            pl.BlockSpec((1, d), lambda i: (0, 0)),
            pl.BlockSpec((d, d), lambda i: (0, 0)),
            pl.BlockSpec((pd, d), lambda i: (0, 0)),
            pl.BlockSpec((1, d), lambda i: (0, 0)),
        ],
        out_specs=pl.BlockSpec((tm, d), lambda i: (i, 0)),
        out_shape=jax.ShapeDtypeStruct((t, d), F32),
        compiler_params=_cparams(("parallel",), 48),
        name="ple",
    )(x2d, p2d, g_norm.reshape(1, d).astype(F32), w_gate, w_up, g_post.reshape(1, d).astype(F32))


def kernel(x, p, w_in, da_q_norm, da_k_norm, da_lambda_q1, da_lambda_k1, da_lambda_q2, da_lambda_k2, da_subln, gla_gate_w2, gla_gate_b, gla_out_norm, w_branch_da, w_branch_gla, w_merge_gate, b_merge_gate, w_out, norm_mix, norm_ffn, router_w, router_b, w_gate_up, b_gate_up, w_down, b_down, norm_ple, w_ple_gate, w_ple_up, norm_ple_post):
    bsz, s_len, d = x.shape
    t = bsz * s_len
    depth = w_in.shape[0]
    qk_w = DA_HEADS * 2 * DA_HEAD_DIM
    v_w = DA_HEADS * DA_V_DIM
    gk_w = gla_gate_w2.shape[2]
    gv_w = w_branch_gla.shape[1]
    rest_w = 2 * gk_w + 2 * gv_w
    x2d = x.reshape(t, d)
    for i in range(depth):
        lambda_init = 0.8 - 0.6 * math.exp(-0.3 * i)
        h = _rmsnorm(x2d, norm_mix[i])
        w = w_in[i]
        c_v = 2 * qk_w
        c_rest = c_v + v_w
        qn = _proj(h, w[:, :qk_w].astype(BF16), out_dtype=BF16, tm=512, tn=1024,
                   gain=da_q_norm[i], scale=DA_HEAD_DIM ** -0.5)
        kn = _proj(h, w[:, qk_w:c_v].astype(BF16), out_dtype=BF16, tm=512, tn=1024, gain=da_k_norm[i])
        vt = _proj_t(h, w[:, c_v:c_rest].T.astype(BF16), tm=DA_TQ, tn=1024)
        rest = _proj(h, w[:, c_rest:c_rest + rest_w].astype(BF16), out_dtype=BF16, tm=512, tn=1024)
        w_lr = jnp.zeros((d, LANES), BF16).at[:, :GLA_GATE_RANK].set(w[:, c_rest + rest_w:].astype(BF16))
        glr = _proj(h, w_lr, out_dtype=F32, tm=512, tn=LANES)

        lam_vecs = jnp.stack([da_lambda_q1[i], da_lambda_k1[i], da_lambda_q2[i], da_lambda_k2[i]])
        y_da = _diff_attention(qn, kn, vt, lam_vecs, da_subln[i], bsz, s_len, lambda_init, DA_TQ)

        w2p = jnp.zeros((LANES, gk_w), F32).at[:GLA_GATE_RANK].set(gla_gate_w2[i].astype(F32))
        y_gla = _gla(rest, (0, gk_w, 2 * gk_w, 2 * gk_w + gv_w), glr, w2p,
                     gla_gate_b[i], gla_out_norm[i], bsz, s_len)

        mixed = _merge(h, y_da, y_gla, w_merge_gate[i].astype(BF16), b_merge_gate[i].reshape(1, 2 * d).astype(F32),
                       w_branch_da[i].astype(BF16), w_branch_gla[i].astype(BF16))
        x2d = _out_proj(x2d, mixed, w_out[i].astype(BF16))

        x2d = _moe(x2d, norm_ffn[i], router_w[i], router_b[i], w_gate_up[i], b_gate_up[i], w_down[i], b_down[i])

        x2d = _ple(x2d, p[i].reshape(t, p.shape[-1]), norm_ple[i], w_ple_gate[i].astype(BF16),
                   w_ple_up[i].astype(BF16), norm_ple_post[i])
    return x2d.reshape(bsz, s_len, d)
```

```python
import functools
import math

import jax
import jax.numpy as jnp
from jax import lax
from jax.experimental import pallas as pl
from jax.experimental.pallas import tpu as pltpu

F32 = jnp.float32
BF16 = jnp.bfloat16
HIGHEST = lax.Precision.HIGHEST

CHUNK = 64
CHUNK_SHIFT = CHUNK.bit_length() - 1
RMS_EPS = 1e-6
DA_HEADS = 8
DA_HEAD_DIM = 128
DA_V_DIM = 2 * DA_HEAD_DIM
GLA_HEADS = 4
GLA_GATE_RANK = 16
GLA_TAU = 16.0
N_EXPERTS = 32
TOP_K = 4
SWIGLU_LIMIT = 7.0
SWIGLU_ALPHA = 1.702

DA_TQ = 512
DA_ONES_ROWS = 16
LOG2E = 1.4426950408889634

LANES = 128
NEG_BIG = -1e30

MIB = 1024 * 1024


def _cparams(sem, vmem_mib):
    return pltpu.CompilerParams(dimension_semantics=sem, vmem_limit_bytes=vmem_mib * MIB)


def _dot(a, b):
    return jnp.dot(a, b, preferred_element_type=F32)


def _dot_nt(a, b):
    return lax.dot_general(a, b, (((1,), (1,)), ((), ())), preferred_element_type=F32)


def _dot_tn(a, b):
    return lax.dot_general(a, b, (((0,), (0,)), ((), ())), preferred_element_type=F32)


def _rmsnorm_kernel(x_ref, g_ref, o_ref):
    x = x_ref[...]
    ms = jnp.mean(x * x, axis=-1, keepdims=True)
    o_ref[...] = (x * lax.rsqrt(ms + RMS_EPS) * g_ref[...]).astype(o_ref.dtype)


def _rmsnorm(x2d, gain, tm=512):
    t, d = x2d.shape
    return pl.pallas_call(
        _rmsnorm_kernel,
        grid=(t // tm,),
        in_specs=[pl.BlockSpec((tm, d), lambda i: (i, 0)), pl.BlockSpec((1, d), lambda i: (0, 0))],
        out_specs=pl.BlockSpec((tm, d), lambda i: (i, 0)),
        out_shape=jax.ShapeDtypeStruct((t, d), BF16),
        compiler_params=_cparams(("parallel",), 32),
        name="rmsnorm",
    )(x2d, gain.reshape(1, d).astype(F32))


def _proj_kernel(a_ref, w_ref, g_ref, o_ref, *, group_norm, scale):
    acc = _dot(a_ref[...], w_ref[...])
    if group_norm:
        for c in range(acc.shape[1] // DA_HEAD_DIM):
            blk = acc[:, c * DA_HEAD_DIM:(c + 1) * DA_HEAD_DIM]
            ms = jnp.mean(blk * blk, axis=-1, keepdims=True)
            y = blk * lax.rsqrt(ms + RMS_EPS) * g_ref[...] * scale
            o_ref[:, c * DA_HEAD_DIM:(c + 1) * DA_HEAD_DIM] = y.astype(o_ref.dtype)
    else:
        o_ref[...] = acc.astype(o_ref.dtype)


def _proj(a, w, *, out_dtype, tm, tn, gain=None, scale=1.0):
    t, k = a.shape
    n = w.shape[1]
    group_norm = gain is not None
    g = (gain if group_norm else jnp.ones((DA_HEAD_DIM,), F32)).reshape(1, DA_HEAD_DIM).astype(F32)
    return pl.pallas_call(
        functools.partial(_proj_kernel, group_norm=group_norm, scale=scale),
        grid=(n // tn, t // tm),
        in_specs=[
            pl.BlockSpec((tm, k), lambda j, i: (i, 0)),
            pl.BlockSpec((k, tn), lambda j, i: (0, j)),
            pl.BlockSpec((1, DA_HEAD_DIM), lambda j, i: (0, 0)),
        ],
        out_specs=pl.BlockSpec((tm, tn), lambda j, i: (i, j)),
        out_shape=jax.ShapeDtypeStruct((t, n), out_dtype),
        compiler_params=_cparams(("parallel", "parallel"), 48),
        name="proj",
    )(a, w, g)


def _proj_t_kernel(wt_ref, a_ref, o_ref):
    o_ref[0] = _dot_nt(wt_ref[...], a_ref[...]).astype(o_ref.dtype)


def _proj_t(a, wt, *, tm, tn):
    t, k = a.shape
    n = wt.shape[0]
    return pl.pallas_call(
        _proj_t_kernel,
        grid=(n // tn, t // tm),
        in_specs=[
            pl.BlockSpec((tn, k), lambda j, i: (j, 0)),
            pl.BlockSpec((tm, k), lambda j, i: (i, 0)),
        ],
        out_specs=pl.BlockSpec((1, tn, tm), lambda j, i: (i, j, 0)),
        out_shape=jax.ShapeDtypeStruct((t // tm, n, tm), BF16),
        compiler_params=_cparams(("parallel", "parallel"), 48),
        name="proj_t",
    )(wt, a)


def _da_kernel(slope_ref, q_ref, k_ref, vt_ref, lamv_ref, subg_ref, o_ref,
               kaug_ref, qaug_ref, vaug_ref, m_ref, acc_ref, *, tq, s_len, lambda_init):
    h = pl.program_id(1)
    i = pl.program_id(2)
    slope2 = slope_ref[h] * LOG2E
    hd = DA_HEAD_DIM
    dv = DA_V_DIM
    nq = vt_ref.shape[0]

    @pl.when(i == 0)
    def _():
        pos = lax.broadcasted_iota(jnp.int32, (s_len, hd), 0)
        lane = lax.broadcasted_iota(jnp.int32, (s_len, hd), 1)
        piece = jnp.where(lane < 6, jnp.where((lane & 1) == 0, (pos >> 4) << 4, pos & 15), 0)
        piece = piece.astype(F32).astype(BF16)
        for c in range(2):
            kaug_ref[c, :, :hd] = k_ref[:, c * hd:(c + 1) * hd]
            kaug_ref[c, :, hd:] = piece
        for j in range(nq):
            vaug_ref[j, :dv, :] = vt_ref[j]
            vaug_ref[j, dv:, :] = jnp.ones((vaug_ref.shape[1] - dv, tq), BF16)

    lane_q = lax.broadcasted_iota(jnp.int32, (tq, hd), 1)
    s_full = jnp.full((tq, hd), slope2, F32)
    s1 = s_full.astype(BF16).astype(F32)
    r1 = s_full - s1
    s2 = r1.astype(BF16).astype(F32)
    s3 = r1 - s2
    slope_cols = jnp.where(lane_q < 2, s1, jnp.where(lane_q < 4, s2, jnp.where(lane_q < 6, s3, 0.0)))
    slope_cols = slope_cols.astype(BF16)
    for c in range(2):
        qaug_ref[c, :, :hd] = q_ref[:, c * hd:(c + 1) * hd]
        qaug_ref[c, :, hd:] = slope_cols

    m_ref[...] = jnp.full(m_ref.shape, NEG_BIG, F32)
    acc_ref[...] = jnp.zeros(acc_ref.shape, F32)

    def update(c, st, va):
        m_old = m_ref[c]
        m_new = jnp.maximum(m_old, jnp.max(st, axis=0, keepdims=True))
        alpha = jnp.exp2(m_old - m_new)
        p = jnp.exp2(st - m_new)
        acc_ref[c] = alpha * acc_ref[c] + _dot(va, p.astype(BF16))
        m_ref[c] = m_new

    def past_tile(j, carry):
        r0 = pl.multiple_of(j * tq, tq)
        va = vaug_ref[j]
        for c in range(2):
            update(c, _dot_nt(kaug_ref[c, pl.ds(r0, tq), :], qaug_ref[c]), va)
        return carry

    lax.fori_loop(0, i, past_tile, 0)

    r0 = pl.multiple_of(i * tq, tq)
    krow = lax.broadcasted_iota(jnp.int32, (tq, tq), 0)
    qcol = lax.broadcasted_iota(jnp.int32, (tq, tq), 1)
    corr = (-2.0 * slope2) * jnp.maximum(krow - qcol, 0).astype(F32)
    allowed = (krow >> CHUNK_SHIFT) <= (qcol >> CHUNK_SHIFT)
    va = vaug_ref[i]
    for c in range(2):
        st = _dot_nt(kaug_ref[c, pl.ds(r0, tq), :], qaug_ref[c])
        update(c, jnp.where(allowed, st + corr, NEG_BIG), va)

    lamv = lamv_ref[...]
    lam = (jnp.exp(jnp.sum(lamv[0:1] * lamv[1:2], axis=-1, keepdims=True))
           - jnp.exp(jnp.sum(lamv[2:3] * lamv[3:4], axis=-1, keepdims=True)) + lambda_init)
    ot = (acc_ref[0, :dv, :] / acc_ref[0, dv:dv + 1, :]
          - lam * (acc_ref[1, :dv, :] / acc_ref[1, dv:dv + 1, :]))
    ms = jnp.mean(ot * ot, axis=0, keepdims=True)
    o = (ot * lax.rsqrt(ms + RMS_EPS)).T
    o_ref[...] = (o * subg_ref[...] * (1.0 - lambda_init)).astype(o_ref.dtype)


def _diff_attention(qn, kn, vt, lam_vecs, subln_g, bsz, s_len, lambda_init, tq):
    assert s_len <= 4096 and s_len % tq == 0 and tq % CHUNK == 0
    t = bsz * s_len
    nq = s_len // tq
    slopes = jnp.exp2(-8.0 * jnp.arange(1, DA_HEADS + 1, dtype=F32) / DA_HEADS)
    grid_spec = pltpu.PrefetchScalarGridSpec(
        num_scalar_prefetch=1,
        grid=(bsz, DA_HEADS, nq),
        in_specs=[
            pl.BlockSpec((tq, DA_V_DIM), lambda b, h, i, s: (b * nq + i, h)),
            pl.BlockSpec((s_len, DA_V_DIM), lambda b, h, i, s: (b, h)),
            pl.BlockSpec((nq, DA_V_DIM, tq), lambda b, h, i, s: (b, h, 0)),
            pl.BlockSpec((4, DA_HEAD_DIM), lambda b, h, i, s: (0, 0)),
            pl.BlockSpec((1, DA_V_DIM), lambda b, h, i, s: (0, 0)),
        ],
        out_specs=pl.BlockSpec((tq, DA_V_DIM), lambda b, h, i, s: (b * nq + i, h)),
        scratch_shapes=[
            pltpu.VMEM((2, s_len, 2 * DA_HEAD_DIM), BF16),
            pltpu.VMEM((2, tq, 2 * DA_HEAD_DIM), BF16),
            pltpu.VMEM((nq, DA_V_DIM + DA_ONES_ROWS, tq), BF16),
            pltpu.VMEM((2, 1, tq), F32),
            pltpu.VMEM((2, DA_V_DIM + DA_ONES_ROWS, tq), F32),
        ],
    )
    return pl.pallas_call(
        functools.partial(_da_kernel, tq=tq, s_len=s_len, lambda_init=lambda_init),
        grid_spec=grid_spec,
        out_shape=jax.ShapeDtypeStruct((t, DA_HEADS * DA_V_DIM), BF16),
        compiler_params=_cparams(("parallel", "parallel", "arbitrary"), 48),
        name="diff_attention",
    )(slopes, qn, kn, vt, lam_vecs.astype(F32), subln_g.reshape(1, DA_V_DIM).astype(F32))


def _gla_kernel(q_ref, k_ref, v_ref, g_ref, glr_ref, w2_ref, gb_ref, og_ref, o_ref, state_ref,
                *, tb, dk, dv):
    @pl.when(pl.program_id(1) == 0)
    def _():
        state_ref[...] = jnp.zeros(state_ref.shape, F32)

    row = lax.broadcasted_iota(jnp.int32, (CHUNK, CHUNK), 0)
    col = lax.broadcasted_iota(jnp.int32, (CHUNK, CHUNK), 1)
    lower = row >= col
    tri = lower.astype(F32)
    mid = CHUNK // 2

    def chunk(c, carry):
        r0 = pl.multiple_of(c * CHUNK, CHUNK)
        z = jnp.dot(glr_ref[pl.ds(r0, CHUNK), :], w2_ref[...], precision=HIGHEST,
                    preferred_element_type=F32) + gb_ref[...]
        log_a = (jnp.minimum(z, 0.0) - jnp.log1p(jnp.exp(-jnp.abs(z)))) * (1.0 / GLA_TAU)
        b_all = jnp.dot(tri, log_a, precision=HIGHEST, preferred_element_type=F32)
        for h in range(GLA_HEADS):
            ks = slice(h * dk, (h + 1) * dk)
            vs = slice(h * dv, (h + 1) * dv)
            b = b_all[:, ks]
            b_last = b[CHUNK - 1:CHUNK, :]
            b_mid = b[mid:mid + 1, :]
            q = q_ref[pl.ds(r0, CHUNK), ks].astype(F32) * (dk ** -0.5)
            k = k_ref[pl.ds(r0, CHUNK), ks].astype(F32)
            v = v_ref[pl.ds(r0, CHUNK), vs]
            e_fwd = jnp.exp(b - b_mid)
            e_bwd = jnp.exp(b_mid - b)
            a_lo = _dot_nt((q * e_fwd).astype(BF16), (k * e_bwd).astype(BF16))
            a_up = _dot_nt((q * e_bwd).astype(BF16), (k * e_fwd).astype(BF16))
            attn = jnp.where(lower, a_lo, a_up)
            state = state_ref[h]
            o = _dot(attn.astype(BF16), v) + _dot_nt((q * jnp.exp(b)).astype(BF16), state.astype(BF16))
            kd = (k * jnp.exp(b_last - b)).astype(BF16)
            state_ref[h] = state * jnp.exp(b_last) + _dot_tn(v, kd)
            ms = jnp.mean(o * o, axis=-1, keepdims=True)
            g = g_ref[pl.ds(r0, CHUNK), vs].astype(F32)
            y = o * lax.rsqrt(ms + RMS_EPS) * og_ref[...] * (g * jax.nn.sigmoid(g))
            o_ref[pl.ds(r0, CHUNK), vs] = y.astype(o_ref.dtype)
        return carry

    lax.fori_loop(0, tb // CHUNK, chunk, 0)


def _gla(src, cols, glr, w2p, gate_b, out_g, bsz, s_len, tb=512):
    t = bsz * s_len
    nb = s_len // tb
    kw = w2p.shape[1]
    dk = kw // GLA_HEADS
    dv = out_g.shape[0]
    vw = dv * GLA_HEADS
    cq, ck, cv, cg = cols
    return pl.pallas_call(
        functools.partial(_gla_kernel, tb=tb, dk=dk, dv=dv),
        grid=(bsz, nb),
        in_specs=[
            pl.BlockSpec((tb, kw), lambda b, i: (b * nb + i, cq // kw)),
            pl.BlockSpec((tb, kw), lambda b, i: (b * nb + i, ck // kw)),
            pl.BlockSpec((tb, vw), lambda b, i: (b * nb + i, cv // vw)),
            pl.BlockSpec((tb, vw), lambda b, i: (b * nb + i, cg // vw)),
            pl.BlockSpec((tb, LANES), lambda b, i: (b * nb + i, 0)),
            pl.BlockSpec((LANES, kw), lambda b, i: (0, 0)),
            pl.BlockSpec((1, kw), lambda b, i: (0, 0)),
            pl.BlockSpec((1, dv), lambda b, i: (0, 0)),
        ],
        out_specs=pl.BlockSpec((tb, vw), lambda b, i: (b * nb + i, 0)),
        out_shape=jax.ShapeDtypeStruct((t, vw), BF16),
        scratch_shapes=[pltpu.VMEM((GLA_HEADS, dv, dk), F32)],
        compiler_params=_cparams(("parallel", "arbitrary"), 48),
        name="gla",
    )(src, src, src, src, glr, w2p, gate_b.reshape(1, kw).astype(F32), out_g.reshape(1, dv).astype(F32))


def _merge_kernel(h_ref, ya_ref, yb_ref, wga_ref, wgb_ref, ba_ref, bb_ref, wa_ref, wb_ref, o_ref):
    h = h_ref[...]
    ga = jax.nn.sigmoid(_dot(h, wga_ref[...]) + ba_ref[...])
    gb = jax.nn.sigmoid(_dot(h, wgb_ref[...]) + bb_ref[...])
    mixed = ga * _dot(ya_ref[...], wa_ref[...]) + gb * _dot(yb_ref[...], wb_ref[...])
    o_ref[...] = mixed.astype(o_ref.dtype)


def _merge(h, y_da, y_gla, w_gate, b_gate, w_da, w_gla, tm=512, tn=512):
    t, d = h.shape
    nb = d // tn
    act = lambda: pl.BlockSpec((tm, d), lambda j, i: (i, 0))
    return pl.pallas_call(
        _merge_kernel,
        grid=(nb, t // tm),
        in_specs=[
            act(), act(), act(),
            pl.BlockSpec((d, tn), lambda j, i: (0, j)),
            pl.BlockSpec((d, tn), lambda j, i: (0, nb + j)),
            pl.BlockSpec((1, tn), lambda j, i: (0, j)),
            pl.BlockSpec((1, tn), lambda j, i: (0, nb + j)),
            pl.BlockSpec((d, tn), lambda j, i: (0, j)),
            pl.BlockSpec((d, tn), lambda j, i: (0, j)),
        ],
        out_specs=pl.BlockSpec((tm, tn), lambda j, i: (i, j)),
        out_shape=jax.ShapeDtypeStruct((t, d), BF16),
        compiler_params=_cparams(("parallel", "parallel"), 48),
        name="merge",
    )(h, y_da, y_gla, w_gate, w_gate, b_gate, b_gate, w_da, w_gla)


def _out_proj_kernel(x_ref, m_ref, w_ref, o_ref):
    o_ref[...] = x_ref[...] + _dot(m_ref[...], w_ref[...])


def _out_proj(x2d, mixed, w_out, tm=256):
    t, d = x2d.shape
    return pl.pallas_call(
        _out_proj_kernel,
        grid=(t // tm,),
        in_specs=[
            pl.BlockSpec((tm, d), lambda i: (i, 0)),
            pl.BlockSpec((tm, d), lambda i: (i, 0)),
            pl.BlockSpec((d, d), lambda i: (0, 0)),
        ],
        out_specs=pl.BlockSpec((tm, d), lambda i: (i, 0)),
        out_shape=jax.ShapeDtypeStruct((t, d), F32),
        compiler_params=_cparams(("parallel",), 48),
        name="out_proj",
    )(x2d, mixed, w_out)


ROW_SUBLANES = 8


def _pack_rows(val, store):
    half = val.shape[1] // 2
    assert half == ROW_SUBLANES * LANES
    lo = lax.bitcast_convert_type(val[:, :half].astype(BF16).astype(F32), jnp.uint32)
    hi = lax.bitcast_convert_type(val[:, half:].astype(BF16).astype(F32), jnp.uint32)
    packed = (lo >> 16) | (hi & jnp.uint32(0xFFFF0000))
    for c in range(ROW_SUBLANES):
        store(c, packed[:, c * LANES:(c + 1) * LANES])


def _unpack_chunk(chunk):
    lo = lax.bitcast_convert_type(chunk << 16, F32)
    hi = lax.bitcast_convert_type(chunk & jnp.uint32(0xFFFF0000), F32)
    return lo, hi


def _router_kernel(x_ref, g_ref, rw_ref, rb_ref, hp_ref, idx_ref, gate_ref):
    x = x_ref[...]
    tm = x.shape[0]
    ms = jnp.mean(x * x, axis=-1, keepdims=True)
    h = x * lax.rsqrt(ms + RMS_EPS) * g_ref[...]

    def store(c, chunk):
        hp_ref[pl.ds(c, tm, stride=ROW_SUBLANES), :] = chunk

    _pack_rows(h, store)

    logits = jnp.dot(h, rw_ref[...], precision=HIGHEST, preferred_element_type=F32) + rb_ref[...]
    lane = lax.broadcasted_iota(jnp.int32, logits.shape, 1)
    vals, idxs = [], []
    for _ in range(TOP_K):
        m = jnp.max(logits, axis=-1, keepdims=True)
        idx = jnp.min(jnp.where(logits == m, lane, LANES), axis=-1, keepdims=True)
        vals.append(m)
        idxs.append(idx)
        logits = jnp.where(lane == idx, -jnp.inf, logits)
    exps = [jnp.exp(v - vals[0]) for v in vals]
    denom = exps[0] + exps[1] + exps[2] + exps[3]
    idx_out = jnp.zeros(lane.shape, jnp.int32)
    gate_out = jnp.zeros(lane.shape, F32)
    for k in range(TOP_K):
        idx_out = jnp.where(lane == k, idxs[k], idx_out)
        gate_out = jnp.where(lane == k, exps[k] / denom, gate_out)
    idx_ref[...] = idx_out
    gate_ref[...] = gate_out


def _router(x2d, gain, router_w, router_b, tm=256):
    t, d = x2d.shape
    e = router_w.shape[1]
    rw = jnp.zeros((d, LANES), F32).at[:, :e].set(router_w.astype(F32))
    rb = jnp.full((1, LANES), NEG_BIG, F32).at[0, :e].set(router_b.astype(F32))
    return pl.pallas_call(
        _router_kernel,
        grid=(t // tm,),
        in_specs=[
            pl.BlockSpec((tm, d), lambda i: (i, 0)),
            pl.BlockSpec((1, d), lambda i: (0, 0)),
            pl.BlockSpec((d, LANES), lambda i: (0, 0)),
            pl.BlockSpec((1, LANES), lambda i: (0, 0)),
        ],
        out_specs=[
            pl.BlockSpec((tm * ROW_SUBLANES, LANES), lambda i: (i, 0)),
            pl.BlockSpec((tm, LANES), lambda i: (i, 0)),
            pl.BlockSpec((tm, LANES), lambda i: (i, 0)),
        ],
        out_shape=[
            jax.ShapeDtypeStruct((t * ROW_SUBLANES, LANES), jnp.uint32),
            jax.ShapeDtypeStruct((t, LANES), jnp.int32),
            jax.ShapeDtypeStruct((t, LANES), F32),
        ],
        compiler_params=_cparams(("parallel",), 32),
        name="router",
    )(x2d, gain.reshape(1, d).astype(F32), rw, rb)


def _gather_kernel(idx_ref, src_ref, o_ref, sem, *, rows):
    def copy(r):
        return pltpu.make_async_copy(src_ref.at[idx_ref[0, 0, r]], o_ref.at[r], sem)

    def issue(r2, carry):
        copy(2 * r2).start(priority=0)
        copy(2 * r2 + 1).start(priority=1)
        return carry

    def wait(r, carry):
        copy(r).wait()
        return carry

    lax.fori_loop(0, rows // 2, issue, 0, unroll=4)
    lax.fori_loop(0, rows, wait, 0, unroll=8)


def _gather_rows(src, row_idx, rows=1024):
    n = row_idx.shape[0]
    rows = math.gcd(n, rows)
    tile = src.shape[1:]
    return pl.pallas_call(
        functools.partial(_gather_kernel, rows=rows),
        grid=(n // rows,),
        in_specs=[
            pl.BlockSpec((1, 1, rows), lambda i: (i, 0, 0), memory_space=pltpu.SMEM),
            pl.BlockSpec(memory_space=pl.ANY),
        ],
        out_specs=pl.BlockSpec((rows,) + tile, lambda i: (i, 0, 0)),
        out_shape=jax.ShapeDtypeStruct((n,) + tile, src.dtype),
        scratch_shapes=[pltpu.SemaphoreType.DMA(())],
        compiler_params=_cparams(("arbitrary",), 32),
        name="moe_gather",
    )(row_idx.reshape(n // rows, 1, rows), src)


MOE_SUB = 256
MOE_NSUB = 6
MOE_TF = 256


def _expert_kernel(ge_ref, gs_ref, gn_ref, ng_ref, tot_ref,
                   xs_ref, wgu_ref, bgu_ref, wdn_ref, bdn_ref, ys_ref,
                   xu_ref, xb_ref, acc_ref, wgu_b_ref, wdn_p_ref, wdn_b_ref, zero_ref, sem_in, sem_out,
                   *, n_sub_total):
    g = pl.program_id(0)
    f = pl.program_id(1)
    nf = pl.num_programs(1)
    active = g < ng_ref[0]
    nsub = gn_ref[g]
    start = gs_ref[g]
    half = xb_ref.shape[2] // 2
    tf = wdn_ref.shape[1]
    sub_words = MOE_SUB * ROW_SUBLANES

    def in_copy(s):
        r0 = pl.multiple_of((start + s) * sub_words, sub_words)
        return pltpu.make_async_copy(xs_ref.at[pl.ds(r0, sub_words)], xu_ref.at[s], sem_in.at[s])

    def out_copy(s):
        r0 = pl.multiple_of((start + s) * sub_words, sub_words)
        return pltpu.make_async_copy(xu_ref.at[s], ys_ref.at[pl.ds(r0, sub_words)], sem_out)

    @pl.when(jnp.logical_and(g == 0, f == 0))
    def _():
        zero_ref[...] = jnp.zeros(zero_ref.shape, zero_ref.dtype)

    @pl.when(jnp.logical_and(active, f == 0))
    def _():
        def issue(s, c):
            in_copy(s).start()
            return c

        def unpack(s, c):
            in_copy(s).wait()
            for ch in range(ROW_SUBLANES):
                lo, hi = _unpack_chunk(xu_ref[s, pl.ds(ch, MOE_SUB, stride=ROW_SUBLANES), :])
                xb_ref[s, :, ch * LANES:(ch + 1) * LANES] = lo.astype(BF16)
                xb_ref[s, :, half + ch * LANES:half + (ch + 1) * LANES] = hi.astype(BF16)
            acc_ref[s] = jnp.broadcast_to(bdn_ref[0], acc_ref.shape[1:])
            return c

        lax.fori_loop(0, nsub, issue, 0)
        lax.fori_loop(0, nsub, unpack, 0)

    @pl.when(active)
    def _():
        wgu_b_ref[...] = wgu_ref[0].astype(BF16)
        hl = LANES // 2
        for c in range(wdn_p_ref.shape[0]):
            cols = slice(c * LANES, (c + 1) * LANES)
            for m in range(tf // LANES):
                wdn_p_ref[c, pl.ds(m * LANES, hl, stride=2), :] = wdn_ref[0, m * LANES:m * LANES + hl, cols]
                wdn_p_ref[c, pl.ds(m * LANES + 1, hl, stride=2), :] = (
                    wdn_ref[0, m * LANES + hl:(m + 1) * LANES, cols])
            wdn_b_ref[:, cols] = wdn_p_ref[c].astype(BF16)
        bgu = bgu_ref[0]
        lane = lax.broadcasted_iota(jnp.int32, (MOE_SUB, LANES), 1)
        even = (lane & 1) == 0

        def gate_up(s):
            gu = _dot(xb_ref[s], wgu_b_ref[...]) + bgu
            gates, ups = [], []
            for m in range(tf // LANES):
                a = gu[:, 2 * m * LANES:(2 * m + 1) * LANES]
                b = gu[:, (2 * m + 1) * LANES:(2 * m + 2) * LANES]
                gates.append(jnp.where(even, a, pltpu.roll(b, 1, 1)))
                ups.append(jnp.where(even, pltpu.roll(a, LANES - 1, 1), b))
            gate = jnp.minimum(jnp.concatenate(gates, axis=1), SWIGLU_LIMIT)
            up = jnp.clip(jnp.concatenate(ups, axis=1), -SWIGLU_LIMIT, SWIGLU_LIMIT)
            return ((up + 1.0) * gate * jax.nn.sigmoid(SWIGLU_ALPHA * gate)).astype(BF16)

        def down(s, act):
            acc_ref[s] += _dot(act, wdn_b_ref[...])

        def body(s, act_prev):
            act = gate_up(s)
            down(s - 1, act_prev)
            return act

        act_last = lax.fori_loop(1, nsub, body, gate_up(0))
        down(nsub - 1, act_last)

    @pl.when(jnp.logical_and(active, f == nf - 1))
    def _():
        def issue(s, c):
            def store(ch, chunk):
                xu_ref[s, pl.ds(ch, MOE_SUB, stride=ROW_SUBLANES), :] = chunk

            _pack_rows(acc_ref[s], store)
            out_copy(s).start()
            return c

        def wait(s, c):
            out_copy(s).wait()
            return c

        lax.fori_loop(0, nsub, issue, 0)
        lax.fori_loop(0, nsub, wait, 0)

    @pl.when(jnp.logical_and(g == pl.num_programs(0) - 1, f == nf - 1))
    def _():
        def fill(s, c):
            r0 = pl.multiple_of(s * sub_words, sub_words)
            cp = pltpu.make_async_copy(zero_ref, ys_ref.at[pl.ds(r0, sub_words)], sem_out)
            cp.start()
            cp.wait()
            return c

        lax.fori_loop(tot_ref[0], n_sub_total, fill, 0)


def _experts(xs, grp_expert, grp_start, grp_nsub, n_groups, tot_sub, w_gu, b_gu, w_dn, b_dn):
    n_rows = xs.shape[0] // ROW_SUBLANES
    sub_words = MOE_SUB * ROW_SUBLANES
    e, d, f2 = w_gu.shape
    fdim = f2 // 2
    tf = MOE_TF
    nf = fdim // tf
    n_grp = grp_expert.shape[0]

    def wsel(g, f, ge, gs, gn, ng, tot):
        on = g < ng[0]
        last = jnp.maximum(ng[0] - 1, 0)
        return jnp.where(on, ge[g], ge[last]), jnp.where(on, f, nf - 1)

    def w_gu_map(g, f, *pre):
        ee, ff = wsel(g, f, *pre)
        return (ee, 0, ff)

    def w_dn_map(g, f, *pre):
        ee, ff = wsel(g, f, *pre)
        return (ee, ff, 0)

    def b_dn_map(g, f, *pre):
        ee, _ = wsel(g, f, *pre)
        return (ee, 0, 0)

    grid_spec = pltpu.PrefetchScalarGridSpec(
        num_scalar_prefetch=5,
        grid=(n_grp, nf),
        in_specs=[
            pl.BlockSpec(memory_space=pl.ANY),
            pl.BlockSpec((1, d, 2 * tf), w_gu_map),
            pl.BlockSpec((1, 1, 2 * tf), w_gu_map),
            pl.BlockSpec((1, tf, d), w_dn_map),
            pl.BlockSpec((1, 1, d), b_dn_map),
        ],
        out_specs=pl.BlockSpec(memory_space=pl.ANY),
        scratch_shapes=[
            pltpu.VMEM((MOE_NSUB, sub_words, LANES), jnp.uint32),
            pltpu.VMEM((MOE_NSUB, MOE_SUB, d), BF16),
            pltpu.VMEM((MOE_NSUB, MOE_SUB, d), F32),
            pltpu.VMEM((d, 2 * tf), BF16),
            pltpu.VMEM((d // LANES, tf, LANES), F32),
            pltpu.VMEM((tf, d), BF16),
            pltpu.VMEM((sub_words, LANES), jnp.uint32),
            pltpu.SemaphoreType.DMA((MOE_NSUB,)),
            pltpu.SemaphoreType.DMA(()),
        ],
    )
    return pl.pallas_call(
        functools.partial(_expert_kernel, n_sub_total=n_rows // MOE_SUB),
        grid_spec=grid_spec,
        out_shape=jax.ShapeDtypeStruct(xs.shape, jnp.uint32),
        compiler_params=_cparams(("arbitrary", "arbitrary"), 56),
        name="moe_experts",
    )(grp_expert, grp_start, grp_nsub, n_groups, tot_sub, xs,
      w_gu, b_gu.reshape(e, 1, f2).astype(F32), w_dn, b_dn.reshape(e, 1, d).astype(F32))


def _combine_kernel(pr_ref, gate_ref, x_ref, ys_ref, o_ref, buf_ref, sem, *, tc):
    def copy(p):
        tok = p >> 2
        return pltpu.make_async_copy(
            ys_ref.at[pr_ref[0, 0, p]],
            buf_ref.at[p & (TOP_K - 1), pl.ds(pl.multiple_of(tok * ROW_SUBLANES, ROW_SUBLANES), ROW_SUBLANES)],
            sem)

    def issue(p2, carry):
        copy(2 * p2).start(priority=0)
        copy(2 * p2 + 1).start(priority=1)
        return carry

    def wait(p, carry):
        copy(p).wait()
        return carry

    lax.fori_loop(0, tc * TOP_K // 2, issue, 0, unroll=4)
    lax.fori_loop(0, tc * TOP_K, wait, 0, unroll=8)
    gates = gate_ref[...]
    half = x_ref.shape[1] // 2
    for ch in range(ROW_SUBLANES):
        lo_cols = slice(ch * LANES, (ch + 1) * LANES)
        hi_cols = slice(half + ch * LANES, half + (ch + 1) * LANES)
        acc_lo = x_ref[:, lo_cols]
        acc_hi = x_ref[:, hi_cols]
        for k in range(TOP_K):
            lo, hi = _unpack_chunk(buf_ref[k, pl.ds(ch, tc, stride=ROW_SUBLANES), :])
            acc_lo = acc_lo + gates[:, k:k + 1] * lo
            acc_hi = acc_hi + gates[:, k:k + 1] * hi
        o_ref[:, lo_cols] = acc_lo
        o_ref[:, hi_cols] = acc_hi


def _combine(x2d, gates, pair_row, ys, tc=256):
    t, d = x2d.shape
    return pl.pallas_call(
        functools.partial(_combine_kernel, tc=tc),
        grid=(t // tc,),
        in_specs=[
            pl.BlockSpec((1, 1, tc * TOP_K), lambda i: (i, 0, 0), memory_space=pltpu.SMEM),
            pl.BlockSpec((tc, LANES), lambda i: (i, 0)),
            pl.BlockSpec((tc, d), lambda i: (i, 0)),
            pl.BlockSpec(memory_space=pl.ANY),
        ],
        out_specs=pl.BlockSpec((tc, d), lambda i: (i, 0)),
        out_shape=jax.ShapeDtypeStruct((t, d), F32),
        scratch_shapes=[pltpu.VMEM((TOP_K, tc * ROW_SUBLANES, LANES), jnp.uint32), pltpu.SemaphoreType.DMA(())],
        compiler_params=_cparams(("arbitrary",), 32),
        name="moe_combine",
    )(pair_row.reshape(t // tc, 1, tc * TOP_K), gates, x2d, ys)


def _moe(x2d, gain, router_w, router_b, w_gu, b_gu, w_dn, b_dn):
    t, d = x2d.shape
    e = router_w.shape[1]
    hp, idx, gates = _router(x2d, gain, router_w, router_b)

    n_pairs = t * TOP_K
    flat_e = idx[:, :TOP_K].reshape(n_pairs)
    onehot = (flat_e[:, None] == jnp.arange(e, dtype=jnp.int32)[None, :]).astype(jnp.int32)
    csum = jnp.cumsum(onehot, axis=0)
    rank = jnp.sum(onehot * csum, axis=1) - 1
    counts = csum[-1]
    sub_e = (counts + MOE_SUB - 1) // MOE_SUB
    sub_end = jnp.cumsum(sub_e)
    sub_start = sub_end - sub_e
    dest = (jnp.sum(onehot * (sub_start * MOE_SUB)[None, :], axis=1) + rank).astype(jnp.int32)
    n_sub_total = n_pairs // MOE_SUB + e
    n_rows = n_sub_total * MOE_SUB
    row_tok = jnp.zeros((n_rows,), jnp.int32).at[dest].set(jnp.arange(n_pairs, dtype=jnp.int32) // TOP_K)

    grp_e = (sub_e + MOE_NSUB - 1) // MOE_NSUB
    grp_end = jnp.cumsum(grp_e)
    n_grp_max = (n_sub_total + MOE_NSUB - 1) // MOE_NSUB + e
    gidx = jnp.arange(n_grp_max, dtype=jnp.int32)
    g_exp = jnp.minimum(jnp.searchsorted(grp_end, gidx, side="right"), e - 1).astype(jnp.int32)
    g_local = gidx - (grp_end - grp_e)[g_exp]
    g_on = gidx < grp_end[-1]
    g_start = jnp.where(g_on, sub_start[g_exp] + g_local * MOE_NSUB, 0).astype(jnp.int32)
    g_nsub = jnp.where(g_on, jnp.minimum(MOE_NSUB, sub_e[g_exp] - g_local * MOE_NSUB), 0).astype(jnp.int32)

    xs = _gather_rows(hp.reshape(t, ROW_SUBLANES, LANES), row_tok)
    ys = _experts(xs.reshape(n_rows * ROW_SUBLANES, LANES), g_exp, g_start, g_nsub,
                  grp_end[-1].astype(jnp.int32).reshape(1), sub_end[-1].astype(jnp.int32).reshape(1),
                  w_gu, b_gu, w_dn, b_dn)
    return _combine(x2d, gates, dest, ys.reshape(n_rows, ROW_SUBLANES, LANES))


def _ple_kernel(x_ref, p_ref, gn_ref, wg_ref, wu_ref, gp_ref, o_ref):
    x = x_ref[...]
    ms = jnp.mean(x * x, axis=-1, keepdims=True)
    h = (x * lax.rsqrt(ms + RMS_EPS) * gn_ref[...]).astype(BF16)
    gate = jax.nn.sigmoid(_dot(h, wg_ref[...]))
    up = _dot(p_ref[...].astype(BF16), wu_ref[...])
    ms_u = jnp.mean(up * up, axis=-1, keepdims=True)
    o_ref[...] = x + up * lax.rsqrt(ms_u + RMS_EPS) * gp_ref[...] * gate


def _ple(x2d, p2d, g_norm, w_gate, w_up, g_post, tm=256):
    t, d = x2d.shape
    pd = p2d.shape[1]
    return pl.pallas_call(
        _ple_kernel,
        grid=(t // tm,),
        in_specs=[
            pl.BlockSpec((tm, d), lambda i: (i, 0)),
            pl.BlockSpec((tm, pd), lambda i: (i, 0)),
            pl.BlockSpec((1, d), lambda i: (0, 0)),
            pl.BlockSpec((d, d), lambda i: (0, 0)),
            pl.BlockSpec((pd, d), lambda i: (0, 0)),
            pl.BlockSpec((1, d), lambda i: (0, 0)),
        ],
        out_specs=pl.BlockSpec((tm, d), lambda i: (i, 0)),
        out_shape=jax.ShapeDtypeStruct((t, d), F32),
        compiler_params=_cparams(("parallel",), 48),
        name="ple",
    )(x2d, p2d, g_norm.reshape(1, d).astype(F32), w_gate, w_up, g_post.reshape(1, d).astype(F32))


def kernel(x, p, w_in, da_q_norm, da_k_norm, da_lambda_q1, da_lambda_k1, da_lambda_q2, da_lambda_k2, da_subln, gla_gate_w2, gla_gate_b, gla_out_norm, w_branch_da, w_branch_gla, w_merge_gate, b_merge_gate, w_out, norm_mix, norm_ffn, router_w, router_b, w_gate_up, b_gate_up, w_down, b_down, norm_ple, w_ple_gate, w_ple_up, norm_ple_post):
    bsz, s_len, d = x.shape
    t = bsz * s_len
    depth = w_in.shape[0]
    qk_w = DA_HEADS * 2 * DA_HEAD_DIM
    v_w = DA_HEADS * DA_V_DIM
    gk_w = gla_gate_w2.shape[2]
    gv_w = w_branch_gla.shape[1]
    rest_w = 2 * gk_w + 2 * gv_w
    x2d = x.reshape(t, d)
    for i in range(depth):
        lambda_init = 0.8 - 0.6 * math.exp(-0.3 * i)
        h = _rmsnorm(x2d, norm_mix[i])
        w = w_in[i]
        c_v = 2 * qk_w
        c_rest = c_v + v_w
        qn = _proj(h, w[:, :qk_w].astype(BF16), out_dtype=BF16, tm=512, tn=1024,
                   gain=da_q_norm[i], scale=DA_HEAD_DIM ** -0.5 * LOG2E)
        kn = _proj(h, w[:, qk_w:c_v].astype(BF16), out_dtype=BF16, tm=512, tn=1024, gain=da_k_norm[i])
        vt = _proj_t(h, w[:, c_v:c_rest].T.astype(BF16), tm=DA_TQ, tn=1024)
        rest = _proj(h, w[:, c_rest:c_rest + rest_w].astype(BF16), out_dtype=BF16, tm=512, tn=1024)
        w_lr = jnp.zeros((d, LANES), BF16).at[:, :GLA_GATE_RANK].set(w[:, c_rest + rest_w:].astype(BF16))
        glr = _proj(h, w_lr, out_dtype=F32, tm=512, tn=LANES)

        lam_vecs = jnp.stack([da_lambda_q1[i], da_lambda_k1[i], da_lambda_q2[i], da_lambda_k2[i]])
        y_da = _diff_attention(qn, kn, vt, lam_vecs, da_subln[i], bsz, s_len, lambda_init, DA_TQ)

        w2p = jnp.zeros((LANES, gk_w), F32).at[:GLA_GATE_RANK].set(gla_gate_w2[i].astype(F32))
        y_gla = _gla(rest, (0, gk_w, 2 * gk_w, 2 * gk_w + gv_w), glr, w2p,
                     gla_gate_b[i], gla_out_norm[i], bsz, s_len)

        mixed = _merge(h, y_da, y_gla, w_merge_gate[i].astype(BF16), b_merge_gate[i].reshape(1, 2 * d).astype(F32),
                       w_branch_da[i].astype(BF16), w_branch_gla[i].astype(BF16))
        x2d = _out_proj(x2d, mixed, w_out[i].astype(BF16))

        x2d = _moe(x2d, norm_ffn[i], router_w[i], router_b[i], w_gate_up[i], b_gate_up[i], w_down[i], b_down[i])

        x2d = _ple(x2d, p[i].reshape(t, p.shape[-1]), norm_ple[i], w_ple_gate[i].astype(BF16),
                   w_ple_up[i].astype(BF16), norm_ple_post[i])
    return x2d.reshape(bsz, s_len, d)
```

```python
import functools
import math

import jax
import jax.numpy as jnp
from jax import lax
from jax.experimental import pallas as pl
from jax.experimental.pallas import tpu as pltpu

F32 = jnp.float32
BF16 = jnp.bfloat16
HIGHEST = lax.Precision.HIGHEST

CHUNK = 64
CHUNK_SHIFT = CHUNK.bit_length() - 1
RMS_EPS = 1e-6
DA_HEADS = 8
DA_HEAD_DIM = 128
DA_V_DIM = 2 * DA_HEAD_DIM
GLA_HEADS = 4
GLA_GATE_RANK = 16
GLA_TAU = 16.0
N_EXPERTS = 32
TOP_K = 4
SWIGLU_LIMIT = 7.0
SWIGLU_ALPHA = 1.702

DA_TQ = 512
DA_ONES_ROWS = 16
LOG2E = 1.4426950408889634

LANES = 128
NEG_BIG = -1e30

MIB = 1024 * 1024


def _cparams(sem, vmem_mib):
    return pltpu.CompilerParams(dimension_semantics=sem, vmem_limit_bytes=vmem_mib * MIB)


def _dot(a, b):
    return jnp.dot(a, b, preferred_element_type=F32)


def _dot_nt(a, b):
    return lax.dot_general(a, b, (((1,), (1,)), ((), ())), preferred_element_type=F32)


def _dot_tn(a, b):
    return lax.dot_general(a, b, (((0,), (0,)), ((), ())), preferred_element_type=F32)


def _rmsnorm_kernel(x_ref, g_ref, o_ref):
    x = x_ref[...]
    ms = jnp.mean(x * x, axis=-1, keepdims=True)
    o_ref[...] = (x * lax.rsqrt(ms + RMS_EPS) * g_ref[...]).astype(o_ref.dtype)


def _rmsnorm(x2d, gain, tm=512):
    t, d = x2d.shape
    return pl.pallas_call(
        _rmsnorm_kernel,
        grid=(t // tm,),
        in_specs=[pl.BlockSpec((tm, d), lambda i: (i, 0)), pl.BlockSpec((1, d), lambda i: (0, 0))],
        out_specs=pl.BlockSpec((tm, d), lambda i: (i, 0)),
        out_shape=jax.ShapeDtypeStruct((t, d), BF16),
        compiler_params=_cparams(("parallel",), 32),
        name="rmsnorm",
    )(x2d, gain.reshape(1, d).astype(F32))


def _proj_kernel(a_ref, w_ref, g_ref, o_ref, *, group_norm, scale):
    acc = _dot(a_ref[...], w_ref[...])
    if group_norm:
        for c in range(acc.shape[1] // DA_HEAD_DIM):
            blk = acc[:, c * DA_HEAD_DIM:(c + 1) * DA_HEAD_DIM]
            ms = jnp.mean(blk * blk, axis=-1, keepdims=True)
            y = blk * lax.rsqrt(ms + RMS_EPS) * g_ref[...] * scale
            o_ref[:, c * DA_HEAD_DIM:(c + 1) * DA_HEAD_DIM] = y.astype(o_ref.dtype)
    else:
        o_ref[...] = acc.astype(o_ref.dtype)


def _proj(a, w, *, out_dtype, tm, tn, gain=None, scale=1.0):
    t, k = a.shape
    n = w.shape[1]
    group_norm = gain is not None
    g = (gain if group_norm else jnp.ones((DA_HEAD_DIM,), F32)).reshape(1, DA_HEAD_DIM).astype(F32)
    return pl.pallas_call(
        functools.partial(_proj_kernel, group_norm=group_norm, scale=scale),
        grid=(n // tn, t // tm),
        in_specs=[
            pl.BlockSpec((tm, k), lambda j, i: (i, 0)),
            pl.BlockSpec((k, tn), lambda j, i: (0, j)),
            pl.BlockSpec((1, DA_HEAD_DIM), lambda j, i: (0, 0)),
        ],
        out_specs=pl.BlockSpec((tm, tn), lambda j, i: (i, j)),
        out_shape=jax.ShapeDtypeStruct((t, n), out_dtype),
        compiler_params=_cparams(("parallel", "parallel"), 48),
        name="proj",
    )(a, w, g)


def _proj_t_kernel(wt_ref, a_ref, o_ref):
    o_ref[0] = _dot_nt(wt_ref[...], a_ref[...]).astype(o_ref.dtype)


def _proj_t(a, wt, *, tm, tn):
    t, k = a.shape
    n = wt.shape[0]
    return pl.pallas_call(
        _proj_t_kernel,
        grid=(n // tn, t // tm),
        in_specs=[
            pl.BlockSpec((tn, k), lambda j, i: (j, 0)),
            pl.BlockSpec((tm, k), lambda j, i: (i, 0)),
        ],
        out_specs=pl.BlockSpec((1, tn, tm), lambda j, i: (i, j, 0)),
        out_shape=jax.ShapeDtypeStruct((t // tm, n, tm), BF16),
        compiler_params=_cparams(("parallel", "parallel"), 48),
        name="proj_t",
    )(wt, a)


def _da_kernel(slope_ref, q_ref, k_ref, vt_ref, lamv_ref, subg_ref, o_ref,
               kaug_ref, qaug_ref, vaug_ref, m_ref, acc_ref, *, tq, s_len, lambda_init):
    h = pl.program_id(1)
    i = pl.program_id(2)
    slope2 = slope_ref[h] * LOG2E
    hd = DA_HEAD_DIM
    dv = DA_V_DIM
    nq = vt_ref.shape[0]

    @pl.when(i == 0)
    def _():
        pos = lax.broadcasted_iota(jnp.int32, (s_len, hd), 0)
        lane = lax.broadcasted_iota(jnp.int32, (s_len, hd), 1)
        piece = jnp.where(lane < 6, jnp.where((lane & 1) == 0, (pos >> 4) << 4, pos & 15), 0)
        piece = piece.astype(F32).astype(BF16)
        for c in range(2):
            kaug_ref[c, :, :hd] = k_ref[:, c * hd:(c + 1) * hd]
            kaug_ref[c, :, hd:] = piece
        for j in range(nq):
            vaug_ref[j, :dv, :] = vt_ref[j]
            vaug_ref[j, dv:, :] = jnp.ones((vaug_ref.shape[1] - dv, tq), BF16)

    lane_q = lax.broadcasted_iota(jnp.int32, (tq, hd), 1)
    s_full = jnp.full((tq, hd), slope2, F32)
    s1 = s_full.astype(BF16).astype(F32)
    r1 = s_full - s1
    s2 = r1.astype(BF16).astype(F32)
    s3 = r1 - s2
    slope_cols = jnp.where(lane_q < 2, s1, jnp.where(lane_q < 4, s2, jnp.where(lane_q < 6, s3, 0.0)))
    slope_cols = slope_cols.astype(BF16)
    for c in range(2):
        qaug_ref[c, :, :hd] = q_ref[:, c * hd:(c + 1) * hd]
        qaug_ref[c, :, hd:] = slope_cols

    m_ref[...] = jnp.full(m_ref.shape, NEG_BIG, F32)
    acc_ref[...] = jnp.zeros(acc_ref.shape, F32)

    def update(c, st, va):
        m_old = m_ref[c]
        m_new = jnp.maximum(m_old, jnp.max(st, axis=0, keepdims=True))
        alpha = jnp.exp2(m_old - m_new)
        p = jnp.exp2(st - m_new)
        acc_ref[c] = alpha * acc_ref[c] + _dot(va, p.astype(BF16))
        m_ref[c] = m_new

    def past_tile(j, carry):
        r0 = pl.multiple_of(j * tq, tq)
        va = vaug_ref[j]
        for c in range(2):
            update(c, _dot_nt(kaug_ref[c, pl.ds(r0, tq), :], qaug_ref[c]), va)
        return carry

    lax.fori_loop(0, i, past_tile, 0)

    r0 = pl.multiple_of(i * tq, tq)
    krow = lax.broadcasted_iota(jnp.int32, (tq, tq), 0)
    qcol = lax.broadcasted_iota(jnp.int32, (tq, tq), 1)
    corr = (-2.0 * slope2) * jnp.maximum(krow - qcol, 0).astype(F32)
    allowed = (krow >> CHUNK_SHIFT) <= (qcol >> CHUNK_SHIFT)
    va = vaug_ref[i]
    for c in range(2):
        st = _dot_nt(kaug_ref[c, pl.ds(r0, tq), :], qaug_ref[c])
        update(c, jnp.where(allowed, st + corr, NEG_BIG), va)

    lamv = lamv_ref[...]
    lam = (jnp.exp(jnp.sum(lamv[0:1] * lamv[1:2], axis=-1, keepdims=True))
           - jnp.exp(jnp.sum(lamv[2:3] * lamv[3:4], axis=-1, keepdims=True)) + lambda_init)
    ot = (acc_ref[0, :dv, :] / acc_ref[0, dv:dv + 1, :]
          - lam * (acc_ref[1, :dv, :] / acc_ref[1, dv:dv + 1, :]))
    ms = jnp.mean(ot * ot, axis=0, keepdims=True)
    o = (ot * lax.rsqrt(ms + RMS_EPS)).T
    o_ref[...] = (o * subg_ref[...] * (1.0 - lambda_init)).astype(o_ref.dtype)


def _diff_attention(qn, kn, vt, lam_vecs, subln_g, bsz, s_len, lambda_init, tq):
    assert s_len <= 4096 and s_len % tq == 0 and tq % CHUNK == 0
    t = bsz * s_len
    nq = s_len // tq
    slopes = jnp.exp2(-8.0 * jnp.arange(1, DA_HEADS + 1, dtype=F32) / DA_HEADS)
    grid_spec = pltpu.PrefetchScalarGridSpec(
        num_scalar_prefetch=1,
        grid=(bsz, DA_HEADS, nq),
        in_specs=[
            pl.BlockSpec((tq, DA_V_DIM), lambda b, h, i, s: (b * nq + i, h)),
            pl.BlockSpec((s_len, DA_V_DIM), lambda b, h, i, s: (b, h)),
            pl.BlockSpec((nq, DA_V_DIM, tq), lambda b, h, i, s: (b, h, 0)),
            pl.BlockSpec((4, DA_HEAD_DIM), lambda b, h, i, s: (0, 0)),
            pl.BlockSpec((1, DA_V_DIM), lambda b, h, i, s: (0, 0)),
        ],
        out_specs=pl.BlockSpec((tq, DA_V_DIM), lambda b, h, i, s: (b * nq + i, h)),
        scratch_shapes=[
            pltpu.VMEM((2, s_len, 2 * DA_HEAD_DIM), BF16),
            pltpu.VMEM((2, tq, 2 * DA_HEAD_DIM), BF16),
            pltpu.VMEM((nq, DA_V_DIM + DA_ONES_ROWS, tq), BF16),
            pltpu.VMEM((2, 1, tq), F32),
            pltpu.VMEM((2, DA_V_DIM + DA_ONES_ROWS, tq), F32),
        ],
    )
    return pl.pallas_call(
        functools.partial(_da_kernel, tq=tq, s_len=s_len, lambda_init=lambda_init),
        grid_spec=grid_spec,
        out_shape=jax.ShapeDtypeStruct((t, DA_HEADS * DA_V_DIM), BF16),
        compiler_params=_cparams(("parallel", "parallel", "arbitrary"), 48),
        name="diff_attention",
    )(slopes, qn, kn, vt, lam_vecs.astype(F32), subln_g.reshape(1, DA_V_DIM).astype(F32))


def _gla_kernel(q_ref, k_ref, v_ref, g_ref, glr_ref, w2_ref, gb_ref, og_ref, o_ref, state_ref,
                *, tb, dk, dv):
    @pl.when(pl.program_id(1) == 0)
    def _():
        state_ref[...] = jnp.zeros(state_ref.shape, F32)

    row = lax.broadcasted_iota(jnp.int32, (CHUNK, CHUNK), 0)
    col = lax.broadcasted_iota(jnp.int32, (CHUNK, CHUNK), 1)
    lower = row >= col
    tri = lower.astype(F32)
    mid = CHUNK // 2

    def chunk(c, carry):
        r0 = pl.multiple_of(c * CHUNK, CHUNK)
        z = jnp.dot(glr_ref[pl.ds(r0, CHUNK), :], w2_ref[...], precision=HIGHEST,
                    preferred_element_type=F32) + gb_ref[...]
        log_a = (jnp.minimum(z, 0.0) - jnp.log1p(jnp.exp(-jnp.abs(z)))) * (1.0 / GLA_TAU)
        b_all = jnp.dot(tri, log_a, precision=HIGHEST, preferred_element_type=F32)
        for h in range(GLA_HEADS):
            ks = slice(h * dk, (h + 1) * dk)
            vs = slice(h * dv, (h + 1) * dv)
            b = b_all[:, ks]
            b_last = b[CHUNK - 1:CHUNK, :]
            b_mid = b[mid:mid + 1, :]
            q = q_ref[pl.ds(r0, CHUNK), ks].astype(F32) * (dk ** -0.5)
            k = k_ref[pl.ds(r0, CHUNK), ks].astype(F32)
            v = v_ref[pl.ds(r0, CHUNK), vs]
            e_fwd = jnp.exp(b - b_mid)
            e_bwd = jnp.exp(b_mid - b)
            a_lo = _dot_nt((q * e_fwd).astype(BF16), (k * e_bwd).astype(BF16))
            a_up = _dot_nt((q * e_bwd).astype(BF16), (k * e_fwd).astype(BF16))
            attn = jnp.where(lower, a_lo, a_up)
            state = state_ref[h]
            o = _dot(attn.astype(BF16), v) + _dot_nt((q * jnp.exp(b)).astype(BF16), state.astype(BF16))
            kd = (k * jnp.exp(b_last - b)).astype(BF16)
            state_ref[h] = state * jnp.exp(b_last) + _dot_tn(v, kd)
            ms = jnp.mean(o * o, axis=-1, keepdims=True)
            g = g_ref[pl.ds(r0, CHUNK), vs].astype(F32)
            y = o * lax.rsqrt(ms + RMS_EPS) * og_ref[...] * (g * jax.nn.sigmoid(g))
            o_ref[pl.ds(r0, CHUNK), vs] = y.astype(o_ref.dtype)
        return carry

    lax.fori_loop(0, tb // CHUNK, chunk, 0)


def _gla(src, cols, glr, w2p, gate_b, out_g, bsz, s_len, tb=512):
    t = bsz * s_len
    nb = s_len // tb
    kw = w2p.shape[1]
    dk = kw // GLA_HEADS
    dv = out_g.shape[0]
    vw = dv * GLA_HEADS
    cq, ck, cv, cg = cols
    return pl.pallas_call(
        functools.partial(_gla_kernel, tb=tb, dk=dk, dv=dv),
        grid=(bsz, nb),
        in_specs=[
            pl.BlockSpec((tb, kw), lambda b, i: (b * nb + i, cq // kw)),
            pl.BlockSpec((tb, kw), lambda b, i: (b * nb + i, ck // kw)),
            pl.BlockSpec((tb, vw), lambda b, i: (b * nb + i, cv // vw)),
            pl.BlockSpec((tb, vw), lambda b, i: (b * nb + i, cg // vw)),
            pl.BlockSpec((tb, LANES), lambda b, i: (b * nb + i, 0)),
            pl.BlockSpec((LANES, kw), lambda b, i: (0, 0)),
            pl.BlockSpec((1, kw), lambda b, i: (0, 0)),
            pl.BlockSpec((1, dv), lambda b, i: (0, 0)),
        ],
        out_specs=pl.BlockSpec((tb, vw), lambda b, i: (b * nb + i, 0)),
        out_shape=jax.ShapeDtypeStruct((t, vw), BF16),
        scratch_shapes=[pltpu.VMEM((GLA_HEADS, dv, dk), F32)],
        compiler_params=_cparams(("parallel", "arbitrary"), 48),
        name="gla",
    )(src, src, src, src, glr, w2p, gate_b.reshape(1, kw).astype(F32), out_g.reshape(1, dv).astype(F32))


def _merge_kernel(h_ref, ya_ref, yb_ref, wga_ref, wgb_ref, ba_ref, bb_ref, wa_ref, wb_ref, o_ref):
    h = h_ref[...]
    ga = jax.nn.sigmoid(_dot(h, wga_ref[...]) + ba_ref[...])
    gb = jax.nn.sigmoid(_dot(h, wgb_ref[...]) + bb_ref[...])
    mixed = ga * _dot(ya_ref[...], wa_ref[...]) + gb * _dot(yb_ref[...], wb_ref[...])
    o_ref[...] = mixed.astype(o_ref.dtype)


def _merge(h, y_da, y_gla, w_gate, b_gate, w_da, w_gla, tm=512, tn=512):
    t, d = h.shape
    nb = d // tn
    act = lambda: pl.BlockSpec((tm, d), lambda j, i: (i, 0))
    return pl.pallas_call(
        _merge_kernel,
        grid=(nb, t // tm),
        in_specs=[
            act(), act(), act(),
            pl.BlockSpec((d, tn), lambda j, i: (0, j)),
            pl.BlockSpec((d, tn), lambda j, i: (0, nb + j)),
            pl.BlockSpec((1, tn), lambda j, i: (0, j)),
            pl.BlockSpec((1, tn), lambda j, i: (0, nb + j)),
            pl.BlockSpec((d, tn), lambda j, i: (0, j)),
            pl.BlockSpec((d, tn), lambda j, i: (0, j)),
        ],
        out_specs=pl.BlockSpec((tm, tn), lambda j, i: (i, j)),
        out_shape=jax.ShapeDtypeStruct((t, d), BF16),
        compiler_params=_cparams(("parallel", "parallel"), 48),
        name="merge",
    )(h, y_da, y_gla, w_gate, w_gate, b_gate, b_gate, w_da, w_gla)


def _out_proj_kernel(x_ref, m_ref, w_ref, o_ref):
    o_ref[...] = x_ref[...] + _dot(m_ref[...], w_ref[...])


def _out_proj(x2d, mixed, w_out, tm=256):
    t, d = x2d.shape
    return pl.pallas_call(
        _out_proj_kernel,
        grid=(t // tm,),
        in_specs=[
            pl.BlockSpec((tm, d), lambda i: (i, 0)),
            pl.BlockSpec((tm, d), lambda i: (i, 0)),
            pl.BlockSpec((d, d), lambda i: (0, 0)),
        ],
        out_specs=pl.BlockSpec((tm, d), lambda i: (i, 0)),
        out_shape=jax.ShapeDtypeStruct((t, d), F32),
        compiler_params=_cparams(("parallel",), 48),
        name="out_proj",
    )(x2d, mixed, w_out)


ROW_SUBLANES = 8


def _pack_rows(val, store):
    half = val.shape[1] // 2
    assert half == ROW_SUBLANES * LANES
    lo = lax.bitcast_convert_type(val[:, :half].astype(BF16).astype(F32), jnp.uint32)
    hi = lax.bitcast_convert_type(val[:, half:].astype(BF16).astype(F32), jnp.uint32)
    packed = (lo >> 16) | (hi & jnp.uint32(0xFFFF0000))
    for c in range(ROW_SUBLANES):
        store(c, packed[:, c * LANES:(c + 1) * LANES])


def _unpack_chunk(chunk):
    lo = lax.bitcast_convert_type(chunk << 16, F32)
    hi = lax.bitcast_convert_type(chunk & jnp.uint32(0xFFFF0000), F32)
    return lo, hi


def _router_kernel(x_ref, g_ref, rw_ref, rb_ref, hp_ref, idx_ref, gate_ref):
    x = x_ref[...]
    tm = x.shape[0]
    ms = jnp.mean(x * x, axis=-1, keepdims=True)
    h = x * lax.rsqrt(ms + RMS_EPS) * g_ref[...]

    def store(c, chunk):
        hp_ref[pl.ds(c, tm, stride=ROW_SUBLANES), :] = chunk

    _pack_rows(h, store)

    logits = jnp.dot(h, rw_ref[...], precision=HIGHEST, preferred_element_type=F32) + rb_ref[...]
    lane = lax.broadcasted_iota(jnp.int32, logits.shape, 1)
    vals, idxs = [], []
    for _ in range(TOP_K):
        m = jnp.max(logits, axis=-1, keepdims=True)
        idx = jnp.min(jnp.where(logits == m, lane, LANES), axis=-1, keepdims=True)
        vals.append(m)
        idxs.append(idx)
        logits = jnp.where(lane == idx, -jnp.inf, logits)
    exps = [jnp.exp(v - vals[0]) for v in vals]
    denom = exps[0] + exps[1] + exps[2] + exps[3]
    idx_out = jnp.zeros(lane.shape, jnp.int32)
    gate_out = jnp.zeros(lane.shape, F32)
    for k in range(TOP_K):
        idx_out = jnp.where(lane == k, idxs[k], idx_out)
        gate_out = jnp.where(lane == k, exps[k] / denom, gate_out)
    idx_ref[...] = idx_out
    gate_ref[...] = gate_out


def _router(x2d, gain, router_w, router_b, tm=256):
    t, d = x2d.shape
    e = router_w.shape[1]
    rw = jnp.zeros((d, LANES), F32).at[:, :e].set(router_w.astype(F32))
    rb = jnp.full((1, LANES), NEG_BIG, F32).at[0, :e].set(router_b.astype(F32))
    return pl.pallas_call(
        _router_kernel,
        grid=(t // tm,),
        in_specs=[
            pl.BlockSpec((tm, d), lambda i: (i, 0)),
            pl.BlockSpec((1, d), lambda i: (0, 0)),
            pl.BlockSpec((d, LANES), lambda i: (0, 0)),
            pl.BlockSpec((1, LANES), lambda i: (0, 0)),
        ],
        out_specs=[
            pl.BlockSpec((tm * ROW_SUBLANES, LANES), lambda i: (i, 0)),
            pl.BlockSpec((tm, LANES), lambda i: (i, 0)),
            pl.BlockSpec((tm, LANES), lambda i: (i, 0)),
        ],
        out_shape=[
            jax.ShapeDtypeStruct((t * ROW_SUBLANES, LANES), jnp.uint32),
            jax.ShapeDtypeStruct((t, LANES), jnp.int32),
            jax.ShapeDtypeStruct((t, LANES), F32),
        ],
        compiler_params=_cparams(("parallel",), 32),
        name="router",
    )(x2d, gain.reshape(1, d).astype(F32), rw, rb)


MOE_SUB = 256
MOE_NSUB = 6
MOE_TF = 256


def _expert_kernel(ge_ref, gs_ref, gn_ref, ng_ref, tot_ref,
                   hp_ref, idx_ref, wgu_ref, bgu_ref, wdn_ref, bdn_ref, ys_ref,
                   idx_smem, xu_ref, yst_ref, xb_ref, acc_ref, wgu_b_ref, wdn_p_ref, wdn_b_ref, zero_ref,
                   sem_idx, sem_in, sem_out, *, n_sub_total):
    g = pl.program_id(0)
    f = pl.program_id(1)
    nf = pl.num_programs(1)
    ng = ng_ref[0]
    active = g < ng
    nsub = gn_ref[g]
    start = gs_ref[g]
    half = xb_ref.shape[2] // 2
    tf = wdn_ref.shape[1]
    sub_words = MOE_SUB * ROW_SUBLANES
    unroll = 4

    def fetch_rows(gg):
        idx_cp = pltpu.make_async_copy(idx_ref.at[pl.ds(gs_ref[gg], MOE_NSUB)], idx_smem, sem_idx)
        idx_cp.start()
        idx_cp.wait()

        def issue_sub(s, c):
            def issue_rows(r4, c2):
                for u in range(unroll):
                    r = r4 * unroll + u
                    src = hp_ref.at[pl.ds(pl.multiple_of(idx_smem[s, 0, r] * ROW_SUBLANES, ROW_SUBLANES),
                                          ROW_SUBLANES)]
                    dst = xu_ref.at[s, pl.ds(pl.multiple_of(r * ROW_SUBLANES, ROW_SUBLANES), ROW_SUBLANES)]
                    pltpu.make_async_copy(src, dst, sem_in.at[s]).start(priority=u % 2)
                return c2

            lax.fori_loop(0, MOE_SUB // unroll, issue_rows, 0)
            return c

        lax.fori_loop(0, gn_ref[gg], issue_sub, 0)

    def wait_rows(s):
        pltpu.make_async_copy(hp_ref.at[pl.ds(0, sub_words)], xu_ref.at[s], sem_in.at[s]).wait()

    def out_copy(s, first_sub):
        r0 = pl.multiple_of((first_sub + s) * sub_words, sub_words)
        return pltpu.make_async_copy(yst_ref.at[s], ys_ref.at[pl.ds(r0, sub_words)], sem_out)

    def wait_out(gg):
        def wait(s, c):
            out_copy(s, gs_ref[gg]).wait()
            return c

        lax.fori_loop(0, gn_ref[gg], wait, 0)

    @pl.when(jnp.logical_and(g == 0, f == 0))
    def _():
        zero_ref[...] = jnp.zeros(zero_ref.shape, zero_ref.dtype)
        fetch_rows(0)

    @pl.when(jnp.logical_and(active, f == 0))
    def _():
        def unpack(s, c):
            wait_rows(s)
            for ch in range(ROW_SUBLANES):
                lo, hi = _unpack_chunk(xu_ref[s, pl.ds(ch, MOE_SUB, stride=ROW_SUBLANES), :])
                xb_ref[s, :, ch * LANES:(ch + 1) * LANES] = lo.astype(BF16)
                xb_ref[s, :, half + ch * LANES:half + (ch + 1) * LANES] = hi.astype(BF16)
            acc_ref[s] = jnp.broadcast_to(bdn_ref[0], acc_ref.shape[1:])
            return c

        lax.fori_loop(0, nsub, unpack, 0)

        @pl.when(g + 1 < ng)
        def _():
            fetch_rows(g + 1)

    @pl.when(active)
    def _():
        wgu_b_ref[...] = wgu_ref[0].astype(BF16)
        hl = LANES // 2
        for c in range(wdn_p_ref.shape[0]):
            cols = slice(c * LANES, (c + 1) * LANES)
            for m in range(tf // LANES):
                wdn_p_ref[c, pl.ds(m * LANES, hl, stride=2), :] = wdn_ref[0, m * LANES:m * LANES + hl, cols]
                wdn_p_ref[c, pl.ds(m * LANES + 1, hl, stride=2), :] = (
                    wdn_ref[0, m * LANES + hl:(m + 1) * LANES, cols])
            wdn_b_ref[:, cols] = wdn_p_ref[c].astype(BF16)
        bgu = bgu_ref[0]
        lane = lax.broadcasted_iota(jnp.int32, (MOE_SUB, LANES), 1)
        even = (lane & 1) == 0

        def gate_up(s):
            gu = _dot(xb_ref[s], wgu_b_ref[...]) + bgu
            gates, ups = [], []
            for m in range(tf // LANES):
                a = gu[:, 2 * m * LANES:(2 * m + 1) * LANES]
                b = gu[:, (2 * m + 1) * LANES:(2 * m + 2) * LANES]
                gates.append(jnp.where(even, a, pltpu.roll(b, 1, 1)))
                ups.append(jnp.where(even, pltpu.roll(a, LANES - 1, 1), b))
            gate = jnp.minimum(jnp.concatenate(gates, axis=1), SWIGLU_LIMIT)
            up = jnp.clip(jnp.concatenate(ups, axis=1), -SWIGLU_LIMIT, SWIGLU_LIMIT)
            return ((up + 1.0) * gate * jax.nn.sigmoid(SWIGLU_ALPHA * gate)).astype(BF16)

        def down(s, act):
            acc_ref[s] += _dot(act, wdn_b_ref[...])

        def body(s, act_prev):
            act = gate_up(s)
            down(s - 1, act_prev)
            return act

        act_last = lax.fori_loop(1, nsub, body, gate_up(0))
        down(nsub - 1, act_last)

    @pl.when(jnp.logical_and(active, f == nf - 1))
    def _():
        @pl.when(g > 0)
        def _():
            wait_out(g - 1)

        def issue(s, c):
            def store(ch, chunk):
                yst_ref[s, pl.ds(ch, MOE_SUB, stride=ROW_SUBLANES), :] = chunk

            _pack_rows(acc_ref[s], store)
            out_copy(s, start).start()
            return c

        lax.fori_loop(0, nsub, issue, 0)

    @pl.when(jnp.logical_and(g == pl.num_programs(0) - 1, f == nf - 1))
    def _():
        wait_out(ng - 1)

        def fill(s, c):
            r0 = pl.multiple_of(s * sub_words, sub_words)
            cp = pltpu.make_async_copy(zero_ref, ys_ref.at[pl.ds(r0, sub_words)], sem_out)
            cp.start()
            cp.wait()
            return c

        lax.fori_loop(tot_ref[0], n_sub_total, fill, 0)


def _experts(hp, row_tok, grp_expert, grp_start, grp_nsub, n_groups, tot_sub, w_gu, b_gu, w_dn, b_dn):
    n_rows = row_tok.shape[0]
    sub_words = MOE_SUB * ROW_SUBLANES
    idx2 = jnp.concatenate([row_tok.reshape(n_rows // MOE_SUB, 1, MOE_SUB),
                            jnp.zeros((MOE_NSUB, 1, MOE_SUB), jnp.int32)], axis=0)
    e, d, f2 = w_gu.shape
    fdim = f2 // 2
    tf = MOE_TF
    nf = fdim // tf
    n_grp = grp_expert.shape[0]

    def wsel(g, f, ge, gs, gn, ng, tot):
        on = g < ng[0]
        last = jnp.maximum(ng[0] - 1, 0)
        return jnp.where(on, ge[g], ge[last]), jnp.where(on, f, nf - 1)

    def w_gu_map(g, f, *pre):
        ee, ff = wsel(g, f, *pre)
        return (ee, 0, ff)

    def w_dn_map(g, f, *pre):
        ee, ff = wsel(g, f, *pre)
        return (ee, ff, 0)

    def b_dn_map(g, f, *pre):
        ee, _ = wsel(g, f, *pre)
        return (ee, 0, 0)

    grid_spec = pltpu.PrefetchScalarGridSpec(
        num_scalar_prefetch=5,
        grid=(n_grp, nf),
        in_specs=[
            pl.BlockSpec(memory_space=pl.ANY),
            pl.BlockSpec(memory_space=pl.ANY),
            pl.BlockSpec((1, d, 2 * tf), w_gu_map),
            pl.BlockSpec((1, 1, 2 * tf), w_gu_map),
            pl.BlockSpec((1, tf, d), w_dn_map),
            pl.BlockSpec((1, 1, d), b_dn_map),
        ],
        out_specs=pl.BlockSpec(memory_space=pl.ANY),
        scratch_shapes=[
            pltpu.SMEM((MOE_NSUB, 1, MOE_SUB), jnp.int32),
            pltpu.VMEM((MOE_NSUB, sub_words, LANES), jnp.uint32),
            pltpu.VMEM((MOE_NSUB, sub_words, LANES), jnp.uint32),
            pltpu.VMEM((MOE_NSUB, MOE_SUB, d), BF16),
            pltpu.VMEM((MOE_NSUB, MOE_SUB, d), F32),
            pltpu.VMEM((d, 2 * tf), BF16),
            pltpu.VMEM((d // LANES, tf, LANES), F32),
            pltpu.VMEM((tf, d), BF16),
            pltpu.VMEM((sub_words, LANES), jnp.uint32),
            pltpu.SemaphoreType.DMA(()),
            pltpu.SemaphoreType.DMA((MOE_NSUB,)),
            pltpu.SemaphoreType.DMA(()),
        ],
    )
    return pl.pallas_call(
        functools.partial(_expert_kernel, n_sub_total=n_rows // MOE_SUB),
        grid_spec=grid_spec,
        out_shape=jax.ShapeDtypeStruct((n_rows * ROW_SUBLANES, LANES), jnp.uint32),
        compiler_params=_cparams(("arbitrary", "arbitrary"), 56),
        name="moe_experts",
    )(grp_expert, grp_start, grp_nsub, n_groups, tot_sub, hp, idx2,
      w_gu, b_gu.reshape(e, 1, f2).astype(F32), w_dn, b_dn.reshape(e, 1, d).astype(F32))


def _combine_kernel(pr_ref, gate_ref, x_ref, ys_ref, o_ref, buf_ref, sem, *, tc):
    def copy(p):
        tok = p >> 2
        return pltpu.make_async_copy(
            ys_ref.at[pr_ref[0, 0, p]],
            buf_ref.at[p & (TOP_K - 1), pl.ds(pl.multiple_of(tok * ROW_SUBLANES, ROW_SUBLANES), ROW_SUBLANES)],
            sem)

    def issue(p2, carry):
        copy(2 * p2).start(priority=0)
        copy(2 * p2 + 1).start(priority=1)
        return carry

    def wait(p, carry):
        copy(p).wait()
        return carry

    lax.fori_loop(0, tc * TOP_K // 2, issue, 0, unroll=4)
    lax.fori_loop(0, tc * TOP_K, wait, 0, unroll=8)
    gates = gate_ref[...]
    half = x_ref.shape[1] // 2
    for ch in range(ROW_SUBLANES):
        lo_cols = slice(ch * LANES, (ch + 1) * LANES)
        hi_cols = slice(half + ch * LANES, half + (ch + 1) * LANES)
        acc_lo = x_ref[:, lo_cols]
        acc_hi = x_ref[:, hi_cols]
        for k in range(TOP_K):
            lo, hi = _unpack_chunk(buf_ref[k, pl.ds(ch, tc, stride=ROW_SUBLANES), :])
            acc_lo = acc_lo + gates[:, k:k + 1] * lo
            acc_hi = acc_hi + gates[:, k:k + 1] * hi
        o_ref[:, lo_cols] = acc_lo
        o_ref[:, hi_cols] = acc_hi


def _combine(x2d, gates, pair_row, ys, tc=256):
    t, d = x2d.shape
    return pl.pallas_call(
        functools.partial(_combine_kernel, tc=tc),
        grid=(t // tc,),
        in_specs=[
            pl.BlockSpec((1, 1, tc * TOP_K), lambda i: (i, 0, 0), memory_space=pltpu.SMEM),
            pl.BlockSpec((tc, LANES), lambda i: (i, 0)),
            pl.BlockSpec((tc, d), lambda i: (i, 0)),
            pl.BlockSpec(memory_space=pl.ANY),
        ],
        out_specs=pl.BlockSpec((tc, d), lambda i: (i, 0)),
        out_shape=jax.ShapeDtypeStruct((t, d), F32),
        scratch_shapes=[pltpu.VMEM((TOP_K, tc * ROW_SUBLANES, LANES), jnp.uint32), pltpu.SemaphoreType.DMA(())],
        compiler_params=_cparams(("arbitrary",), 32),
        name="moe_combine",
    )(pair_row.reshape(t // tc, 1, tc * TOP_K), gates, x2d, ys)


def _moe(x2d, gain, router_w, router_b, w_gu, b_gu, w_dn, b_dn):
    t, d = x2d.shape
    e = router_w.shape[1]
    hp, idx, gates = _router(x2d, gain, router_w, router_b)

    n_pairs = t * TOP_K
    flat_e = idx[:, :TOP_K].reshape(n_pairs)
    onehot = (flat_e[:, None] == jnp.arange(e, dtype=jnp.int32)[None, :]).astype(jnp.int32)
    csum = jnp.cumsum(onehot, axis=0)
    rank = jnp.sum(onehot * csum, axis=1) - 1
    counts = csum[-1]
    sub_e = (counts + MOE_SUB - 1) // MOE_SUB
    sub_end = jnp.cumsum(sub_e)
    sub_start = sub_end - sub_e
    dest = (jnp.sum(onehot * (sub_start * MOE_SUB)[None, :], axis=1) + rank).astype(jnp.int32)
    n_sub_total = n_pairs // MOE_SUB + e
    n_rows = n_sub_total * MOE_SUB
    row_tok = jnp.zeros((n_rows,), jnp.int32).at[dest].set(jnp.arange(n_pairs, dtype=jnp.int32) // TOP_K)

    grp_e = (sub_e + MOE_NSUB - 1) // MOE_NSUB
    grp_end = jnp.cumsum(grp_e)
    n_grp_max = (n_sub_total + MOE_NSUB - 1) // MOE_NSUB + e
    gidx = jnp.arange(n_grp_max, dtype=jnp.int32)
    g_exp = jnp.minimum(jnp.searchsorted(grp_end, gidx, side="right"), e - 1).astype(jnp.int32)
    g_local = gidx - (grp_end - grp_e)[g_exp]
    g_on = gidx < grp_end[-1]
    g_start = jnp.where(g_on, sub_start[g_exp] + g_local * MOE_NSUB, 0).astype(jnp.int32)
    g_nsub = jnp.where(g_on, jnp.minimum(MOE_NSUB, sub_e[g_exp] - g_local * MOE_NSUB), 0).astype(jnp.int32)

    ys = _experts(hp, row_tok, g_exp, g_start, g_nsub,
                  grp_end[-1].astype(jnp.int32).reshape(1), sub_end[-1].astype(jnp.int32).reshape(1),
                  w_gu, b_gu, w_dn, b_dn)
    return _combine(x2d, gates, dest, ys.reshape(n_rows, ROW_SUBLANES, LANES))


def _ple_kernel(x_ref, p_ref, gn_ref, wg_ref, wu_ref, gp_ref, o_ref):
    x = x_ref[...]
    ms = jnp.mean(x * x, axis=-1, keepdims=True)
    h = (x * lax.rsqrt(ms + RMS_EPS) * gn_ref[...]).astype(BF16)
    gate = jax.nn.sigmoid(_dot(h, wg_ref[...]))
    up = _dot(p_ref[...].astype(BF16), wu_ref[...])
    ms_u = jnp.mean(up * up, axis=-1, keepdims=True)
    o_ref[...] = x + up * lax.rsqrt(ms_u + RMS_EPS) * gp_ref[...] * gate


def _ple(x2d, p2d, g_norm, w_gate, w_up, g_post, tm=256):
    t, d = x2d.shape
    pd = p2d.shape[1]
    return pl.pallas_call(
        _ple_kernel,
        grid=(t // tm,),
        in_specs=[
            pl.BlockSpec((tm, d), lambda i: (i, 0)),
            pl.BlockSpec((tm, pd), lambda i: (i, 0)),
            pl.BlockSpec((1, d), lambda i: (0, 0)),
            pl.BlockSpec((d, d), lambda i: (0, 0)),
            pl.BlockSpec((pd, d), lambda i: (0, 0)),
            pl.BlockSpec((1, d), lambda i: (0, 0)),
        ],
        out_specs=pl.BlockSpec((tm, d), lambda i: (i, 0)),
        out_shape=jax.ShapeDtypeStruct((t, d), F32),
        compiler_params=_cparams(("parallel",), 48),
        name="ple",
    )(x2d, p2d, g_norm.reshape(1, d).astype(F32), w_gate, w_up, g_post.reshape(1, d).astype(F32))


def kernel(x, p, w_in, da_q_norm, da_k_norm, da_lambda_q1, da_lambda_k1, da_lambda_q2, da_lambda_k2, da_subln, gla_gate_w2, gla_gate_b, gla_out_norm, w_branch_da, w_branch_gla, w_merge_gate, b_merge_gate, w_out, norm_mix, norm_ffn, router_w, router_b, w_gate_up, b_gate_up, w_down, b_down, norm_ple, w_ple_gate, w_ple_up, norm_ple_post):
    bsz, s_len, d = x.shape
    t = bsz * s_len
    depth = w_in.shape[0]
    qk_w = DA_HEADS * 2 * DA_HEAD_DIM
    v_w = DA_HEADS * DA_V_DIM
    gk_w = gla_gate_w2.shape[2]
    gv_w = w_branch_gla.shape[1]
    rest_w = 2 * gk_w + 2 * gv_w
    x2d = x.reshape(t, d)
    for i in range(depth):
        lambda_init = 0.8 - 0.6 * math.exp(-0.3 * i)
        h = _rmsnorm(x2d, norm_mix[i])
        w = w_in[i]
        c_v = 2 * qk_w
        c_rest = c_v + v_w
        qn = _proj(h, w[:, :qk_w].astype(BF16), out_dtype=BF16, tm=512, tn=1024,
                   gain=da_q_norm[i], scale=DA_HEAD_DIM ** -0.5 * LOG2E)
        kn = _proj(h, w[:, qk_w:c_v].astype(BF16), out_dtype=BF16, tm=512, tn=1024, gain=da_k_norm[i])
        vt = _proj_t(h, w[:, c_v:c_rest].T.astype(BF16), tm=DA_TQ, tn=1024)
        rest = _proj(h, w[:, c_rest:c_rest + rest_w].astype(BF16), out_dtype=BF16, tm=512, tn=1024)
        w_lr = jnp.zeros((d, LANES), BF16).at[:, :GLA_GATE_RANK].set(w[:, c_rest + rest_w:].astype(BF16))
        glr = _proj(h, w_lr, out_dtype=F32, tm=512, tn=LANES)

        lam_vecs = jnp.stack([da_lambda_q1[i], da_lambda_k1[i], da_lambda_q2[i], da_lambda_k2[i]])
        y_da = _diff_attention(qn, kn, vt, lam_vecs, da_subln[i], bsz, s_len, lambda_init, DA_TQ)

        w2p = jnp.zeros((LANES, gk_w), F32).at[:GLA_GATE_RANK].set(gla_gate_w2[i].astype(F32))
        y_gla = _gla(rest, (0, gk_w, 2 * gk_w, 2 * gk_w + gv_w), glr, w2p,
                     gla_gate_b[i], gla_out_norm[i], bsz, s_len)

        mixed = _merge(h, y_da, y_gla, w_merge_gate[i].astype(BF16), b_merge_gate[i].reshape(1, 2 * d).astype(F32),
                       w_branch_da[i].astype(BF16), w_branch_gla[i].astype(BF16))
        x2d = _out_proj(x2d, mixed, w_out[i].astype(BF16))

        x2d = _moe(x2d, norm_ffn[i], router_w[i], router_b[i], w_gate_up[i], b_gate_up[i], w_down[i], b_down[i])

        x2d = _ple(x2d, p[i].reshape(t, p.shape[-1]), norm_ple[i], w_ple_gate[i].astype(BF16),
                   w_ple_up[i].astype(BF16), norm_ple_post[i])
    return x2d.reshape(bsz, s_len, d)
```

```python
import functools
import math

import jax
import jax.numpy as jnp
from jax import lax
from jax.experimental import pallas as pl
from jax.experimental.pallas import tpu as pltpu

F32 = jnp.float32
BF16 = jnp.bfloat16
HIGHEST = lax.Precision.HIGHEST

CHUNK = 64
CHUNK_SHIFT = CHUNK.bit_length() - 1
RMS_EPS = 1e-6
DA_HEADS = 8
DA_HEAD_DIM = 128
DA_V_DIM = 2 * DA_HEAD_DIM
GLA_HEADS = 4
GLA_GATE_RANK = 16
GLA_TAU = 16.0
N_EXPERTS = 32
TOP_K = 4
SWIGLU_LIMIT = 7.0
SWIGLU_ALPHA = 1.702

DA_TQ = 1024
DA_HPS = 2
DA_ONES_ROWS = 16
LOG2E = 1.4426950408889634

LANES = 128
NEG_BIG = -1e30

MIB = 1024 * 1024


def _cparams(sem, vmem_mib, flags=None):
    return pltpu.CompilerParams(dimension_semantics=sem, vmem_limit_bytes=vmem_mib * MIB, flags=flags)


def _dot(a, b):
    return jnp.dot(a, b, preferred_element_type=F32)


def _dot_nt(a, b):
    return lax.dot_general(a, b, (((1,), (1,)), ((), ())), preferred_element_type=F32)


def _dot_tn(a, b):
    return lax.dot_general(a, b, (((0,), (0,)), ((), ())), preferred_element_type=F32)


def _rmsnorm_kernel(x_ref, g_ref, o_ref):
    x = x_ref[...]
    ms = jnp.mean(x * x, axis=-1, keepdims=True)
    o_ref[...] = (x * lax.rsqrt(ms + RMS_EPS) * g_ref[...]).astype(o_ref.dtype)


def _rmsnorm(x2d, gain, tm=512):
    t, d = x2d.shape
    return pl.pallas_call(
        _rmsnorm_kernel,
        grid=(t // tm,),
        in_specs=[pl.BlockSpec((tm, d), lambda i: (i, 0)), pl.BlockSpec((1, d), lambda i: (0, 0))],
        out_specs=pl.BlockSpec((tm, d), lambda i: (i, 0)),
        out_shape=jax.ShapeDtypeStruct((t, d), BF16),
        compiler_params=_cparams(("parallel",), 32),
        name="rmsnorm",
    )(x2d, gain.reshape(1, d).astype(F32))


def _proj_kernel(a_ref, w_ref, g_ref, o_ref, *, group_norm, scale):
    acc = _dot(a_ref[...], w_ref[...])
    if group_norm:
        for c in range(acc.shape[1] // DA_HEAD_DIM):
            blk = acc[:, c * DA_HEAD_DIM:(c + 1) * DA_HEAD_DIM]
            ms = jnp.mean(blk * blk, axis=-1, keepdims=True)
            y = blk * lax.rsqrt(ms + RMS_EPS) * g_ref[...] * scale
            o_ref[:, c * DA_HEAD_DIM:(c + 1) * DA_HEAD_DIM] = y.astype(o_ref.dtype)
    else:
        o_ref[...] = acc.astype(o_ref.dtype)


def _proj(a, w, *, out_dtype, tm, tn, gain=None, scale=1.0):
    t, k = a.shape
    n = w.shape[1]
    group_norm = gain is not None
    g = (gain if group_norm else jnp.ones((DA_HEAD_DIM,), F32)).reshape(1, DA_HEAD_DIM).astype(F32)
    return pl.pallas_call(
        functools.partial(_proj_kernel, group_norm=group_norm, scale=scale),
        grid=(n // tn, t // tm),
        in_specs=[
            pl.BlockSpec((tm, k), lambda j, i: (i, 0)),
            pl.BlockSpec((k, tn), lambda j, i: (0, j)),
            pl.BlockSpec((1, DA_HEAD_DIM), lambda j, i: (0, 0)),
        ],
        out_specs=pl.BlockSpec((tm, tn), lambda j, i: (i, j)),
        out_shape=jax.ShapeDtypeStruct((t, n), out_dtype),
        compiler_params=_cparams(("parallel", "parallel"), 48),
        name="proj",
    )(a, w, g)


def _proj_t_kernel(wt_ref, a_ref, o_ref):
    o_ref[0] = _dot_nt(wt_ref[...], a_ref[...]).astype(o_ref.dtype)


def _proj_t(a, wt, *, tm, tn):
    t, k = a.shape
    n = wt.shape[0]
    return pl.pallas_call(
        _proj_t_kernel,
        grid=(n // tn, t // tm),
        in_specs=[
            pl.BlockSpec((tn, k), lambda j, i: (j, 0)),
            pl.BlockSpec((tm, k), lambda j, i: (i, 0)),
        ],
        out_specs=pl.BlockSpec((1, tn, tm), lambda j, i: (i, j, 0)),
        out_shape=jax.ShapeDtypeStruct((t // tm, n, tm), BF16),
        compiler_params=_cparams(("parallel", "parallel"), 48),
        name="proj_t",
    )(wt, a)


def _da_kernel(slope_ref, q_ref, k_ref, vt_ref, lamv_ref, subg_ref, o_ref,
               kaug_ref, qaug_ref, vaug_ref, m_ref, acc_ref, *, tq, s_len, lambda_init):
    hp = pl.program_id(1)
    i = pl.program_id(2)
    hd = DA_HEAD_DIM
    dv = DA_V_DIM
    nq = vt_ref.shape[0]
    chains = [(hh, c) for hh in range(DA_HPS) for c in range(2)]
    slopes2 = [slope_ref[hp * DA_HPS + hh] * LOG2E for hh in range(DA_HPS)]

    @pl.when(i == 0)
    def _():
        pos = lax.broadcasted_iota(jnp.int32, (s_len, hd), 0)
        lane = lax.broadcasted_iota(jnp.int32, (s_len, hd), 1)
        piece = jnp.where(lane < 6, jnp.where((lane & 1) == 0, (pos >> 4) << 4, pos & 15), 0)
        piece = piece.astype(F32).astype(BF16)
        for hh, c in chains:
            col0 = hh * dv + c * hd
            kaug_ref[hh * 2 + c, :, :hd] = k_ref[:, col0:col0 + hd]
            kaug_ref[hh * 2 + c, :, hd:] = piece
        for hh in range(DA_HPS):
            for j in range(nq):
                vaug_ref[hh, j, :dv, :] = vt_ref[j, hh * dv:(hh + 1) * dv, :]
                vaug_ref[hh, j, dv:, :] = jnp.ones((vaug_ref.shape[2] - dv, tq), BF16)

    lane_q = lax.broadcasted_iota(jnp.int32, (tq, hd), 1)
    for hh in range(DA_HPS):
        s_full = jnp.full((tq, hd), slopes2[hh], F32)
        s1 = s_full.astype(BF16).astype(F32)
        r1 = s_full - s1
        s2 = r1.astype(BF16).astype(F32)
        s3 = r1 - s2
        slope_cols = jnp.where(lane_q < 2, s1, jnp.where(lane_q < 4, s2, jnp.where(lane_q < 6, s3, 0.0)))
        slope_cols = slope_cols.astype(BF16)
        for c in range(2):
            col0 = hh * dv + c * hd
            qaug_ref[hh * 2 + c, :, :hd] = q_ref[:, col0:col0 + hd]
            qaug_ref[hh * 2 + c, :, hd:] = slope_cols

    m_ref[...] = jnp.full(m_ref.shape, NEG_BIG, F32)
    acc_ref[...] = jnp.zeros(acc_ref.shape, F32)

    def update(ci, st, va):
        m_old = m_ref[ci]
        m_new = jnp.maximum(m_old, jnp.max(st, axis=0, keepdims=True))
        alpha = jnp.exp2(m_old - m_new)
        p = jnp.exp2(st - m_new)
        acc_ref[ci] = alpha * acc_ref[ci] + _dot(va, p.astype(BF16))
        m_ref[ci] = m_new

    def past_tile(j, carry):
        r0 = pl.multiple_of(j * tq, tq)
        for hh, c in chains:
            ci = hh * 2 + c
            update(ci, _dot_nt(kaug_ref[ci, pl.ds(r0, tq), :], qaug_ref[ci]), vaug_ref[hh, j])
        return carry

    lax.fori_loop(0, i, past_tile, 0)

    r0 = pl.multiple_of(i * tq, tq)
    krow = lax.broadcasted_iota(jnp.int32, (tq, tq), 0)
    qcol = lax.broadcasted_iota(jnp.int32, (tq, tq), 1)
    ahead = jnp.maximum(krow - qcol, 0).astype(F32)
    allowed = (krow >> CHUNK_SHIFT) <= (qcol >> CHUNK_SHIFT)
    for hh, c in chains:
        ci = hh * 2 + c
        st = _dot_nt(kaug_ref[ci, pl.ds(r0, tq), :], qaug_ref[ci])
        update(ci, jnp.where(allowed, st + (-2.0 * slopes2[hh]) * ahead, NEG_BIG), vaug_ref[hh, i])

    lamv = lamv_ref[...]
    lam = (jnp.exp(jnp.sum(lamv[0:1] * lamv[1:2], axis=-1, keepdims=True))
           - jnp.exp(jnp.sum(lamv[2:3] * lamv[3:4], axis=-1, keepdims=True)) + lambda_init)
    for hh in range(DA_HPS):
        a0 = acc_ref[hh * 2]
        a1 = acc_ref[hh * 2 + 1]
        ot = a0[:dv, :] / a0[dv:dv + 1, :] - lam * (a1[:dv, :] / a1[dv:dv + 1, :])
        ms = jnp.mean(ot * ot, axis=0, keepdims=True)
        o = (ot * lax.rsqrt(ms + RMS_EPS)).T
        o_ref[:, hh * dv:(hh + 1) * dv] = (o * subg_ref[...] * (1.0 - lambda_init)).astype(o_ref.dtype)


def _diff_attention(qn, kn, vt, lam_vecs, subln_g, bsz, s_len, lambda_init, tq):
    assert s_len <= 4096 and s_len % tq == 0 and tq % CHUNK == 0 and DA_HEADS % DA_HPS == 0
    t = bsz * s_len
    nq = s_len // tq
    w = DA_HPS * DA_V_DIM
    rows_aug = DA_V_DIM + DA_ONES_ROWS
    slopes = jnp.exp2(-8.0 * jnp.arange(1, DA_HEADS + 1, dtype=F32) / DA_HEADS)
    grid_spec = pltpu.PrefetchScalarGridSpec(
        num_scalar_prefetch=1,
        grid=(bsz, DA_HEADS // DA_HPS, nq),
        in_specs=[
            pl.BlockSpec((tq, w), lambda b, h, i, s: (b * nq + i, h)),
            pl.BlockSpec((s_len, w), lambda b, h, i, s: (b, h)),
            pl.BlockSpec((nq, w, tq), lambda b, h, i, s: (b, h, 0)),
            pl.BlockSpec((4, DA_HEAD_DIM), lambda b, h, i, s: (0, 0)),
            pl.BlockSpec((1, DA_V_DIM), lambda b, h, i, s: (0, 0)),
        ],
        out_specs=pl.BlockSpec((tq, w), lambda b, h, i, s: (b * nq + i, h)),
        scratch_shapes=[
            pltpu.VMEM((2 * DA_HPS, s_len, 2 * DA_HEAD_DIM), BF16),
            pltpu.VMEM((2 * DA_HPS, tq, 2 * DA_HEAD_DIM), BF16),
            pltpu.VMEM((DA_HPS, nq, rows_aug, tq), BF16),
            pltpu.VMEM((2 * DA_HPS, 1, tq), F32),
            pltpu.VMEM((2 * DA_HPS, rows_aug, tq), F32),
        ],
    )
    return pl.pallas_call(
        functools.partial(_da_kernel, tq=tq, s_len=s_len, lambda_init=lambda_init),
        grid_spec=grid_spec,
        out_shape=jax.ShapeDtypeStruct((t, DA_HEADS * DA_V_DIM), BF16),
        compiler_params=_cparams(("parallel", "parallel", "arbitrary"), 56),
        name="diff_attention",
    )(slopes, qn, kn, vt, lam_vecs.astype(F32), subln_g.reshape(1, DA_V_DIM).astype(F32))


def _gla_kernel(q_ref, k_ref, v_ref, g_ref, glr_ref, w2_ref, gb_ref, og_ref, o_ref, state_ref,
                *, tb, dk, dv):
    @pl.when(pl.program_id(1) == 0)
    def _():
        state_ref[...] = jnp.zeros(state_ref.shape, F32)

    row = lax.broadcasted_iota(jnp.int32, (CHUNK, CHUNK), 0)
    col = lax.broadcasted_iota(jnp.int32, (CHUNK, CHUNK), 1)
    lower = row >= col
    tri = lower.astype(F32)
    mid = CHUNK // 2

    def chunk(c, carry):
        r0 = pl.multiple_of(c * CHUNK, CHUNK)
        z = jnp.dot(glr_ref[pl.ds(r0, CHUNK), :], w2_ref[...], precision=HIGHEST,
                    preferred_element_type=F32) + gb_ref[...]
        log_a = (jnp.minimum(z, 0.0) - jnp.log1p(jnp.exp(-jnp.abs(z)))) * (1.0 / GLA_TAU)
        b_all = jnp.dot(tri, log_a, precision=HIGHEST, preferred_element_type=F32)
        for h in range(GLA_HEADS):
            ks = slice(h * dk, (h + 1) * dk)
            vs = slice(h * dv, (h + 1) * dv)
            b = b_all[:, ks]
            b_last = b[CHUNK - 1:CHUNK, :]
            b_mid = b[mid:mid + 1, :]
            q = q_ref[pl.ds(r0, CHUNK), ks].astype(F32) * (dk ** -0.5)
            k = k_ref[pl.ds(r0, CHUNK), ks].astype(F32)
            v = v_ref[pl.ds(r0, CHUNK), vs]
            e_fwd = jnp.exp(b - b_mid)
            e_bwd = jnp.exp(b_mid - b)
            a_lo = _dot_nt((q * e_fwd).astype(BF16), (k * e_bwd).astype(BF16))
            a_up = _dot_nt((q * e_bwd).astype(BF16), (k * e_fwd).astype(BF16))
            attn = jnp.where(lower, a_lo, a_up)
            state = state_ref[h]
            o = _dot(attn.astype(BF16), v) + _dot_nt((q * jnp.exp(b)).astype(BF16), state.astype(BF16))
            kd = (k * jnp.exp(b_last - b)).astype(BF16)
            state_ref[h] = state * jnp.exp(b_last) + _dot_tn(v, kd)
            ms = jnp.mean(o * o, axis=-1, keepdims=True)
            g = g_ref[pl.ds(r0, CHUNK), vs].astype(F32)
            y = o * lax.rsqrt(ms + RMS_EPS) * og_ref[...] * (g * jax.nn.sigmoid(g))
            o_ref[pl.ds(r0, CHUNK), vs] = y.astype(o_ref.dtype)
        return carry

    lax.fori_loop(0, tb // CHUNK, chunk, 0)


def _gla(src, cols, glr, w2p, gate_b, out_g, bsz, s_len, tb=512):
    t = bsz * s_len
    nb = s_len // tb
    kw = w2p.shape[1]
    dk = kw // GLA_HEADS
    dv = out_g.shape[0]
    vw = dv * GLA_HEADS
    cq, ck, cv, cg = cols
    return pl.pallas_call(
        functools.partial(_gla_kernel, tb=tb, dk=dk, dv=dv),
        grid=(bsz, nb),
        in_specs=[
            pl.BlockSpec((tb, kw), lambda b, i: (b * nb + i, cq // kw)),
            pl.BlockSpec((tb, kw), lambda b, i: (b * nb + i, ck // kw)),
            pl.BlockSpec((tb, vw), lambda b, i: (b * nb + i, cv // vw)),
            pl.BlockSpec((tb, vw), lambda b, i: (b * nb + i, cg // vw)),
            pl.BlockSpec((tb, LANES), lambda b, i: (b * nb + i, 0)),
            pl.BlockSpec((LANES, kw), lambda b, i: (0, 0)),
            pl.BlockSpec((1, kw), lambda b, i: (0, 0)),
            pl.BlockSpec((1, dv), lambda b, i: (0, 0)),
        ],
        out_specs=pl.BlockSpec((tb, vw), lambda b, i: (b * nb + i, 0)),
        out_shape=jax.ShapeDtypeStruct((t, vw), BF16),
        scratch_shapes=[pltpu.VMEM((GLA_HEADS, dv, dk), F32)],
        compiler_params=_cparams(("parallel", "arbitrary"), 48),
        name="gla",
    )(src, src, src, src, glr, w2p, gate_b.reshape(1, kw).astype(F32), out_g.reshape(1, dv).astype(F32))


def _merge_kernel(h_ref, ya_ref, yb_ref, wga_ref, wgb_ref, ba_ref, bb_ref, wa_ref, wb_ref, o_ref):
    h = h_ref[...]
    ga = jax.nn.sigmoid(_dot(h, wga_ref[...]) + ba_ref[...])
    gb = jax.nn.sigmoid(_dot(h, wgb_ref[...]) + bb_ref[...])
    mixed = ga * _dot(ya_ref[...], wa_ref[...]) + gb * _dot(yb_ref[...], wb_ref[...])
    o_ref[...] = mixed.astype(o_ref.dtype)


def _merge(h, y_da, y_gla, w_gate, b_gate, w_da, w_gla, tm=512, tn=512):
    t, d = h.shape
    nb = d // tn
    act = lambda: pl.BlockSpec((tm, d), lambda j, i: (i, 0))
    return pl.pallas_call(
        _merge_kernel,
        grid=(nb, t // tm),
        in_specs=[
            act(), act(), act(),
            pl.BlockSpec((d, tn), lambda j, i: (0, j)),
            pl.BlockSpec((d, tn), lambda j, i: (0, nb + j)),
            pl.BlockSpec((1, tn), lambda j, i: (0, j)),
            pl.BlockSpec((1, tn), lambda j, i: (0, nb + j)),
            pl.BlockSpec((d, tn), lambda j, i: (0, j)),
            pl.BlockSpec((d, tn), lambda j, i: (0, j)),
        ],
        out_specs=pl.BlockSpec((tm, tn), lambda j, i: (i, j)),
        out_shape=jax.ShapeDtypeStruct((t, d), BF16),
        compiler_params=_cparams(("parallel", "parallel"), 48),
        name="merge",
    )(h, y_da, y_gla, w_gate, w_gate, b_gate, b_gate, w_da, w_gla)


def _out_proj_kernel(x_ref, m_ref, w_ref, o_ref):
    o_ref[...] = x_ref[...] + _dot(m_ref[...], w_ref[...])


def _out_proj(x2d, mixed, w_out, tm=256):
    t, d = x2d.shape
    return pl.pallas_call(
        _out_proj_kernel,
        grid=(t // tm,),
        in_specs=[
            pl.BlockSpec((tm, d), lambda i: (i, 0)),
            pl.BlockSpec((tm, d), lambda i: (i, 0)),
            pl.BlockSpec((d, d), lambda i: (0, 0)),
        ],
        out_specs=pl.BlockSpec((tm, d), lambda i: (i, 0)),
        out_shape=jax.ShapeDtypeStruct((t, d), F32),
        compiler_params=_cparams(("parallel",), 48),
        name="out_proj",
    )(x2d, mixed, w_out)


ROW_SUBLANES = 8


def _pack_rows(val, store):
    half = val.shape[1] // 2
    assert half == ROW_SUBLANES * LANES
    lo = lax.bitcast_convert_type(val[:, :half].astype(BF16).astype(F32), jnp.uint32)
    hi = lax.bitcast_convert_type(val[:, half:].astype(BF16).astype(F32), jnp.uint32)
    packed = (lo >> 16) | (hi & jnp.uint32(0xFFFF0000))
    for c in range(ROW_SUBLANES):
        store(c, packed[:, c * LANES:(c + 1) * LANES])


def _unpack_chunk(chunk):
    lo = lax.bitcast_convert_type(chunk << 16, F32)
    hi = lax.bitcast_convert_type(chunk & jnp.uint32(0xFFFF0000), F32)
    return lo, hi


def _router_kernel(x_ref, g_ref, rw_ref, rb_ref, hp_ref, idx_ref, gate_ref):
    x = x_ref[...]
    tm = x.shape[0]
    ms = jnp.mean(x * x, axis=-1, keepdims=True)
    h = x * lax.rsqrt(ms + RMS_EPS) * g_ref[...]

    def store(c, chunk):
        hp_ref[pl.ds(c, tm, stride=ROW_SUBLANES), :] = chunk

    _pack_rows(h, store)

    logits = jnp.dot(h, rw_ref[...], precision=HIGHEST, preferred_element_type=F32) + rb_ref[...]
    lane = lax.broadcasted_iota(jnp.int32, logits.shape, 1)
    vals, idxs = [], []
    for _ in range(TOP_K):
        m = jnp.max(logits, axis=-1, keepdims=True)
        idx = jnp.min(jnp.where(logits == m, lane, LANES), axis=-1, keepdims=True)
        vals.append(m)
        idxs.append(idx)
        logits = jnp.where(lane == idx, -jnp.inf, logits)
    exps = [jnp.exp(v - vals[0]) for v in vals]
    denom = exps[0] + exps[1] + exps[2] + exps[3]
    idx_out = jnp.zeros(lane.shape, jnp.int32)
    gate_out = jnp.zeros(lane.shape, F32)
    for k in range(TOP_K):
        idx_out = jnp.where(lane == k, idxs[k], idx_out)
        gate_out = jnp.where(lane == k, exps[k] / denom, gate_out)
    idx_ref[...] = idx_out
    gate_ref[...] = gate_out


def _router(x2d, gain, router_w, router_b, tm=256):
    t, d = x2d.shape
    e = router_w.shape[1]
    rw = jnp.zeros((d, LANES), F32).at[:, :e].set(router_w.astype(F32))
    rb = jnp.full((1, LANES), NEG_BIG, F32).at[0, :e].set(router_b.astype(F32))
    return pl.pallas_call(
        _router_kernel,
        grid=(t // tm,),
        in_specs=[
            pl.BlockSpec((tm, d), lambda i: (i, 0)),
            pl.BlockSpec((1, d), lambda i: (0, 0)),
            pl.BlockSpec((d, LANES), lambda i: (0, 0)),
            pl.BlockSpec((1, LANES), lambda i: (0, 0)),
        ],
        out_specs=[
            pl.BlockSpec((tm * ROW_SUBLANES, LANES), lambda i: (i, 0)),
            pl.BlockSpec((tm, LANES), lambda i: (i, 0)),
            pl.BlockSpec((tm, LANES), lambda i: (i, 0)),
        ],
        out_shape=[
            jax.ShapeDtypeStruct((t * ROW_SUBLANES, LANES), jnp.uint32),
            jax.ShapeDtypeStruct((t, LANES), jnp.int32),
            jax.ShapeDtypeStruct((t, LANES), F32),
        ],
        compiler_params=_cparams(("parallel",), 32),
        name="router",
    )(x2d, gain.reshape(1, d).astype(F32), rw, rb)


MOE_SUB = 256
MOE_NSUB = 6
MOE_TF = 256


def _expert_kernel(ge_ref, gs_ref, gn_ref, ng_ref, tot_ref,
                   hp_ref, idx_ref, wgu_ref, bgu_ref, wdn_ref, bdn_ref, ys_ref,
                   idx_smem, xu_ref, yst_ref, xb_ref, acc_ref, wgu_b_ref, wdn_p_ref, wdn_b_ref, zero_ref,
                   sem_idx, sem_in, sem_out, *, n_sub_total):
    g = pl.program_id(0)
    f = pl.program_id(1)
    nf = pl.num_programs(1)
    ng = ng_ref[0]
    active = g < ng
    nsub = gn_ref[g]
    start = gs_ref[g]
    half = xb_ref.shape[2] // 2
    tf = wdn_ref.shape[1]
    sub_words = MOE_SUB * ROW_SUBLANES
    sub_shift = MOE_SUB.bit_length() - 1
    assert MOE_SUB == 1 << sub_shift

    def fetch_idx(gg):
        idx_cp = pltpu.make_async_copy(idx_ref.at[pl.ds(gs_ref[gg], MOE_NSUB)], idx_smem, sem_idx)
        idx_cp.start()
        idx_cp.wait()

    def issue_rows(lo, hi):
        def pair(q, c):
            for u in range(2):
                row = lo + 2 * q + u
                s = row >> sub_shift
                r = row & (MOE_SUB - 1)
                src = hp_ref.at[pl.ds(pl.multiple_of(idx_smem[s, 0, r] * ROW_SUBLANES, ROW_SUBLANES),
                                      ROW_SUBLANES)]
                dst = xu_ref.at[s, pl.ds(pl.multiple_of(r * ROW_SUBLANES, ROW_SUBLANES), ROW_SUBLANES)]
                pltpu.make_async_copy(src, dst, sem_in.at[s]).start(priority=u)
            return c

        lax.fori_loop(0, jnp.maximum(hi - lo, 0) >> 1, pair, 0)

    def wait_rows(s):
        pltpu.make_async_copy(hp_ref.at[pl.ds(0, sub_words)], xu_ref.at[s], sem_in.at[s]).wait()

    def out_copy(s, first_sub):
        r0 = pl.multiple_of((first_sub + s) * sub_words, sub_words)
        return pltpu.make_async_copy(yst_ref.at[s], ys_ref.at[pl.ds(r0, sub_words)], sem_out)

    def wait_out(gg):
        def wait(s, c):
            out_copy(s, gs_ref[gg]).wait()
            return c

        lax.fori_loop(0, gn_ref[gg], wait, 0)

    @pl.when(jnp.logical_and(g == 0, f == 0))
    def _():
        zero_ref[...] = jnp.zeros(zero_ref.shape, zero_ref.dtype)
        fetch_idx(0)
        issue_rows(0, gn_ref[0] * MOE_SUB)

    @pl.when(jnp.logical_and(active, f == 0))
    def _():
        def unpack(s, c):
            wait_rows(s)
            for ch in range(ROW_SUBLANES):
                lo, hi = _unpack_chunk(xu_ref[s, pl.ds(ch, MOE_SUB, stride=ROW_SUBLANES), :])
                xb_ref[s, :, ch * LANES:(ch + 1) * LANES] = lo.astype(BF16)
                xb_ref[s, :, half + ch * LANES:half + (ch + 1) * LANES] = hi.astype(BF16)
            acc_ref[s] = jnp.broadcast_to(bdn_ref[0], acc_ref.shape[1:])
            return c

        lax.fori_loop(0, nsub, unpack, 0)

        @pl.when(g + 1 < ng)
        def _():
            fetch_idx(g + 1)

    @pl.when(active)
    def _():
        nxt = jnp.minimum(g + 1, pl.num_programs(0) - 1)
        next_rows = jnp.where(g + 1 < ng, gn_ref[nxt] * MOE_SUB, 0)
        iters2 = 2 * nf * nsub
        per_it = 2 * lax.div(next_rows + iters2 - 1, iters2)

        def issue_next(s):
            lo = (f * nsub + s) * per_it
            issue_rows(lo, jnp.minimum(lo + per_it, next_rows))

        wgu_b_ref[...] = wgu_ref[0].astype(BF16)
        hl = LANES // 2
        for c in range(wdn_p_ref.shape[0]):
            cols = slice(c * LANES, (c + 1) * LANES)
            for m in range(tf // LANES):
                wdn_p_ref[c, pl.ds(m * LANES, hl, stride=2), :] = wdn_ref[0, m * LANES:m * LANES + hl, cols]
                wdn_p_ref[c, pl.ds(m * LANES + 1, hl, stride=2), :] = (
                    wdn_ref[0, m * LANES + hl:(m + 1) * LANES, cols])
            wdn_b_ref[:, cols] = wdn_p_ref[c].astype(BF16)
        bgu = bgu_ref[0]
        lane = lax.broadcasted_iota(jnp.int32, (MOE_SUB, LANES), 1)
        even = (lane & 1) == 0

        def gate_up(s):
            gu = _dot(xb_ref[s], wgu_b_ref[...]) + bgu
            gates, ups = [], []
            for m in range(tf // LANES):
                a = gu[:, 2 * m * LANES:(2 * m + 1) * LANES]
                b = gu[:, (2 * m + 1) * LANES:(2 * m + 2) * LANES]
                gates.append(jnp.where(even, a, pltpu.roll(b, 1, 1)))
                ups.append(jnp.where(even, pltpu.roll(a, LANES - 1, 1), b))
            gate = jnp.minimum(jnp.concatenate(gates, axis=1), SWIGLU_LIMIT)
            up = jnp.clip(jnp.concatenate(ups, axis=1), -SWIGLU_LIMIT, SWIGLU_LIMIT)
            return ((up + 1.0) * gate * jax.nn.sigmoid(SWIGLU_ALPHA * gate)).astype(BF16)

        def down(s, act):
            acc_ref[s] += _dot(act, wdn_b_ref[...])

        def body(s, act_prev):
            issue_next(s)
            act = gate_up(s)
            down(s - 1, act_prev)
            return act

        issue_next(0)
        act_last = lax.fori_loop(1, nsub, body, gate_up(0))
        down(nsub - 1, act_last)

    @pl.when(jnp.logical_and(active, f == nf - 1))
    def _():
        @pl.when(g > 0)
        def _():
            wait_out(g - 1)

        def issue(s, c):
            def store(ch, chunk):
                yst_ref[s, pl.ds(ch, MOE_SUB, stride=ROW_SUBLANES), :] = chunk

            _pack_rows(acc_ref[s], store)
            out_copy(s, start).start()
            return c

        lax.fori_loop(0, nsub, issue, 0)

    @pl.when(jnp.logical_and(g == pl.num_programs(0) - 1, f == nf - 1))
    def _():
        wait_out(ng - 1)

        def fill(s, c):
            r0 = pl.multiple_of(s * sub_words, sub_words)
            cp = pltpu.make_async_copy(zero_ref, ys_ref.at[pl.ds(r0, sub_words)], sem_out)
            cp.start()
            cp.wait()
            return c

        lax.fori_loop(tot_ref[0], n_sub_total, fill, 0)


def _experts(hp, row_tok, grp_expert, grp_start, grp_nsub, n_groups, tot_sub, w_gu, b_gu, w_dn, b_dn):
    n_rows = row_tok.shape[0]
    sub_words = MOE_SUB * ROW_SUBLANES
    idx2 = jnp.concatenate([row_tok.reshape(n_rows // MOE_SUB, 1, MOE_SUB),
                            jnp.zeros((MOE_NSUB, 1, MOE_SUB), jnp.int32)], axis=0)
    e, d, f2 = w_gu.shape
    fdim = f2 // 2
    tf = MOE_TF
    nf = fdim // tf
    n_grp = grp_expert.shape[0]

    def wsel(g, f, ge, gs, gn, ng, tot):
        on = g < ng[0]
        last = jnp.maximum(ng[0] - 1, 0)
        return jnp.where(on, ge[g], ge[last]), jnp.where(on, f, nf - 1)

    def w_gu_map(g, f, *pre):
        ee, ff = wsel(g, f, *pre)
        return (ee, 0, ff)

    def w_dn_map(g, f, *pre):
        ee, ff = wsel(g, f, *pre)
        return (ee, ff, 0)

    def b_dn_map(g, f, *pre):
        ee, _ = wsel(g, f, *pre)
        return (ee, 0, 0)

    grid_spec = pltpu.PrefetchScalarGridSpec(
        num_scalar_prefetch=5,
        grid=(n_grp, nf),
        in_specs=[
            pl.BlockSpec(memory_space=pl.ANY),
            pl.BlockSpec(memory_space=pl.ANY),
            pl.BlockSpec((1, d, 2 * tf), w_gu_map),
            pl.BlockSpec((1, 1, 2 * tf), w_gu_map),
            pl.BlockSpec((1, tf, d), w_dn_map),
            pl.BlockSpec((1, 1, d), b_dn_map),
        ],
        out_specs=pl.BlockSpec(memory_space=pl.ANY),
        scratch_shapes=[
            pltpu.SMEM((MOE_NSUB, 1, MOE_SUB), jnp.int32),
            pltpu.VMEM((MOE_NSUB, sub_words, LANES), jnp.uint32),
            pltpu.VMEM((MOE_NSUB, sub_words, LANES), jnp.uint32),
            pltpu.VMEM((MOE_NSUB, MOE_SUB, d), BF16),
            pltpu.VMEM((MOE_NSUB, MOE_SUB, d), F32),
            pltpu.VMEM((d, 2 * tf), BF16),
            pltpu.VMEM((d // LANES, tf, LANES), F32),
            pltpu.VMEM((tf, d), BF16),
            pltpu.VMEM((sub_words, LANES), jnp.uint32),
            pltpu.SemaphoreType.DMA(()),
            pltpu.SemaphoreType.DMA((MOE_NSUB,)),
            pltpu.SemaphoreType.DMA(()),
        ],
    )
    return pl.pallas_call(
        functools.partial(_expert_kernel, n_sub_total=n_rows // MOE_SUB),
        grid_spec=grid_spec,
        out_shape=jax.ShapeDtypeStruct((n_rows * ROW_SUBLANES, LANES), jnp.uint32),
        compiler_params=_cparams(("arbitrary", "arbitrary"), 56),
        name="moe_experts",
    )(grp_expert, grp_start, grp_nsub, n_groups, tot_sub, hp, idx2,
      w_gu, b_gu.reshape(e, 1, f2).astype(F32), w_dn, b_dn.reshape(e, 1, d).astype(F32))


def _combine_kernel(pr_ref, gate_ref, x_ref, ys_ref, o_ref, buf_ref, sem, *, tc):
    def copy(p):
        tok = p >> 2
        return pltpu.make_async_copy(
            ys_ref.at[pr_ref[0, 0, p]],
            buf_ref.at[p & (TOP_K - 1), pl.ds(pl.multiple_of(tok * ROW_SUBLANES, ROW_SUBLANES), ROW_SUBLANES)],
            sem)

    def issue(p2, carry):
        copy(2 * p2).start(priority=0)
        copy(2 * p2 + 1).start(priority=1)
        return carry

    def wait(p, carry):
        copy(p).wait()
        return carry

    lax.fori_loop(0, tc * TOP_K // 2, issue, 0, unroll=4)
    lax.fori_loop(0, tc * TOP_K, wait, 0, unroll=8)
    gates = gate_ref[...]
    half = x_ref.shape[1] // 2
    for ch in range(ROW_SUBLANES):
        lo_cols = slice(ch * LANES, (ch + 1) * LANES)
        hi_cols = slice(half + ch * LANES, half + (ch + 1) * LANES)
        acc_lo = x_ref[:, lo_cols]
        acc_hi = x_ref[:, hi_cols]
        for k in range(TOP_K):
            lo, hi = _unpack_chunk(buf_ref[k, pl.ds(ch, tc, stride=ROW_SUBLANES), :])
            acc_lo = acc_lo + gates[:, k:k + 1] * lo
            acc_hi = acc_hi + gates[:, k:k + 1] * hi
        o_ref[:, lo_cols] = acc_lo
        o_ref[:, hi_cols] = acc_hi


def _combine(x2d, gates, pair_row, ys, tc=256):
    t, d = x2d.shape
    return pl.pallas_call(
        functools.partial(_combine_kernel, tc=tc),
        grid=(t // tc,),
        in_specs=[
            pl.BlockSpec((1, 1, tc * TOP_K), lambda i: (i, 0, 0), memory_space=pltpu.SMEM),
            pl.BlockSpec((tc, LANES), lambda i: (i, 0)),
            pl.BlockSpec((tc, d), lambda i: (i, 0)),
            pl.BlockSpec(memory_space=pl.ANY),
        ],
        out_specs=pl.BlockSpec((tc, d), lambda i: (i, 0)),
        out_shape=jax.ShapeDtypeStruct((t, d), F32),
        scratch_shapes=[pltpu.VMEM((TOP_K, tc * ROW_SUBLANES, LANES), jnp.uint32), pltpu.SemaphoreType.DMA(())],
        compiler_params=_cparams(("arbitrary",), 32),
        name="moe_combine",
    )(pair_row.reshape(t // tc, 1, tc * TOP_K), gates, x2d, ys)


def _moe(x2d, gain, router_w, router_b, w_gu, b_gu, w_dn, b_dn):
    t, d = x2d.shape
    e = router_w.shape[1]
    hp, idx, gates = _router(x2d, gain, router_w, router_b)

    n_pairs = t * TOP_K
    flat_e = idx[:, :TOP_K].reshape(n_pairs)
    onehot = (flat_e[:, None] == jnp.arange(e, dtype=jnp.int32)[None, :]).astype(jnp.int32)
    csum = jnp.cumsum(onehot, axis=0)
    rank = jnp.sum(onehot * csum, axis=1) - 1
    counts = csum[-1]
    sub_e = (counts + MOE_SUB - 1) // MOE_SUB
    sub_end = jnp.cumsum(sub_e)
    sub_start = sub_end - sub_e
    dest = (jnp.sum(onehot * (sub_start * MOE_SUB)[None, :], axis=1) + rank).astype(jnp.int32)
    n_sub_total = n_pairs // MOE_SUB + e
    n_rows = n_sub_total * MOE_SUB
    row_tok = jnp.zeros((n_rows,), jnp.int32).at[dest].set(jnp.arange(n_pairs, dtype=jnp.int32) // TOP_K)

    grp_e = (sub_e + MOE_NSUB - 1) // MOE_NSUB
    grp_end = jnp.cumsum(grp_e)
    n_grp_max = (n_sub_total + MOE_NSUB - 1) // MOE_NSUB + e
    gidx = jnp.arange(n_grp_max, dtype=jnp.int32)
    g_exp = jnp.minimum(jnp.searchsorted(grp_end, gidx, side="right"), e - 1).astype(jnp.int32)
    g_local = gidx - (grp_end - grp_e)[g_exp]
    g_on = gidx < grp_end[-1]
    g_start = jnp.where(g_on, sub_start[g_exp] + g_local * MOE_NSUB, 0).astype(jnp.int32)
    g_nsub = jnp.where(g_on, jnp.minimum(MOE_NSUB, sub_e[g_exp] - g_local * MOE_NSUB), 0).astype(jnp.int32)

    ys = _experts(hp, row_tok, g_exp, g_start, g_nsub,
                  grp_end[-1].astype(jnp.int32).reshape(1), sub_end[-1].astype(jnp.int32).reshape(1),
                  w_gu, b_gu, w_dn, b_dn)
    return _combine(x2d, gates, dest, ys.reshape(n_rows, ROW_SUBLANES, LANES))


def _ple_kernel(x_ref, p_ref, gn_ref, wg_ref, wu_ref, gp_ref, o_ref):
    x = x_ref[...]
    ms = jnp.mean(x * x, axis=-1, keepdims=True)
    h = (x * lax.rsqrt(ms + RMS_EPS) * gn_ref[...]).astype(BF16)
    gate = jax.nn.sigmoid(_dot(h, wg_ref[...]))
    up = _dot(p_ref[...].astype(BF16), wu_ref[...])
    ms_u = jnp.mean(up * up, axis=-1, keepdims=True)
    o_ref[...] = x + up * lax.rsqrt(ms_u + RMS_EPS) * gp_ref[...] * gate


def _ple(x2d, p2d, g_norm, w_gate, w_up, g_post, tm=256):
    t, d = x2d.shape
    pd = p2d.shape[1]
    return pl.pallas_call(
        _ple_kernel,
        grid=(t // tm,),
        in_specs=[
            pl.BlockSpec((tm, d), lambda i: (i, 0)),
            pl.BlockSpec((tm, pd), lambda i: (i, 0)),
            pl.BlockSpec((1, d), lambda i: (0, 0)),
            pl.BlockSpec((d, d), lambda i: (0, 0)),
            pl.BlockSpec((pd, d), lambda i: (0, 0)),
            pl.BlockSpec((1, d), lambda i: (0, 0)),
        ],
        out_specs=pl.BlockSpec((tm, d), lambda i: (i, 0)),
        out_shape=jax.ShapeDtypeStruct((t, d), F32),
        compiler_params=_cparams(("parallel",), 48),
        name="ple",
    )(x2d, p2d, g_norm.reshape(1, d).astype(F32), w_gate, w_up, g_post.reshape(1, d).astype(F32))


def kernel(x, p, w_in, da_q_norm, da_k_norm, da_lambda_q1, da_lambda_k1, da_lambda_q2, da_lambda_k2, da_subln, gla_gate_w2, gla_gate_b, gla_out_norm, w_branch_da, w_branch_gla, w_merge_gate, b_merge_gate, w_out, norm_mix, norm_ffn, router_w, router_b, w_gate_up, b_gate_up, w_down, b_down, norm_ple, w_ple_gate, w_ple_up, norm_ple_post):
    bsz, s_len, d = x.shape
    t = bsz * s_len
    depth = w_in.shape[0]
    qk_w = DA_HEADS * 2 * DA_HEAD_DIM
    v_w = DA_HEADS * DA_V_DIM
    gk_w = gla_gate_w2.shape[2]
    gv_w = w_branch_gla.shape[1]
    rest_w = 2 * gk_w + 2 * gv_w
    x2d = x.reshape(t, d)
    for i in range(depth):
        lambda_init = 0.8 - 0.6 * math.exp(-0.3 * i)
        h = _rmsnorm(x2d, norm_mix[i])
        w = w_in[i]
        c_v = 2 * qk_w
        c_rest = c_v + v_w
        qn = _proj(h, w[:, :qk_w].astype(BF16), out_dtype=BF16, tm=512, tn=1024,
                   gain=da_q_norm[i], scale=DA_HEAD_DIM ** -0.5 * LOG2E)
        kn = _proj(h, w[:, qk_w:c_v].astype(BF16), out_dtype=BF16, tm=512, tn=1024, gain=da_k_norm[i])
        vt = _proj_t(h, w[:, c_v:c_rest].T.astype(BF16), tm=DA_TQ, tn=1024)
        rest = _proj(h, w[:, c_rest:c_rest + rest_w].astype(BF16), out_dtype=BF16, tm=512, tn=1024)
        w_lr = jnp.zeros((d, LANES), BF16).at[:, :GLA_GATE_RANK].set(w[:, c_rest + rest_w:].astype(BF16))
        glr = _proj(h, w_lr, out_dtype=F32, tm=512, tn=LANES)

        lam_vecs = jnp.stack([da_lambda_q1[i], da_lambda_k1[i], da_lambda_q2[i], da_lambda_k2[i]])
        y_da = _diff_attention(qn, kn, vt, lam_vecs, da_subln[i], bsz, s_len, lambda_init, DA_TQ)

        w2p = jnp.zeros((LANES, gk_w), F32).at[:GLA_GATE_RANK].set(gla_gate_w2[i].astype(F32))
        y_gla = _gla(rest, (0, gk_w, 2 * gk_w, 2 * gk_w + gv_w), glr, w2p,
                     gla_gate_b[i], gla_out_norm[i], bsz, s_len)

        mixed = _merge(h, y_da, y_gla, w_merge_gate[i].astype(BF16), b_merge_gate[i].reshape(1, 2 * d).astype(F32),
                       w_branch_da[i].astype(BF16), w_branch_gla[i].astype(BF16))
        x2d = _out_proj(x2d, mixed, w_out[i].astype(BF16))

        x2d = _moe(x2d, norm_ffn[i], router_w[i], router_b[i], w_gate_up[i], b_gate_up[i], w_down[i], b_down[i])

        x2d = _ple(x2d, p[i].reshape(t, p.shape[-1]), norm_ple[i], w_ple_gate[i].astype(BF16),
                   w_ple_up[i].astype(BF16), norm_ple_post[i])
    return x2d.reshape(bsz, s_len, d)
```

```python
import functools
import math

import jax
import jax.numpy as jnp
from jax import lax
from jax.experimental import pallas as pl
from jax.experimental.pallas import tpu as pltpu

F32 = jnp.float32
BF16 = jnp.bfloat16
HIGHEST = lax.Precision.HIGHEST

CHUNK = 64
CHUNK_SHIFT = CHUNK.bit_length() - 1
RMS_EPS = 1e-6
DA_HEADS = 8
DA_HEAD_DIM = 128
DA_V_DIM = 2 * DA_HEAD_DIM
GLA_HEADS = 4
GLA_GATE_RANK = 16
GLA_TAU = 16.0
N_EXPERTS = 32
TOP_K = 4
SWIGLU_LIMIT = 7.0
SWIGLU_ALPHA = 1.702

DA_TQ = 1024
DA_HPS = 2
DA_ONES_ROWS = 16
LOG2E = 1.4426950408889634

LANES = 128
NEG_BIG = -1e30

MIB = 1024 * 1024


def _cparams(sem, vmem_mib, flags=None):
    return pltpu.CompilerParams(dimension_semantics=sem, vmem_limit_bytes=vmem_mib * MIB, flags=flags)


def _dot(a, b):
    return jnp.dot(a, b, preferred_element_type=F32)


def _dot_nt(a, b):
    return lax.dot_general(a, b, (((1,), (1,)), ((), ())), preferred_element_type=F32)


def _dot_tn(a, b):
    return lax.dot_general(a, b, (((0,), (0,)), ((), ())), preferred_element_type=F32)


def _rmsnorm_kernel(x_ref, g_ref, o_ref):
    x = x_ref[...]
    ms = jnp.mean(x * x, axis=-1, keepdims=True)
    o_ref[...] = (x * lax.rsqrt(ms + RMS_EPS) * g_ref[...]).astype(o_ref.dtype)


def _rmsnorm(x2d, gain, tm=512):
    t, d = x2d.shape
    return pl.pallas_call(
        _rmsnorm_kernel,
        grid=(t // tm,),
        in_specs=[pl.BlockSpec((tm, d), lambda i: (i, 0)), pl.BlockSpec((1, d), lambda i: (0, 0))],
        out_specs=pl.BlockSpec((tm, d), lambda i: (i, 0)),
        out_shape=jax.ShapeDtypeStruct((t, d), BF16),
        compiler_params=_cparams(("parallel",), 32),
        name="rmsnorm",
    )(x2d, gain.reshape(1, d).astype(F32))


def _proj_kernel(a_ref, w_ref, g_ref, o_ref, wb_ref, *, group_norm, scale):
    @pl.when(pl.program_id(1) == 0)
    def _():
        wb_ref[...] = w_ref[...].astype(BF16)

    acc = _dot(a_ref[...], wb_ref[...])
    if group_norm:
        for c in range(acc.shape[1] // DA_HEAD_DIM):
            blk = acc[:, c * DA_HEAD_DIM:(c + 1) * DA_HEAD_DIM]
            ms = jnp.mean(blk * blk, axis=-1, keepdims=True)
            y = blk * lax.rsqrt(ms + RMS_EPS) * g_ref[...] * scale
            o_ref[:, c * DA_HEAD_DIM:(c + 1) * DA_HEAD_DIM] = y.astype(o_ref.dtype)
    else:
        o_ref[...] = acc.astype(o_ref.dtype)


def _proj(a, w, col0, n, *, out_dtype, tm, tn, gain=None, scale=1.0):
    t, k = a.shape
    assert col0 % tn == 0 and n % tn == 0
    cb = col0 // tn
    group_norm = gain is not None
    g = (gain if group_norm else jnp.ones((DA_HEAD_DIM,), F32)).reshape(1, DA_HEAD_DIM).astype(F32)
    return pl.pallas_call(
        functools.partial(_proj_kernel, group_norm=group_norm, scale=scale),
        grid=(n // tn, t // tm),
        in_specs=[
            pl.BlockSpec((tm, k), lambda j, i: (i, 0)),
            pl.BlockSpec((k, tn), lambda j, i: (0, cb + j)),
            pl.BlockSpec((1, DA_HEAD_DIM), lambda j, i: (0, 0)),
        ],
        out_specs=pl.BlockSpec((tm, tn), lambda j, i: (i, j)),
        out_shape=jax.ShapeDtypeStruct((t, n), out_dtype),
        scratch_shapes=[pltpu.VMEM((k, tn), BF16)],
        compiler_params=_cparams(("parallel", "arbitrary"), 48),
        name="proj",
    )(a, w.astype(F32), g)


def _proj_t_kernel(w_ref, a_ref, o_ref, wt_ref):
    @pl.when(pl.program_id(1) == 0)
    def _():
        wt_ref[...] = w_ref[...].T.astype(BF16)

    o_ref[0] = _dot_nt(wt_ref[...], a_ref[...]).astype(o_ref.dtype)


def _proj_t(a, w, col0, n, *, tm, tn):
    t, k = a.shape
    assert col0 % tn == 0 and n % tn == 0
    cb = col0 // tn
    return pl.pallas_call(
        _proj_t_kernel,
        grid=(n // tn, t // tm),
        in_specs=[
            pl.BlockSpec((k, tn), lambda j, i: (0, cb + j)),
            pl.BlockSpec((tm, k), lambda j, i: (i, 0)),
        ],
        out_specs=pl.BlockSpec((1, tn, tm), lambda j, i: (i, j, 0)),
        out_shape=jax.ShapeDtypeStruct((t // tm, n, tm), BF16),
        scratch_shapes=[pltpu.VMEM((tn, k), BF16)],
        compiler_params=_cparams(("parallel", "arbitrary"), 48),
        name="proj_t",
    )(w.astype(F32), a)


def _da_kernel(slope_ref, q_ref, k_ref, vt_ref, lamv_ref, subg_ref, o_ref,
               kaug_ref, qaug_ref, vaug_ref, m_ref, acc_ref, *, tq, s_len, lambda_init):
    hp = pl.program_id(1)
    i = pl.program_id(2)
    hd = DA_HEAD_DIM
    dv = DA_V_DIM
    nq = vt_ref.shape[0]
    chains = [(hh, c) for hh in range(DA_HPS) for c in range(2)]
    slopes2 = [slope_ref[hp * DA_HPS + hh] * LOG2E for hh in range(DA_HPS)]

    @pl.when(i == 0)
    def _():
        pos = lax.broadcasted_iota(jnp.int32, (s_len, hd), 0)
        lane = lax.broadcasted_iota(jnp.int32, (s_len, hd), 1)
        piece = jnp.where(lane < 6, jnp.where((lane & 1) == 0, (pos >> 4) << 4, pos & 15), 0)
        piece = piece.astype(F32).astype(BF16)
        for hh, c in chains:
            col0 = hh * dv + c * hd
            kaug_ref[hh * 2 + c, :, :hd] = k_ref[:, col0:col0 + hd]
            kaug_ref[hh * 2 + c, :, hd:] = piece
        for hh in range(DA_HPS):
            for j in range(nq):
                vaug_ref[hh, j, :dv, :] = vt_ref[j, hh * dv:(hh + 1) * dv, :]
                vaug_ref[hh, j, dv:, :] = jnp.ones((vaug_ref.shape[2] - dv, tq), BF16)

    lane_q = lax.broadcasted_iota(jnp.int32, (tq, hd), 1)
    for hh in range(DA_HPS):
        s_full = jnp.full((tq, hd), slopes2[hh], F32)
        s1 = s_full.astype(BF16).astype(F32)
        r1 = s_full - s1
        s2 = r1.astype(BF16).astype(F32)
        s3 = r1 - s2
        slope_cols = jnp.where(lane_q < 2, s1, jnp.where(lane_q < 4, s2, jnp.where(lane_q < 6, s3, 0.0)))
        slope_cols = slope_cols.astype(BF16)
        for c in range(2):
            col0 = hh * dv + c * hd
            qaug_ref[hh * 2 + c, :, :hd] = q_ref[:, col0:col0 + hd]
            qaug_ref[hh * 2 + c, :, hd:] = slope_cols

    m_ref[...] = jnp.full(m_ref.shape, NEG_BIG, F32)
    acc_ref[...] = jnp.zeros(acc_ref.shape, F32)

    def update(ci, st, va):
        m_old = m_ref[ci]
        m_new = jnp.maximum(m_old, jnp.max(st, axis=0, keepdims=True))
        alpha = jnp.exp2(m_old - m_new)
        p = jnp.exp2(st - m_new)
        acc_ref[ci] = alpha * acc_ref[ci] + _dot(va, p.astype(BF16))
        m_ref[ci] = m_new

    def past_tile(j, carry):
        r0 = pl.multiple_of(j * tq, tq)
        for hh, c in chains:
            ci = hh * 2 + c
            update(ci, _dot_nt(kaug_ref[ci, pl.ds(r0, tq), :], qaug_ref[ci]), vaug_ref[hh, j])
        return carry

    lax.fori_loop(0, i, past_tile, 0)

    r0 = pl.multiple_of(i * tq, tq)
    krow = lax.broadcasted_iota(jnp.int32, (tq, tq), 0)
    qcol = lax.broadcasted_iota(jnp.int32, (tq, tq), 1)
    ahead = jnp.maximum(krow - qcol, 0).astype(F32)
    allowed = (krow >> CHUNK_SHIFT) <= (qcol >> CHUNK_SHIFT)
    for hh, c in chains:
        ci = hh * 2 + c
        st = _dot_nt(kaug_ref[ci, pl.ds(r0, tq), :], qaug_ref[ci])
        update(ci, jnp.where(allowed, st + (-2.0 * slopes2[hh]) * ahead, NEG_BIG), vaug_ref[hh, i])

    lamv = lamv_ref[...]
    lam = (jnp.exp(jnp.sum(lamv[0:1] * lamv[1:2], axis=-1, keepdims=True))
           - jnp.exp(jnp.sum(lamv[2:3] * lamv[3:4], axis=-1, keepdims=True)) + lambda_init)
    for hh in range(DA_HPS):
        a0 = acc_ref[hh * 2]
        a1 = acc_ref[hh * 2 + 1]
        ot = a0[:dv, :] / a0[dv:dv + 1, :] - lam * (a1[:dv, :] / a1[dv:dv + 1, :])
        ms = jnp.mean(ot * ot, axis=0, keepdims=True)
        o = (ot * lax.rsqrt(ms + RMS_EPS)).T
        o_ref[:, hh * dv:(hh + 1) * dv] = (o * subg_ref[...] * (1.0 - lambda_init)).astype(o_ref.dtype)


def _diff_attention(qn, kn, vt, lam_vecs, subln_g, bsz, s_len, lambda_init, tq):
    assert s_len <= 4096 and s_len % tq == 0 and tq % CHUNK == 0 and DA_HEADS % DA_HPS == 0
    t = bsz * s_len
    nq = s_len // tq
    w = DA_HPS * DA_V_DIM
    rows_aug = DA_V_DIM + DA_ONES_ROWS
    slopes = jnp.exp2(-8.0 * jnp.arange(1, DA_HEADS + 1, dtype=F32) / DA_HEADS)
    grid_spec = pltpu.PrefetchScalarGridSpec(
        num_scalar_prefetch=1,
        grid=(bsz, DA_HEADS // DA_HPS, nq),
        in_specs=[
            pl.BlockSpec((tq, w), lambda b, h, i, s: (b * nq + i, h)),
            pl.BlockSpec((s_len, w), lambda b, h, i, s: (b, h)),
            pl.BlockSpec((nq, w, tq), lambda b, h, i, s: (b, h, 0)),
            pl.BlockSpec((4, DA_HEAD_DIM), lambda b, h, i, s: (0, 0)),
            pl.BlockSpec((1, DA_V_DIM), lambda b, h, i, s: (0, 0)),
        ],
        out_specs=pl.BlockSpec((tq, w), lambda b, h, i, s: (b * nq + i, h)),
        scratch_shapes=[
            pltpu.VMEM((2 * DA_HPS, s_len, 2 * DA_HEAD_DIM), BF16),
            pltpu.VMEM((2 * DA_HPS, tq, 2 * DA_HEAD_DIM), BF16),
            pltpu.VMEM((DA_HPS, nq, rows_aug, tq), BF16),
            pltpu.VMEM((2 * DA_HPS, 1, tq), F32),
            pltpu.VMEM((2 * DA_HPS, rows_aug, tq), F32),
        ],
    )
    return pl.pallas_call(
        functools.partial(_da_kernel, tq=tq, s_len=s_len, lambda_init=lambda_init),
        grid_spec=grid_spec,
        out_shape=jax.ShapeDtypeStruct((t, DA_HEADS * DA_V_DIM), BF16),
        compiler_params=_cparams(("parallel", "parallel", "arbitrary"), 56),
        name="diff_attention",
    )(slopes, qn, kn, vt, lam_vecs.astype(F32), subln_g.reshape(1, DA_V_DIM).astype(F32))


def _gla_kernel(q_ref, k_ref, v_ref, g_ref, glr_ref, w2_ref, gb_ref, og_ref, o_ref, state_ref,
                *, tb, dk, dv):
    @pl.when(pl.program_id(1) == 0)
    def _():
        state_ref[...] = jnp.zeros(state_ref.shape, F32)

    row = lax.broadcasted_iota(jnp.int32, (CHUNK, CHUNK), 0)
    col = lax.broadcasted_iota(jnp.int32, (CHUNK, CHUNK), 1)
    lower = row >= col
    tri = lower.astype(F32)
    mid = CHUNK // 2

    def chunk(c, carry):
        r0 = pl.multiple_of(c * CHUNK, CHUNK)
        z = jnp.dot(glr_ref[pl.ds(r0, CHUNK), :], w2_ref[...], precision=HIGHEST,
                    preferred_element_type=F32) + gb_ref[...]
        log_a = (jnp.minimum(z, 0.0) - jnp.log1p(jnp.exp(-jnp.abs(z)))) * (1.0 / GLA_TAU)
        b_all = jnp.dot(tri, log_a, precision=HIGHEST, preferred_element_type=F32)
        for h in range(GLA_HEADS):
            ks = slice(h * dk, (h + 1) * dk)
            vs = slice(h * dv, (h + 1) * dv)
            b = b_all[:, ks]
            b_last = b[CHUNK - 1:CHUNK, :]
            b_mid = b[mid:mid + 1, :]
            q = q_ref[pl.ds(r0, CHUNK), ks].astype(F32) * (dk ** -0.5)
            k = k_ref[pl.ds(r0, CHUNK), ks].astype(F32)
            v = v_ref[pl.ds(r0, CHUNK), vs]
            e_fwd = jnp.exp(b - b_mid)
            e_bwd = jnp.exp(b_mid - b)
            a_lo = _dot_nt((q * e_fwd).astype(BF16), (k * e_bwd).astype(BF16))
            a_up = _dot_nt((q * e_bwd).astype(BF16), (k * e_fwd).astype(BF16))
            attn = jnp.where(lower, a_lo, a_up)
            state = state_ref[h]
            o = _dot(attn.astype(BF16), v) + _dot_nt((q * jnp.exp(b)).astype(BF16), state.astype(BF16))
            kd = (k * jnp.exp(b_last - b)).astype(BF16)
            state_ref[h] = state * jnp.exp(b_last) + _dot_tn(v, kd)
            ms = jnp.mean(o * o, axis=-1, keepdims=True)
            g = g_ref[pl.ds(r0, CHUNK), vs].astype(F32)
            y = o * lax.rsqrt(ms + RMS_EPS) * og_ref[...] * (g * jax.nn.sigmoid(g))
            o_ref[pl.ds(r0, CHUNK), vs] = y.astype(o_ref.dtype)
        return carry

    lax.fori_loop(0, tb // CHUNK, chunk, 0)


def _gla(src, cols, glr, w2p, gate_b, out_g, bsz, s_len, tb=512):
    t = bsz * s_len
    nb = s_len // tb
    kw = w2p.shape[1]
    dk = kw // GLA_HEADS
    dv = out_g.shape[0]
    vw = dv * GLA_HEADS
    cq, ck, cv, cg = cols
    return pl.pallas_call(
        functools.partial(_gla_kernel, tb=tb, dk=dk, dv=dv),
        grid=(bsz, nb),
        in_specs=[
            pl.BlockSpec((tb, kw), lambda b, i: (b * nb + i, cq // kw)),
            pl.BlockSpec((tb, kw), lambda b, i: (b * nb + i, ck // kw)),
            pl.BlockSpec((tb, vw), lambda b, i: (b * nb + i, cv // vw)),
            pl.BlockSpec((tb, vw), lambda b, i: (b * nb + i, cg // vw)),
            pl.BlockSpec((tb, LANES), lambda b, i: (b * nb + i, 0)),
            pl.BlockSpec((LANES, kw), lambda b, i: (0, 0)),
            pl.BlockSpec((1, kw), lambda b, i: (0, 0)),
            pl.BlockSpec((1, dv), lambda b, i: (0, 0)),
        ],
        out_specs=pl.BlockSpec((tb, vw), lambda b, i: (b * nb + i, 0)),
        out_shape=jax.ShapeDtypeStruct((t, vw), BF16),
        scratch_shapes=[pltpu.VMEM((GLA_HEADS, dv, dk), F32)],
        compiler_params=_cparams(("parallel", "arbitrary"), 48),
        name="gla",
    )(src, src, src, src, glr, w2p, gate_b.reshape(1, kw).astype(F32), out_g.reshape(1, dv).astype(F32))


def _merge_kernel(h_ref, ya_ref, yb_ref, wga_ref, wgb_ref, ba_ref, bb_ref, wa_ref, wb_ref, o_ref):
    h = h_ref[...]
    ga = jax.nn.sigmoid(_dot(h, wga_ref[...]) + ba_ref[...])
    gb = jax.nn.sigmoid(_dot(h, wgb_ref[...]) + bb_ref[...])
    mixed = ga * _dot(ya_ref[...], wa_ref[...]) + gb * _dot(yb_ref[...], wb_ref[...])
    o_ref[...] = mixed.astype(o_ref.dtype)


def _merge(h, y_da, y_gla, w_gate, b_gate, w_da, w_gla, tm=512, tn=512):
    t, d = h.shape
    nb = d // tn
    act = lambda: pl.BlockSpec((tm, d), lambda j, i: (i, 0))
    return pl.pallas_call(
        _merge_kernel,
        grid=(nb, t // tm),
        in_specs=[
            act(), act(), act(),
            pl.BlockSpec((d, tn), lambda j, i: (0, j)),
            pl.BlockSpec((d, tn), lambda j, i: (0, nb + j)),
            pl.BlockSpec((1, tn), lambda j, i: (0, j)),
            pl.BlockSpec((1, tn), lambda j, i: (0, nb + j)),
            pl.BlockSpec((d, tn), lambda j, i: (0, j)),
            pl.BlockSpec((d, tn), lambda j, i: (0, j)),
        ],
        out_specs=pl.BlockSpec((tm, tn), lambda j, i: (i, j)),
        out_shape=jax.ShapeDtypeStruct((t, d), BF16),
        compiler_params=_cparams(("parallel", "parallel"), 48),
        name="merge",
    )(h, y_da, y_gla, w_gate, w_gate, b_gate, b_gate, w_da, w_gla)


def _out_proj_kernel(x_ref, m_ref, w_ref, o_ref):
    o_ref[...] = x_ref[...] + _dot(m_ref[...], w_ref[...])


def _out_proj(x2d, mixed, w_out, tm=256):
    t, d = x2d.shape
    return pl.pallas_call(
        _out_proj_kernel,
        grid=(t // tm,),
        in_specs=[
            pl.BlockSpec((tm, d), lambda i: (i, 0)),
            pl.BlockSpec((tm, d), lambda i: (i, 0)),
            pl.BlockSpec((d, d), lambda i: (0, 0)),
        ],
        out_specs=pl.BlockSpec((tm, d), lambda i: (i, 0)),
        out_shape=jax.ShapeDtypeStruct((t, d), F32),
        compiler_params=_cparams(("parallel",), 48),
        name="out_proj",
    )(x2d, mixed, w_out)


ROW_SUBLANES = 8


def _pack_rows(val, store):
    half = val.shape[1] // 2
    assert half == ROW_SUBLANES * LANES
    lo = lax.bitcast_convert_type(val[:, :half].astype(BF16).astype(F32), jnp.uint32)
    hi = lax.bitcast_convert_type(val[:, half:].astype(BF16).astype(F32), jnp.uint32)
    packed = (lo >> 16) | (hi & jnp.uint32(0xFFFF0000))
    for c in range(ROW_SUBLANES):
        store(c, packed[:, c * LANES:(c + 1) * LANES])


def _unpack_chunk(chunk):
    lo = lax.bitcast_convert_type(chunk << 16, F32)
    hi = lax.bitcast_convert_type(chunk & jnp.uint32(0xFFFF0000), F32)
    return lo, hi


def _router_kernel(x_ref, g_ref, rw_ref, rb_ref, hp_ref, idx_ref, gate_ref):
    x = x_ref[...]
    tm = x.shape[0]
    ms = jnp.mean(x * x, axis=-1, keepdims=True)
    h = x * lax.rsqrt(ms + RMS_EPS) * g_ref[...]

    def store(c, chunk):
        hp_ref[pl.ds(c, tm, stride=ROW_SUBLANES), :] = chunk

    _pack_rows(h, store)

    logits = jnp.dot(h, rw_ref[...], precision=HIGHEST, preferred_element_type=F32) + rb_ref[...]
    lane = lax.broadcasted_iota(jnp.int32, logits.shape, 1)
    vals, idxs = [], []
    for _ in range(TOP_K):
        m = jnp.max(logits, axis=-1, keepdims=True)
        idx = jnp.min(jnp.where(logits == m, lane, LANES), axis=-1, keepdims=True)
        vals.append(m)
        idxs.append(idx)
        logits = jnp.where(lane == idx, -jnp.inf, logits)
    exps = [jnp.exp(v - vals[0]) for v in vals]
    denom = exps[0] + exps[1] + exps[2] + exps[3]
    idx_out = jnp.zeros(lane.shape, jnp.int32)
    gate_out = jnp.zeros(lane.shape, F32)
    for k in range(TOP_K):
        idx_out = jnp.where(lane == k, idxs[k], idx_out)
        gate_out = jnp.where(lane == k, exps[k] / denom, gate_out)
    idx_ref[...] = idx_out
    gate_ref[...] = gate_out


def _router(x2d, gain, router_w, router_b, tm=256):
    t, d = x2d.shape
    e = router_w.shape[1]
    rw = jnp.zeros((d, LANES), F32).at[:, :e].set(router_w.astype(F32))
    rb = jnp.full((1, LANES), NEG_BIG, F32).at[0, :e].set(router_b.astype(F32))
    return pl.pallas_call(
        _router_kernel,
        grid=(t // tm,),
        in_specs=[
            pl.BlockSpec((tm, d), lambda i: (i, 0)),
            pl.BlockSpec((1, d), lambda i: (0, 0)),
            pl.BlockSpec((d, LANES), lambda i: (0, 0)),
            pl.BlockSpec((1, LANES), lambda i: (0, 0)),
        ],
        out_specs=[
            pl.BlockSpec((tm * ROW_SUBLANES, LANES), lambda i: (i, 0)),
            pl.BlockSpec((tm, LANES), lambda i: (i, 0)),
            pl.BlockSpec((tm, LANES), lambda i: (i, 0)),
        ],
        out_shape=[
            jax.ShapeDtypeStruct((t * ROW_SUBLANES, LANES), jnp.uint32),
            jax.ShapeDtypeStruct((t, LANES), jnp.int32),
            jax.ShapeDtypeStruct((t, LANES), F32),
        ],
        compiler_params=_cparams(("parallel",), 32),
        name="router",
    )(x2d, gain.reshape(1, d).astype(F32), rw, rb)


MOE_SUB = 256
MOE_NSUB = 6
MOE_TF = 256
MOE_ISSUE_ROWS = 64


def _expert_kernel(ge_ref, gs_ref, gn_ref, ng_ref, tot_ref,
                   hp_ref, idx_ref, wgu_ref, bgu_ref, wdn_ref, bdn_ref, ys_ref,
                   idx_smem, xu_ref, yst_ref, xb_ref, acc_ref, wgu_b_ref, wdn_p_ref, wdn_b_ref, zero_ref,
                   sem_idx, sem_in, sem_out, *, n_sub_total):
    g = pl.program_id(0)
    f = pl.program_id(1)
    nf = pl.num_programs(1)
    ng = ng_ref[0]
    active = g < ng
    nsub = gn_ref[g]
    start = gs_ref[g]
    half = xb_ref.shape[2] // 2
    tf = wdn_ref.shape[1]
    sub_words = MOE_SUB * ROW_SUBLANES
    sub_shift = MOE_SUB.bit_length() - 1
    assert MOE_SUB == 1 << sub_shift

    def fetch_idx(gg):
        idx_cp = pltpu.make_async_copy(idx_ref.at[pl.ds(gs_ref[gg], MOE_NSUB)], idx_smem, sem_idx)
        idx_cp.start()
        idx_cp.wait()

    blk_rows = min(MOE_ISSUE_ROWS, MOE_SUB)

    def issue_block(blk, n_rows_group):
        base = blk * blk_rows

        @pl.when(base < n_rows_group)
        def _():
            s = base >> sub_shift
            r0 = base & (MOE_SUB - 1)
            for u in range(blk_rows):
                r = r0 + u
                src = hp_ref.at[pl.ds(pl.multiple_of(idx_smem[s, 0, r] * ROW_SUBLANES, ROW_SUBLANES),
                                      ROW_SUBLANES)]
                dst = xu_ref.at[s, pl.ds(pl.multiple_of(r * ROW_SUBLANES, ROW_SUBLANES), ROW_SUBLANES)]
                pltpu.make_async_copy(src, dst, sem_in.at[s]).start(priority=u % 2)

    def issue_blocks(first, n_rows_group):
        def one(blk, c):
            issue_block(blk, n_rows_group)
            return c

        lax.fori_loop(first, lax.div(n_rows_group + blk_rows - 1, blk_rows), one, 0)

    def wait_rows(s):
        pltpu.make_async_copy(hp_ref.at[pl.ds(0, sub_words)], xu_ref.at[s], sem_in.at[s]).wait()

    def out_copy(s, first_sub):
        r0 = pl.multiple_of((first_sub + s) * sub_words, sub_words)
        return pltpu.make_async_copy(yst_ref.at[s], ys_ref.at[pl.ds(r0, sub_words)], sem_out)

    def wait_out(gg):
        def wait(s, c):
            out_copy(s, gs_ref[gg]).wait()
            return c

        lax.fori_loop(0, gn_ref[gg], wait, 0)

    @pl.when(jnp.logical_and(g == 0, f == 0))
    def _():
        zero_ref[...] = jnp.zeros(zero_ref.shape, zero_ref.dtype)
        fetch_idx(0)
        issue_blocks(0, gn_ref[0] * MOE_SUB)

    @pl.when(jnp.logical_and(active, f == 0))
    def _():
        def unpack(s, c):
            wait_rows(s)
            for ch in range(ROW_SUBLANES):
                lo, hi = _unpack_chunk(xu_ref[s, pl.ds(ch, MOE_SUB, stride=ROW_SUBLANES), :])
                xb_ref[s, :, ch * LANES:(ch + 1) * LANES] = lo.astype(BF16)
                xb_ref[s, :, half + ch * LANES:half + (ch + 1) * LANES] = hi.astype(BF16)
            acc_ref[s] = jnp.broadcast_to(bdn_ref[0], acc_ref.shape[1:])
            return c

        lax.fori_loop(0, nsub, unpack, 0)

        @pl.when(g + 1 < ng)
        def _():
            fetch_idx(g + 1)

    @pl.when(active)
    def _():
        nxt = jnp.minimum(g + 1, pl.num_programs(0) - 1)
        next_rows = jnp.where(g + 1 < ng, gn_ref[nxt] * MOE_SUB, 0)

        def issue_next(s):
            issue_block(f * nsub + s, next_rows)

        wgu_b_ref[...] = wgu_ref[0].astype(BF16)
        hl = LANES // 2
        for c in range(wdn_p_ref.shape[0]):
            cols = slice(c * LANES, (c + 1) * LANES)
            for m in range(tf // LANES):
                wdn_p_ref[c, pl.ds(m * LANES, hl, stride=2), :] = wdn_ref[0, m * LANES:m * LANES + hl, cols]
                wdn_p_ref[c, pl.ds(m * LANES + 1, hl, stride=2), :] = (
                    wdn_ref[0, m * LANES + hl:(m + 1) * LANES, cols])
            wdn_b_ref[:, cols] = wdn_p_ref[c].astype(BF16)
        bgu = bgu_ref[0]
        lane = lax.broadcasted_iota(jnp.int32, (MOE_SUB, LANES), 1)
        even = (lane & 1) == 0

        def gate_up(s):
            gu = _dot(xb_ref[s], wgu_b_ref[...]) + bgu
            gates, ups = [], []
            for m in range(tf // LANES):
                a = gu[:, 2 * m * LANES:(2 * m + 1) * LANES]
                b = gu[:, (2 * m + 1) * LANES:(2 * m + 2) * LANES]
                gates.append(jnp.where(even, a, pltpu.roll(b, 1, 1)))
                ups.append(jnp.where(even, pltpu.roll(a, LANES - 1, 1), b))
            gate = jnp.minimum(jnp.concatenate(gates, axis=1), SWIGLU_LIMIT)
            up = jnp.clip(jnp.concatenate(ups, axis=1), -SWIGLU_LIMIT, SWIGLU_LIMIT)
            return ((up + 1.0) * gate * jax.nn.sigmoid(SWIGLU_ALPHA * gate)).astype(BF16)

        def down(s, act):
            acc_ref[s] += _dot(act, wdn_b_ref[...])

        def body(s, act_prev):
            issue_next(s)
            act = gate_up(s)
            down(s - 1, act_prev)
            return act

        issue_next(0)
        act_last = lax.fori_loop(1, nsub, body, gate_up(0))
        down(nsub - 1, act_last)

        @pl.when(f == nf - 1)
        def _():
            issue_blocks(nf * nsub, next_rows)

    @pl.when(jnp.logical_and(active, f == nf - 1))
    def _():
        @pl.when(g > 0)
        def _():
            wait_out(g - 1)

        def issue(s, c):
            def store(ch, chunk):
                yst_ref[s, pl.ds(ch, MOE_SUB, stride=ROW_SUBLANES), :] = chunk

            _pack_rows(acc_ref[s], store)
            out_copy(s, start).start()
            return c

        lax.fori_loop(0, nsub, issue, 0)

    @pl.when(jnp.logical_and(g == pl.num_programs(0) - 1, f == nf - 1))
    def _():
        wait_out(ng - 1)

        def fill(s, c):
            r0 = pl.multiple_of(s * sub_words, sub_words)
            cp = pltpu.make_async_copy(zero_ref, ys_ref.at[pl.ds(r0, sub_words)], sem_out)
            cp.start()
            cp.wait()
            return c

        lax.fori_loop(tot_ref[0], n_sub_total, fill, 0)


def _experts(hp, row_tok, grp_expert, grp_start, grp_nsub, n_groups, tot_sub, w_gu, b_gu, w_dn, b_dn):
    n_rows = row_tok.shape[0]
    sub_words = MOE_SUB * ROW_SUBLANES
    idx2 = jnp.concatenate([row_tok.reshape(n_rows // MOE_SUB, 1, MOE_SUB),
                            jnp.zeros((MOE_NSUB, 1, MOE_SUB), jnp.int32)], axis=0)
    e, d, f2 = w_gu.shape
    fdim = f2 // 2
    tf = MOE_TF
    nf = fdim // tf
    n_grp = grp_expert.shape[0]

    def wsel(g, f, ge, gs, gn, ng, tot):
        on = g < ng[0]
        last = jnp.maximum(ng[0] - 1, 0)
        return jnp.where(on, ge[g], ge[last]), jnp.where(on, f, nf - 1)

    def w_gu_map(g, f, *pre):
        ee, ff = wsel(g, f, *pre)
        return (ee, 0, ff)

    def w_dn_map(g, f, *pre):
        ee, ff = wsel(g, f, *pre)
        return (ee, ff, 0)

    def b_dn_map(g, f, *pre):
        ee, _ = wsel(g, f, *pre)
        return (ee, 0, 0)

    grid_spec = pltpu.PrefetchScalarGridSpec(
        num_scalar_prefetch=5,
        grid=(n_grp, nf),
        in_specs=[
            pl.BlockSpec(memory_space=pl.ANY),
            pl.BlockSpec(memory_space=pl.ANY),
            pl.BlockSpec((1, d, 2 * tf), w_gu_map),
            pl.BlockSpec((1, 1, 2 * tf), w_gu_map),
            pl.BlockSpec((1, tf, d), w_dn_map),
            pl.BlockSpec((1, 1, d), b_dn_map),
        ],
        out_specs=pl.BlockSpec(memory_space=pl.ANY),
        scratch_shapes=[
            pltpu.SMEM((MOE_NSUB, 1, MOE_SUB), jnp.int32),
            pltpu.VMEM((MOE_NSUB, sub_words, LANES), jnp.uint32),
            pltpu.VMEM((MOE_NSUB, sub_words, LANES), jnp.uint32),
            pltpu.VMEM((MOE_NSUB, MOE_SUB, d), BF16),
            pltpu.VMEM((MOE_NSUB, MOE_SUB, d), F32),
            pltpu.VMEM((d, 2 * tf), BF16),
            pltpu.VMEM((d // LANES, tf, LANES), F32),
            pltpu.VMEM((tf, d), BF16),
            pltpu.VMEM((sub_words, LANES), jnp.uint32),
            pltpu.SemaphoreType.DMA(()),
            pltpu.SemaphoreType.DMA((MOE_NSUB,)),
            pltpu.SemaphoreType.DMA(()),
        ],
    )
    return pl.pallas_call(
        functools.partial(_expert_kernel, n_sub_total=n_rows // MOE_SUB),
        grid_spec=grid_spec,
        out_shape=jax.ShapeDtypeStruct((n_rows * ROW_SUBLANES, LANES), jnp.uint32),
        compiler_params=_cparams(("arbitrary", "arbitrary"), 56),
        name="moe_experts",
    )(grp_expert, grp_start, grp_nsub, n_groups, tot_sub, hp, idx2,
      w_gu, b_gu.reshape(e, 1, f2).astype(F32), w_dn, b_dn.reshape(e, 1, d).astype(F32))


def _combine_kernel(pr_ref, gate_ref, x_ref, ys_ref, o_ref, buf_ref, sem, *, tc):
    def copy(p):
        tok = p >> 2
        return pltpu.make_async_copy(
            ys_ref.at[pr_ref[0, 0, p]],
            buf_ref.at[p & (TOP_K - 1), pl.ds(pl.multiple_of(tok * ROW_SUBLANES, ROW_SUBLANES), ROW_SUBLANES)],
            sem)

    def issue(p2, carry):
        copy(2 * p2).start(priority=0)
        copy(2 * p2 + 1).start(priority=1)
        return carry

    def wait(p, carry):
        copy(p).wait()
        return carry

    lax.fori_loop(0, tc * TOP_K // 2, issue, 0, unroll=4)
    lax.fori_loop(0, tc * TOP_K, wait, 0, unroll=8)
    gates = gate_ref[...]
    half = x_ref.shape[1] // 2
    for ch in range(ROW_SUBLANES):
        lo_cols = slice(ch * LANES, (ch + 1) * LANES)
        hi_cols = slice(half + ch * LANES, half + (ch + 1) * LANES)
        acc_lo = x_ref[:, lo_cols]
        acc_hi = x_ref[:, hi_cols]
        for k in range(TOP_K):
            lo, hi = _unpack_chunk(buf_ref[k, pl.ds(ch, tc, stride=ROW_SUBLANES), :])
            acc_lo = acc_lo + gates[:, k:k + 1] * lo
            acc_hi = acc_hi + gates[:, k:k + 1] * hi
        o_ref[:, lo_cols] = acc_lo
        o_ref[:, hi_cols] = acc_hi


def _combine(x2d, gates, pair_row, ys, tc=256):
    t, d = x2d.shape
    return pl.pallas_call(
        functools.partial(_combine_kernel, tc=tc),
        grid=(t // tc,),
        in_specs=[
            pl.BlockSpec((1, 1, tc * TOP_K), lambda i: (i, 0, 0), memory_space=pltpu.SMEM),
            pl.BlockSpec((tc, LANES), lambda i: (i, 0)),
            pl.BlockSpec((tc, d), lambda i: (i, 0)),
            pl.BlockSpec(memory_space=pl.ANY),
        ],
        out_specs=pl.BlockSpec((tc, d), lambda i: (i, 0)),
        out_shape=jax.ShapeDtypeStruct((t, d), F32),
        scratch_shapes=[pltpu.VMEM((TOP_K, tc * ROW_SUBLANES, LANES), jnp.uint32), pltpu.SemaphoreType.DMA(())],
        compiler_params=_cparams(("arbitrary",), 32),
        name="moe_combine",
    )(pair_row.reshape(t // tc, 1, tc * TOP_K), gates, x2d, ys)


def _moe(x2d, gain, router_w, router_b, w_gu, b_gu, w_dn, b_dn):
    t, d = x2d.shape
    e = router_w.shape[1]
    hp, idx, gates = _router(x2d, gain, router_w, router_b)

    n_pairs = t * TOP_K
    flat_e = idx[:, :TOP_K].reshape(n_pairs)
    onehot = (flat_e[:, None] == jnp.arange(e, dtype=jnp.int32)[None, :]).astype(jnp.int32)
    csum = jnp.cumsum(onehot, axis=0)
    rank = jnp.sum(onehot * csum, axis=1) - 1
    counts = csum[-1]
    sub_e = (counts + MOE_SUB - 1) // MOE_SUB
    sub_end = jnp.cumsum(sub_e)
    sub_start = sub_end - sub_e
    dest = (jnp.sum(onehot * (sub_start * MOE_SUB)[None, :], axis=1) + rank).astype(jnp.int32)
    n_sub_total = n_pairs // MOE_SUB + e
    n_rows = n_sub_total * MOE_SUB
    row_tok = jnp.zeros((n_rows,), jnp.int32).at[dest].set(jnp.arange(n_pairs, dtype=jnp.int32) // TOP_K)

    grp_e = (sub_e + MOE_NSUB - 1) // MOE_NSUB
    grp_end = jnp.cumsum(grp_e)
    n_grp_max = (n_sub_total + MOE_NSUB - 1) // MOE_NSUB + e
    gidx = jnp.arange(n_grp_max, dtype=jnp.int32)
    g_exp = jnp.minimum(jnp.searchsorted(grp_end, gidx, side="right"), e - 1).astype(jnp.int32)
    g_local = gidx - (grp_end - grp_e)[g_exp]
    g_on = gidx < grp_end[-1]
    g_start = jnp.where(g_on, sub_start[g_exp] + g_local * MOE_NSUB, 0).astype(jnp.int32)
    g_nsub = jnp.where(g_on, jnp.minimum(MOE_NSUB, sub_e[g_exp] - g_local * MOE_NSUB), 0).astype(jnp.int32)

    ys = _experts(hp, row_tok, g_exp, g_start, g_nsub,
                  grp_end[-1].astype(jnp.int32).reshape(1), sub_end[-1].astype(jnp.int32).reshape(1),
                  w_gu, b_gu, w_dn, b_dn)
    return _combine(x2d, gates, dest, ys.reshape(n_rows, ROW_SUBLANES, LANES))


def _ple_kernel(x_ref, p_ref, gn_ref, wg_ref, wu_ref, gp_ref, o_ref):
    x = x_ref[...]
    ms = jnp.mean(x * x, axis=-1, keepdims=True)
    h = (x * lax.rsqrt(ms + RMS_EPS) * gn_ref[...]).astype(BF16)
    gate = jax.nn.sigmoid(_dot(h, wg_ref[...]))
    up = _dot(p_ref[...].astype(BF16), wu_ref[...])
    ms_u = jnp.mean(up * up, axis=-1, keepdims=True)
    o_ref[...] = x + up * lax.rsqrt(ms_u + RMS_EPS) * gp_ref[...] * gate


def _ple(x2d, p2d, g_norm, w_gate, w_up, g_post, tm=256):
    t, d = x2d.shape
    pd = p2d.shape[1]
    return pl.pallas_call(
        _ple_kernel,
        grid=(t // tm,),
        in_specs=[
            pl.BlockSpec((tm, d), lambda i: (i, 0)),
            pl.BlockSpec((tm, pd), lambda i: (i, 0)),
            pl.BlockSpec((1, d), lambda i: (0, 0)),
            pl.BlockSpec((d, d), lambda i: (0, 0)),
            pl.BlockSpec((pd, d), lambda i: (0, 0)),
            pl.BlockSpec((1, d), lambda i: (0, 0)),
        ],
        out_specs=pl.BlockSpec((tm, d), lambda i: (i, 0)),
        out_shape=jax.ShapeDtypeStruct((t, d), F32),
        compiler_params=_cparams(("parallel",), 48),
        name="ple",
    )(x2d, p2d, g_norm.reshape(1, d).astype(F32), w_gate, w_up, g_post.reshape(1, d).astype(F32))


def kernel(x, p, w_in, da_q_norm, da_k_norm, da_lambda_q1, da_lambda_k1, da_lambda_q2, da_lambda_k2, da_subln, gla_gate_w2, gla_gate_b, gla_out_norm, w_branch_da, w_branch_gla, w_merge_gate, b_merge_gate, w_out, norm_mix, norm_ffn, router_w, router_b, w_gate_up, b_gate_up, w_down, b_down, norm_ple, w_ple_gate, w_ple_up, norm_ple_post):
    bsz, s_len, d = x.shape
    t = bsz * s_len
    depth = w_in.shape[0]
    qk_w = DA_HEADS * 2 * DA_HEAD_DIM
    v_w = DA_HEADS * DA_V_DIM
    gk_w = gla_gate_w2.shape[2]
    gv_w = w_branch_gla.shape[1]
    rest_w = 2 * gk_w + 2 * gv_w
    x2d = x.reshape(t, d)
    for i in range(depth):
        lambda_init = 0.8 - 0.6 * math.exp(-0.3 * i)
        h = _rmsnorm(x2d, norm_mix[i])
        w = w_in[i]
        c_v = 2 * qk_w
        c_rest = c_v + v_w
        qn = _proj(h, w, 0, qk_w, out_dtype=BF16, tm=512, tn=1024,
                   gain=da_q_norm[i], scale=DA_HEAD_DIM ** -0.5 * LOG2E)
        kn = _proj(h, w, qk_w, qk_w, out_dtype=BF16, tm=512, tn=1024, gain=da_k_norm[i])
        vt = _proj_t(h, w, c_v, v_w, tm=DA_TQ, tn=1024)
        rest = _proj(h, w, c_rest, rest_w, out_dtype=BF16, tm=512, tn=1024)
        w_lr = jnp.zeros((d, LANES), F32).at[:, :GLA_GATE_RANK].set(w[:, c_rest + rest_w:].astype(F32))
        glr = _proj(h, w_lr, 0, LANES, out_dtype=F32, tm=512, tn=LANES)

        lam_vecs = jnp.stack([da_lambda_q1[i], da_lambda_k1[i], da_lambda_q2[i], da_lambda_k2[i]])
        y_da = _diff_attention(qn, kn, vt, lam_vecs, da_subln[i], bsz, s_len, lambda_init, DA_TQ)

        w2p = jnp.zeros((LANES, gk_w), F32).at[:GLA_GATE_RANK].set(gla_gate_w2[i].astype(F32))
        y_gla = _gla(rest, (0, gk_w, 2 * gk_w, 2 * gk_w + gv_w), glr, w2p,
                     gla_gate_b[i], gla_out_norm[i], bsz, s_len)

        mixed = _merge(h, y_da, y_gla, w_merge_gate[i].astype(BF16), b_merge_gate[i].reshape(1, 2 * d).astype(F32),
                       w_branch_da[i].astype(BF16), w_branch_gla[i].astype(BF16))
        x2d = _out_proj(x2d, mixed, w_out[i].astype(BF16))

        x2d = _moe(x2d, norm_ffn[i], router_w[i], router_b[i], w_gate_up[i], b_gate_up[i], w_down[i], b_down[i])

        x2d = _ple(x2d, p[i].reshape(t, p.shape[-1]), norm_ple[i], w_ple_gate[i].astype(BF16),
                   w_ple_up[i].astype(BF16), norm_ple_post[i])
    return x2d.reshape(bsz, s_len, d)
```

```python
import functools
import math

import jax
import jax.numpy as jnp
from jax import lax
from jax.experimental import pallas as pl
from jax.experimental.pallas import tpu as pltpu

F32 = jnp.float32
BF16 = jnp.bfloat16
HIGHEST = lax.Precision.HIGHEST

CHUNK = 64
CHUNK_SHIFT = CHUNK.bit_length() - 1
RMS_EPS = 1e-6
DA_HEADS = 8
DA_HEAD_DIM = 128
DA_V_DIM = 2 * DA_HEAD_DIM
GLA_HEADS = 4
GLA_GATE_RANK = 16
GLA_TAU = 16.0
N_EXPERTS = 32
TOP_K = 4
SWIGLU_LIMIT = 7.0
SWIGLU_ALPHA = 1.702

DA_TQ = 1024
DA_HPS = 2
DA_ONES_ROWS = 16
LOG2E = 1.4426950408889634

LANES = 128
NEG_BIG = -1e30

MIB = 1024 * 1024


def _cparams(sem, vmem_mib, flags=None):
    return pltpu.CompilerParams(dimension_semantics=sem, vmem_limit_bytes=vmem_mib * MIB, flags=flags)


def _dot(a, b):
    return jnp.dot(a, b, preferred_element_type=F32)


def _dot_nt(a, b):
    return lax.dot_general(a, b, (((1,), (1,)), ((), ())), preferred_element_type=F32)


def _dot_tn(a, b):
    return lax.dot_general(a, b, (((0,), (0,)), ((), ())), preferred_element_type=F32)


def _rmsnorm_kernel(x_ref, g_ref, o_ref):
    x = x_ref[...]
    ms = jnp.mean(x * x, axis=-1, keepdims=True)
    o_ref[...] = (x * lax.rsqrt(ms + RMS_EPS) * g_ref[...]).astype(o_ref.dtype)


def _rmsnorm(x2d, gain, tm=512):
    t, d = x2d.shape
    return pl.pallas_call(
        _rmsnorm_kernel,
        grid=(t // tm,),
        in_specs=[pl.BlockSpec((tm, d), lambda i: (i, 0)), pl.BlockSpec((1, d), lambda i: (0, 0))],
        out_specs=pl.BlockSpec((tm, d), lambda i: (i, 0)),
        out_shape=jax.ShapeDtypeStruct((t, d), BF16),
        compiler_params=_cparams(("parallel",), 32),
        name="rmsnorm",
    )(x2d, gain.reshape(1, d).astype(F32))


def _proj_kernel(a_ref, w_ref, g_ref, o_ref, wb_ref, *, group_norm, scale):
    @pl.when(pl.program_id(1) == 0)
    def _():
        wb_ref[...] = w_ref[...].astype(BF16)

    acc = _dot(a_ref[...], wb_ref[...])
    if group_norm:
        for c in range(acc.shape[1] // DA_HEAD_DIM):
            blk = acc[:, c * DA_HEAD_DIM:(c + 1) * DA_HEAD_DIM]
            ms = jnp.mean(blk * blk, axis=-1, keepdims=True)
            y = blk * lax.rsqrt(ms + RMS_EPS) * g_ref[...] * scale
            o_ref[:, c * DA_HEAD_DIM:(c + 1) * DA_HEAD_DIM] = y.astype(o_ref.dtype)
    else:
        o_ref[...] = acc.astype(o_ref.dtype)


def _proj(a, w, layer, col0, n, *, out_dtype, tm, tn, gain=None, scale=1.0):
    t, k = a.shape
    assert col0 % tn == 0 and n % tn == 0
    cb = col0 // tn
    group_norm = gain is not None
    g = (gain if group_norm else jnp.ones((DA_HEAD_DIM,), F32)).reshape(1, DA_HEAD_DIM).astype(F32)
    return pl.pallas_call(
        functools.partial(_proj_kernel, group_norm=group_norm, scale=scale),
        grid=(n // tn, t // tm),
        in_specs=[
            pl.BlockSpec((tm, k), lambda j, i: (i, 0)),
            pl.BlockSpec((None, k, tn), lambda j, i: (layer, 0, cb + j)),
            pl.BlockSpec((1, DA_HEAD_DIM), lambda j, i: (0, 0)),
        ],
        out_specs=pl.BlockSpec((tm, tn), lambda j, i: (i, j)),
        out_shape=jax.ShapeDtypeStruct((t, n), out_dtype),
        scratch_shapes=[pltpu.VMEM((k, tn), BF16)],
        compiler_params=_cparams(("parallel", "arbitrary"), 48),
        name="proj",
    )(a, w.astype(F32), g)


def _proj_t_kernel(w_ref, a_ref, o_ref, wt_ref):
    @pl.when(pl.program_id(1) == 0)
    def _():
        wt_ref[...] = w_ref[...].T.astype(BF16)

    o_ref[0] = _dot_nt(wt_ref[...], a_ref[...]).astype(o_ref.dtype)


def _proj_t(a, w, layer, col0, n, *, tm, tn):
    t, k = a.shape
    assert col0 % tn == 0 and n % tn == 0
    cb = col0 // tn
    return pl.pallas_call(
        _proj_t_kernel,
        grid=(n // tn, t // tm),
        in_specs=[
            pl.BlockSpec((None, k, tn), lambda j, i: (layer, 0, cb + j)),
            pl.BlockSpec((tm, k), lambda j, i: (i, 0)),
        ],
        out_specs=pl.BlockSpec((1, tn, tm), lambda j, i: (i, j, 0)),
        out_shape=jax.ShapeDtypeStruct((t // tm, n, tm), BF16),
        scratch_shapes=[pltpu.VMEM((tn, k), BF16)],
        compiler_params=_cparams(("parallel", "arbitrary"), 48),
        name="proj_t",
    )(w.astype(F32), a)


def _da_kernel(slope_ref, q_ref, k_ref, vt_ref, lamv_ref, subg_ref, o_ref,
               kaug_ref, qaug_ref, vaug_ref, m_ref, acc_ref, *, tq, s_len, lambda_init):
    hp = pl.program_id(1)
    i = pl.program_id(2)
    hd = DA_HEAD_DIM
    dv = DA_V_DIM
    nq = vt_ref.shape[0]
    chains = [(hh, c) for hh in range(DA_HPS) for c in range(2)]
    slopes2 = [slope_ref[hp * DA_HPS + hh] * LOG2E for hh in range(DA_HPS)]

    @pl.when(i == 0)
    def _():
        pos = lax.broadcasted_iota(jnp.int32, (s_len, hd), 0)
        lane = lax.broadcasted_iota(jnp.int32, (s_len, hd), 1)
        piece = jnp.where(lane < 6, jnp.where((lane & 1) == 0, (pos >> 4) << 4, pos & 15), 0)
        piece = piece.astype(F32).astype(BF16)
        for hh, c in chains:
            col0 = hh * dv + c * hd
            kaug_ref[hh * 2 + c, :, :hd] = k_ref[:, col0:col0 + hd]
            kaug_ref[hh * 2 + c, :, hd:] = piece
        for hh in range(DA_HPS):
            for j in range(nq):
                vaug_ref[hh, j, :dv, :] = vt_ref[j, hh * dv:(hh + 1) * dv, :]
                vaug_ref[hh, j, dv:, :] = jnp.ones((vaug_ref.shape[2] - dv, tq), BF16)

    lane_q = lax.broadcasted_iota(jnp.int32, (tq, hd), 1)
    for hh in range(DA_HPS):
        s_full = jnp.full((tq, hd), slopes2[hh], F32)
        s1 = s_full.astype(BF16).astype(F32)
        r1 = s_full - s1
        s2 = r1.astype(BF16).astype(F32)
        s3 = r1 - s2
        slope_cols = jnp.where(lane_q < 2, s1, jnp.where(lane_q < 4, s2, jnp.where(lane_q < 6, s3, 0.0)))
        slope_cols = slope_cols.astype(BF16)
        for c in range(2):
            col0 = hh * dv + c * hd
            qaug_ref[hh * 2 + c, :, :hd] = q_ref[:, col0:col0 + hd]
            qaug_ref[hh * 2 + c, :, hd:] = slope_cols

    m_ref[...] = jnp.full(m_ref.shape, NEG_BIG, F32)
    acc_ref[...] = jnp.zeros(acc_ref.shape, F32)

    def update(ci, st, va):
        m_old = m_ref[ci]
        m_new = jnp.maximum(m_old, jnp.max(st, axis=0, keepdims=True))
        alpha = jnp.exp2(m_old - m_new)
        p = jnp.exp2(st - m_new)
        acc_ref[ci] = alpha * acc_ref[ci] + _dot(va, p.astype(BF16))
        m_ref[ci] = m_new

    def past_tile(j, carry):
        r0 = pl.multiple_of(j * tq, tq)
        for hh, c in chains:
            ci = hh * 2 + c
            update(ci, _dot_nt(kaug_ref[ci, pl.ds(r0, tq), :], qaug_ref[ci]), vaug_ref[hh, j])
        return carry

    lax.fori_loop(0, i, past_tile, 0)

    r0 = pl.multiple_of(i * tq, tq)
    krow = lax.broadcasted_iota(jnp.int32, (tq, tq), 0)
    qcol = lax.broadcasted_iota(jnp.int32, (tq, tq), 1)
    ahead = jnp.maximum(krow - qcol, 0).astype(F32)
    allowed = (krow >> CHUNK_SHIFT) <= (qcol >> CHUNK_SHIFT)
    for hh, c in chains:
        ci = hh * 2 + c
        st = _dot_nt(kaug_ref[ci, pl.ds(r0, tq), :], qaug_ref[ci])
        update(ci, jnp.where(allowed, st + (-2.0 * slopes2[hh]) * ahead, NEG_BIG), vaug_ref[hh, i])

    lamv = lamv_ref[...]
    lam = (jnp.exp(jnp.sum(lamv[0:1] * lamv[1:2], axis=-1, keepdims=True))
           - jnp.exp(jnp.sum(lamv[2:3] * lamv[3:4], axis=-1, keepdims=True)) + lambda_init)
    for hh in range(DA_HPS):
        a0 = acc_ref[hh * 2]
        a1 = acc_ref[hh * 2 + 1]
        ot = a0[:dv, :] / a0[dv:dv + 1, :] - lam * (a1[:dv, :] / a1[dv:dv + 1, :])
        ms = jnp.mean(ot * ot, axis=0, keepdims=True)
        o = (ot * lax.rsqrt(ms + RMS_EPS)).T
        o_ref[:, hh * dv:(hh + 1) * dv] = (o * subg_ref[...] * (1.0 - lambda_init)).astype(o_ref.dtype)


def _diff_attention(qn, kn, vt, lam_vecs, subln_g, bsz, s_len, lambda_init, tq):
    assert s_len <= 4096 and s_len % tq == 0 and tq % CHUNK == 0 and DA_HEADS % DA_HPS == 0
    t = bsz * s_len
    nq = s_len // tq
    w = DA_HPS * DA_V_DIM
    rows_aug = DA_V_DIM + DA_ONES_ROWS
    slopes = jnp.exp2(-8.0 * jnp.arange(1, DA_HEADS + 1, dtype=F32) / DA_HEADS)
    grid_spec = pltpu.PrefetchScalarGridSpec(
        num_scalar_prefetch=1,
        grid=(bsz, DA_HEADS // DA_HPS, nq),
        in_specs=[
            pl.BlockSpec((tq, w), lambda b, h, i, s: (b * nq + i, h)),
            pl.BlockSpec((s_len, w), lambda b, h, i, s: (b, h)),
            pl.BlockSpec((nq, w, tq), lambda b, h, i, s: (b, h, 0)),
            pl.BlockSpec((4, DA_HEAD_DIM), lambda b, h, i, s: (0, 0)),
            pl.BlockSpec((1, DA_V_DIM), lambda b, h, i, s: (0, 0)),
        ],
        out_specs=pl.BlockSpec((tq, w), lambda b, h, i, s: (b * nq + i, h)),
        scratch_shapes=[
            pltpu.VMEM((2 * DA_HPS, s_len, 2 * DA_HEAD_DIM), BF16),
            pltpu.VMEM((2 * DA_HPS, tq, 2 * DA_HEAD_DIM), BF16),
            pltpu.VMEM((DA_HPS, nq, rows_aug, tq), BF16),
            pltpu.VMEM((2 * DA_HPS, 1, tq), F32),
            pltpu.VMEM((2 * DA_HPS, rows_aug, tq), F32),
        ],
    )
    return pl.pallas_call(
        functools.partial(_da_kernel, tq=tq, s_len=s_len, lambda_init=lambda_init),
        grid_spec=grid_spec,
        out_shape=jax.ShapeDtypeStruct((t, DA_HEADS * DA_V_DIM), BF16),
        compiler_params=_cparams(("parallel", "parallel", "arbitrary"), 56),
        name="diff_attention",
    )(slopes, qn, kn, vt, lam_vecs.astype(F32), subln_g.reshape(1, DA_V_DIM).astype(F32))


def _gla_kernel(q_ref, k_ref, v_ref, g_ref, glr_ref, w2_ref, gb_ref, og_ref, o_ref, state_ref,
                *, tb, dk, dv):
    @pl.when(pl.program_id(1) == 0)
    def _():
        state_ref[...] = jnp.zeros(state_ref.shape, F32)

    row = lax.broadcasted_iota(jnp.int32, (CHUNK, CHUNK), 0)
    col = lax.broadcasted_iota(jnp.int32, (CHUNK, CHUNK), 1)
    lower = row >= col
    tri = lower.astype(F32)
    mid = CHUNK // 2

    def chunk(c, carry):
        r0 = pl.multiple_of(c * CHUNK, CHUNK)
        z = jnp.dot(glr_ref[pl.ds(r0, CHUNK), :], w2_ref[...], precision=HIGHEST,
                    preferred_element_type=F32) + gb_ref[...]
        log_a = (jnp.minimum(z, 0.0) - jnp.log1p(jnp.exp(-jnp.abs(z)))) * (1.0 / GLA_TAU)
        b_all = jnp.dot(tri, log_a, precision=HIGHEST, preferred_element_type=F32)
        for h in range(GLA_HEADS):
            ks = slice(h * dk, (h + 1) * dk)
            vs = slice(h * dv, (h + 1) * dv)
            b = b_all[:, ks]
            b_last = b[CHUNK - 1:CHUNK, :]
            b_mid = b[mid:mid + 1, :]
            q = q_ref[pl.ds(r0, CHUNK), ks].astype(F32) * (dk ** -0.5)
            k = k_ref[pl.ds(r0, CHUNK), ks].astype(F32)
            v = v_ref[pl.ds(r0, CHUNK), vs]
            e_fwd = jnp.exp(b - b_mid)
            e_bwd = jnp.exp(b_mid - b)
            a_lo = _dot_nt((q * e_fwd).astype(BF16), (k * e_bwd).astype(BF16))
            a_up = _dot_nt((q * e_bwd).astype(BF16), (k * e_fwd).astype(BF16))
            attn = jnp.where(lower, a_lo, a_up)
            state = state_ref[h]
            o = _dot(attn.astype(BF16), v) + _dot_nt((q * jnp.exp(b)).astype(BF16), state.astype(BF16))
            kd = (k * jnp.exp(b_last - b)).astype(BF16)
            state_ref[h] = state * jnp.exp(b_last) + _dot_tn(v, kd)
            ms = jnp.mean(o * o, axis=-1, keepdims=True)
            g = g_ref[pl.ds(r0, CHUNK), vs].astype(F32)
            y = o * lax.rsqrt(ms + RMS_EPS) * og_ref[...] * (g * jax.nn.sigmoid(g))
            o_ref[pl.ds(r0, CHUNK), vs] = y.astype(o_ref.dtype)
        return carry

    lax.fori_loop(0, tb // CHUNK, chunk, 0)


def _gla(src, cols, glr, w2p, gate_b, out_g, bsz, s_len, tb=512):
    t = bsz * s_len
    nb = s_len // tb
    kw = w2p.shape[1]
    dk = kw // GLA_HEADS
    dv = out_g.shape[0]
    vw = dv * GLA_HEADS
    cq, ck, cv, cg = cols
    return pl.pallas_call(
        functools.partial(_gla_kernel, tb=tb, dk=dk, dv=dv),
        grid=(bsz, nb),
        in_specs=[
            pl.BlockSpec((tb, kw), lambda b, i: (b * nb + i, cq // kw)),
            pl.BlockSpec((tb, kw), lambda b, i: (b * nb + i, ck // kw)),
            pl.BlockSpec((tb, vw), lambda b, i: (b * nb + i, cv // vw)),
            pl.BlockSpec((tb, vw), lambda b, i: (b * nb + i, cg // vw)),
            pl.BlockSpec((tb, LANES), lambda b, i: (b * nb + i, 0)),
            pl.BlockSpec((LANES, kw), lambda b, i: (0, 0)),
            pl.BlockSpec((1, kw), lambda b, i: (0, 0)),
            pl.BlockSpec((1, dv), lambda b, i: (0, 0)),
        ],
        out_specs=pl.BlockSpec((tb, vw), lambda b, i: (b * nb + i, 0)),
        out_shape=jax.ShapeDtypeStruct((t, vw), BF16),
        scratch_shapes=[pltpu.VMEM((GLA_HEADS, dv, dk), F32)],
        compiler_params=_cparams(("parallel", "arbitrary"), 48),
        name="gla",
    )(src, src, src, src, glr, w2p, gate_b.reshape(1, kw).astype(F32), out_g.reshape(1, dv).astype(F32))


def _merge_kernel(h_ref, ya_ref, yb_ref, wga_ref, wgb_ref, ba_ref, bb_ref, wa_ref, wb_ref, o_ref):
    h = h_ref[...]
    ga = jax.nn.sigmoid(_dot(h, wga_ref[...]) + ba_ref[...])
    gb = jax.nn.sigmoid(_dot(h, wgb_ref[...]) + bb_ref[...])
    mixed = ga * _dot(ya_ref[...], wa_ref[...]) + gb * _dot(yb_ref[...], wb_ref[...])
    o_ref[...] = mixed.astype(o_ref.dtype)


def _merge(h, y_da, y_gla, w_gate, b_gate, w_da, w_gla, tm=512, tn=512):
    t, d = h.shape
    nb = d // tn
    act = lambda: pl.BlockSpec((tm, d), lambda j, i: (i, 0))
    return pl.pallas_call(
        _merge_kernel,
        grid=(nb, t // tm),
        in_specs=[
            act(), act(), act(),
            pl.BlockSpec((d, tn), lambda j, i: (0, j)),
            pl.BlockSpec((d, tn), lambda j, i: (0, nb + j)),
            pl.BlockSpec((1, tn), lambda j, i: (0, j)),
            pl.BlockSpec((1, tn), lambda j, i: (0, nb + j)),
            pl.BlockSpec((d, tn), lambda j, i: (0, j)),
            pl.BlockSpec((d, tn), lambda j, i: (0, j)),
        ],
        out_specs=pl.BlockSpec((tm, tn), lambda j, i: (i, j)),
        out_shape=jax.ShapeDtypeStruct((t, d), BF16),
        compiler_params=_cparams(("parallel", "parallel"), 48),
        name="merge",
    )(h, y_da, y_gla, w_gate, w_gate, b_gate, b_gate, w_da, w_gla)


def _out_proj_kernel(x_ref, m_ref, w_ref, o_ref):
    o_ref[...] = x_ref[...] + _dot(m_ref[...], w_ref[...])


def _out_proj(x2d, mixed, w_out, tm=256):
    t, d = x2d.shape
    return pl.pallas_call(
        _out_proj_kernel,
        grid=(t // tm,),
        in_specs=[
            pl.BlockSpec((tm, d), lambda i: (i, 0)),
            pl.BlockSpec((tm, d), lambda i: (i, 0)),
            pl.BlockSpec((d, d), lambda i: (0, 0)),
        ],
        out_specs=pl.BlockSpec((tm, d), lambda i: (i, 0)),
        out_shape=jax.ShapeDtypeStruct((t, d), F32),
        compiler_params=_cparams(("parallel",), 48),
        name="out_proj",
    )(x2d, mixed, w_out)


ROW_SUBLANES = 8


def _pack_rows(val, store):
    half = val.shape[1] // 2
    assert half == ROW_SUBLANES * LANES
    lo = lax.bitcast_convert_type(val[:, :half].astype(BF16).astype(F32), jnp.uint32)
    hi = lax.bitcast_convert_type(val[:, half:].astype(BF16).astype(F32), jnp.uint32)
    packed = (lo >> 16) | (hi & jnp.uint32(0xFFFF0000))
    for c in range(ROW_SUBLANES):
        store(c, packed[:, c * LANES:(c + 1) * LANES])


def _unpack_chunk(chunk):
    lo = lax.bitcast_convert_type(chunk << 16, F32)
    hi = lax.bitcast_convert_type(chunk & jnp.uint32(0xFFFF0000), F32)
    return lo, hi


def _router_kernel(x_ref, g_ref, rw_ref, rb_ref, hp_ref, idx_ref, gate_ref):
    x = x_ref[...]
    tm = x.shape[0]
    ms = jnp.mean(x * x, axis=-1, keepdims=True)
    h = x * lax.rsqrt(ms + RMS_EPS) * g_ref[...]

    def store(c, chunk):
        hp_ref[pl.ds(c, tm, stride=ROW_SUBLANES), :] = chunk

    _pack_rows(h, store)

    logits = jnp.dot(h, rw_ref[...], precision=HIGHEST, preferred_element_type=F32) + rb_ref[...]
    lane = lax.broadcasted_iota(jnp.int32, logits.shape, 1)
    vals, idxs = [], []
    for _ in range(TOP_K):
        m = jnp.max(logits, axis=-1, keepdims=True)
        idx = jnp.min(jnp.where(logits == m, lane, LANES), axis=-1, keepdims=True)
        vals.append(m)
        idxs.append(idx)
        logits = jnp.where(lane == idx, -jnp.inf, logits)
    exps = [jnp.exp(v - vals[0]) for v in vals]
    denom = exps[0] + exps[1] + exps[2] + exps[3]
    idx_out = jnp.zeros(lane.shape, jnp.int32)
    gate_out = jnp.zeros(lane.shape, F32)
    for k in range(TOP_K):
        idx_out = jnp.where(lane == k, idxs[k], idx_out)
        gate_out = jnp.where(lane == k, exps[k] / denom, gate_out)
    idx_ref[...] = idx_out
    gate_ref[...] = gate_out


def _router(x2d, gain, router_w, router_b, tm=256):
    t, d = x2d.shape
    e = router_w.shape[1]
    rw = jnp.zeros((d, LANES), F32).at[:, :e].set(router_w.astype(F32))
    rb = jnp.full((1, LANES), NEG_BIG, F32).at[0, :e].set(router_b.astype(F32))
    return pl.pallas_call(
        _router_kernel,
        grid=(t // tm,),
        in_specs=[
            pl.BlockSpec((tm, d), lambda i: (i, 0)),
            pl.BlockSpec((1, d), lambda i: (0, 0)),
            pl.BlockSpec((d, LANES), lambda i: (0, 0)),
            pl.BlockSpec((1, LANES), lambda i: (0, 0)),
        ],
        out_specs=[
            pl.BlockSpec((tm * ROW_SUBLANES, LANES), lambda i: (i, 0)),
            pl.BlockSpec((tm, LANES), lambda i: (i, 0)),
            pl.BlockSpec((tm, LANES), lambda i: (i, 0)),
        ],
        out_shape=[
            jax.ShapeDtypeStruct((t * ROW_SUBLANES, LANES), jnp.uint32),
            jax.ShapeDtypeStruct((t, LANES), jnp.int32),
            jax.ShapeDtypeStruct((t, LANES), F32),
        ],
        compiler_params=_cparams(("parallel",), 32),
        name="router",
    )(x2d, gain.reshape(1, d).astype(F32), rw, rb)


MOE_SUB = 256
MOE_NSUB = 6
MOE_TF = 256
MOE_ISSUE_ROWS = 64


def _expert_kernel(ge_ref, gs_ref, gn_ref, ng_ref, tot_ref,
                   hp_ref, idx_ref, wgu_ref, bgu_ref, wdn_ref, bdn_ref, ys_ref,
                   idx_smem, xu_ref, yst_ref, xb_ref, acc_ref, wgu_b_ref, wdn_b_ref, zero_ref,
                   sem_idx, sem_in, sem_out, *, n_sub_total):
    g = pl.program_id(0)
    f = pl.program_id(1)
    nf = pl.num_programs(1)
    ng = ng_ref[0]
    active = g < ng
    nsub = gn_ref[g]
    start = gs_ref[g]
    half = xb_ref.shape[2] // 2
    tf = wdn_ref.shape[1]
    sub_words = MOE_SUB * ROW_SUBLANES
    sub_shift = MOE_SUB.bit_length() - 1
    assert MOE_SUB == 1 << sub_shift

    def fetch_idx(gg):
        idx_cp = pltpu.make_async_copy(idx_ref.at[pl.ds(gs_ref[gg], MOE_NSUB)], idx_smem, sem_idx)
        idx_cp.start()
        idx_cp.wait()

    blk_rows = min(MOE_ISSUE_ROWS, MOE_SUB)

    def issue_block(blk, n_rows_group):
        base = blk * blk_rows

        @pl.when(base < n_rows_group)
        def _():
            s = base >> sub_shift
            r0 = base & (MOE_SUB - 1)
            for u in range(blk_rows):
                r = r0 + u
                src = hp_ref.at[pl.ds(pl.multiple_of(idx_smem[s, 0, r] * ROW_SUBLANES, ROW_SUBLANES),
                                      ROW_SUBLANES)]
                dst = xu_ref.at[s, pl.ds(pl.multiple_of(r * ROW_SUBLANES, ROW_SUBLANES), ROW_SUBLANES)]
                pltpu.make_async_copy(src, dst, sem_in.at[s]).start(priority=u % 2)

    def issue_blocks(first, n_rows_group):
        def one(blk, c):
            issue_block(blk, n_rows_group)
            return c

        lax.fori_loop(first, lax.div(n_rows_group + blk_rows - 1, blk_rows), one, 0)

    def wait_rows(s):
        pltpu.make_async_copy(hp_ref.at[pl.ds(0, sub_words)], xu_ref.at[s], sem_in.at[s]).wait()

    def out_copy(s, first_sub):
        r0 = pl.multiple_of((first_sub + s) * sub_words, sub_words)
        return pltpu.make_async_copy(yst_ref.at[s], ys_ref.at[pl.ds(r0, sub_words)], sem_out)

    def wait_out(gg):
        def wait(s, c):
            out_copy(s, gs_ref[gg]).wait()
            return c

        lax.fori_loop(0, gn_ref[gg], wait, 0)

    @pl.when(jnp.logical_and(g == 0, f == 0))
    def _():
        zero_ref[...] = jnp.zeros(zero_ref.shape, zero_ref.dtype)
        fetch_idx(0)
        issue_blocks(0, gn_ref[0] * MOE_SUB)

    @pl.when(jnp.logical_and(active, f == 0))
    def _():
        def unpack(s, c):
            wait_rows(s)
            for ch in range(ROW_SUBLANES):
                lo, hi = _unpack_chunk(xu_ref[s, pl.ds(ch, MOE_SUB, stride=ROW_SUBLANES), :])
                xb_ref[s, :, ch * LANES:(ch + 1) * LANES] = lo.astype(BF16)
                xb_ref[s, :, half + ch * LANES:half + (ch + 1) * LANES] = hi.astype(BF16)
            acc_ref[s] = jnp.broadcast_to(bdn_ref[0], acc_ref.shape[1:])
            return c

        lax.fori_loop(0, nsub, unpack, 0)

        @pl.when(g + 1 < ng)
        def _():
            fetch_idx(g + 1)

    @pl.when(active)
    def _():
        nxt = jnp.minimum(g + 1, pl.num_programs(0) - 1)
        next_rows = jnp.where(g + 1 < ng, gn_ref[nxt] * MOE_SUB, 0)

        def issue_next(s):
            issue_block(f * nsub + s, next_rows)

        wgu_b_ref[...] = wgu_ref[0].astype(BF16)
        hl = LANES // 2
        for m in range(tf // LANES):
            first = wdn_ref[0, m * LANES:m * LANES + hl, :]
            second = wdn_ref[0, m * LANES + hl:(m + 1) * LANES, :]
            lo = lax.bitcast_convert_type(first.astype(BF16).astype(F32), jnp.uint32) >> 16
            hi = lax.bitcast_convert_type(second.astype(BF16).astype(F32), jnp.uint32) & jnp.uint32(0xFFFF0000)
            wdn_b_ref[m * LANES:(m + 1) * LANES, :] = pltpu.bitcast(lo | hi, BF16)
        bgu = bgu_ref[0]
        lane = lax.broadcasted_iota(jnp.int32, (MOE_SUB, LANES), 1)
        even = (lane & 1) == 0

        def gate_up(s):
            gu = _dot(xb_ref[s], wgu_b_ref[...]) + bgu
            gates, ups = [], []
            for m in range(tf // LANES):
                a = gu[:, 2 * m * LANES:(2 * m + 1) * LANES]
                b = gu[:, (2 * m + 1) * LANES:(2 * m + 2) * LANES]
                gates.append(jnp.where(even, a, pltpu.roll(b, 1, 1)))
                ups.append(jnp.where(even, pltpu.roll(a, LANES - 1, 1), b))
            gate = jnp.minimum(jnp.concatenate(gates, axis=1), SWIGLU_LIMIT)
            up = jnp.clip(jnp.concatenate(ups, axis=1), -SWIGLU_LIMIT, SWIGLU_LIMIT)
            return ((up + 1.0) * gate * jax.nn.sigmoid(SWIGLU_ALPHA * gate)).astype(BF16)

        def down(s, act):
            acc_ref[s] += _dot(act, wdn_b_ref[...])

        def body(s, act_prev):
            issue_next(s)
            act = gate_up(s)
            down(s - 1, act_prev)
            return act

        issue_next(0)
        act_last = lax.fori_loop(1, nsub, body, gate_up(0))
        down(nsub - 1, act_last)

        @pl.when(f == nf - 1)
        def _():
            issue_blocks(nf * nsub, next_rows)

    @pl.when(jnp.logical_and(active, f == nf - 1))
    def _():
        @pl.when(g > 0)
        def _():
            wait_out(g - 1)

        def issue(s, c):
            def store(ch, chunk):
                yst_ref[s, pl.ds(ch, MOE_SUB, stride=ROW_SUBLANES), :] = chunk

            _pack_rows(acc_ref[s], store)
            out_copy(s, start).start()
            return c

        lax.fori_loop(0, nsub, issue, 0)

    @pl.when(jnp.logical_and(g == pl.num_programs(0) - 1, f == nf - 1))
    def _():
        wait_out(ng - 1)

        def fill(s, c):
            r0 = pl.multiple_of(s * sub_words, sub_words)
            cp = pltpu.make_async_copy(zero_ref, ys_ref.at[pl.ds(r0, sub_words)], sem_out)
            cp.start()
            cp.wait()
            return c

        lax.fori_loop(tot_ref[0], n_sub_total, fill, 0)


def _experts(hp, row_tok, grp_expert, grp_start, grp_nsub, n_groups, tot_sub, w_gu, b_gu, w_dn, b_dn):
    n_rows = row_tok.shape[0]
    sub_words = MOE_SUB * ROW_SUBLANES
    idx2 = jnp.concatenate([row_tok.reshape(n_rows // MOE_SUB, 1, MOE_SUB),
                            jnp.zeros((MOE_NSUB, 1, MOE_SUB), jnp.int32)], axis=0)
    e, d, f2 = w_gu.shape
    fdim = f2 // 2
    tf = MOE_TF
    nf = fdim // tf
    n_grp = grp_expert.shape[0]

    def wsel(g, f, ge, gs, gn, ng, tot):
        on = g < ng[0]
        last = jnp.maximum(ng[0] - 1, 0)
        return jnp.where(on, ge[g], ge[last]), jnp.where(on, f, nf - 1)

    def w_gu_map(g, f, *pre):
        ee, ff = wsel(g, f, *pre)
        return (ee, 0, ff)

    def w_dn_map(g, f, *pre):
        ee, ff = wsel(g, f, *pre)
        return (ee, ff, 0)

    def b_dn_map(g, f, *pre):
        ee, _ = wsel(g, f, *pre)
        return (ee, 0, 0)

    grid_spec = pltpu.PrefetchScalarGridSpec(
        num_scalar_prefetch=5,
        grid=(n_grp, nf),
        in_specs=[
            pl.BlockSpec(memory_space=pl.ANY),
            pl.BlockSpec(memory_space=pl.ANY),
            pl.BlockSpec((1, d, 2 * tf), w_gu_map),
            pl.BlockSpec((1, 1, 2 * tf), w_gu_map),
            pl.BlockSpec((1, tf, d), w_dn_map),
            pl.BlockSpec((1, 1, d), b_dn_map),
        ],
        out_specs=pl.BlockSpec(memory_space=pl.ANY),
        scratch_shapes=[
            pltpu.SMEM((MOE_NSUB, 1, MOE_SUB), jnp.int32),
            pltpu.VMEM((MOE_NSUB, sub_words, LANES), jnp.uint32),
            pltpu.VMEM((MOE_NSUB, sub_words, LANES), jnp.uint32),
            pltpu.VMEM((MOE_NSUB, MOE_SUB, d), BF16),
            pltpu.VMEM((MOE_NSUB, MOE_SUB, d), F32),
            pltpu.VMEM((d, 2 * tf), BF16),
            pltpu.VMEM((tf, d), BF16),
            pltpu.VMEM((sub_words, LANES), jnp.uint32),
            pltpu.SemaphoreType.DMA(()),
            pltpu.SemaphoreType.DMA((MOE_NSUB,)),
            pltpu.SemaphoreType.DMA(()),
        ],
    )
    return pl.pallas_call(
        functools.partial(_expert_kernel, n_sub_total=n_rows // MOE_SUB),
        grid_spec=grid_spec,
        out_shape=jax.ShapeDtypeStruct((n_rows * ROW_SUBLANES, LANES), jnp.uint32),
        compiler_params=_cparams(("arbitrary", "arbitrary"), 56),
        name="moe_experts",
    )(grp_expert, grp_start, grp_nsub, n_groups, tot_sub, hp, idx2,
      w_gu, b_gu.reshape(e, 1, f2).astype(F32), w_dn, b_dn.reshape(e, 1, d).astype(F32))


def _combine_ple_kernel(pr_ref, prn_ref, gate_ref, x_ref, ys_ref, ys2_ref, p_ref, gn_ref, wg_ref, wu_ref, gp_ref,
                        o_ref, buf_ref, x2_ref, sem, *, tc):
    i = pl.program_id(0)
    slot = i & 1
    n_pairs = tc * TOP_K
    tile_words = tc * ROW_SUBLANES

    def issue(idx_ref, sl):
        def pair(p2, carry):
            for u in range(2):
                p = 2 * p2 + u
                tok = p >> 2
                pltpu.make_async_copy(
                    ys_ref.at[idx_ref[0, 0, p]],
                    buf_ref.at[sl, p & (TOP_K - 1),
                               pl.ds(pl.multiple_of(tok * ROW_SUBLANES, ROW_SUBLANES), ROW_SUBLANES)],
                    sem.at[sl]).start(priority=u)
            return carry

        lax.fori_loop(0, n_pairs // 2, pair, 0, unroll=4)

    @pl.when(i == 0)
    def _():
        issue(pr_ref, 0)

    @pl.when(i + 1 < pl.num_programs(0))
    def _():
        issue(prn_ref, 1 - slot)

    for k in range(TOP_K):
        pltpu.make_async_copy(ys2_ref.at[pl.ds(0, tile_words)], buf_ref.at[slot, k], sem.at[slot]).wait()

    gates = gate_ref[...]
    half = x_ref.shape[1] // 2
    for ch in range(ROW_SUBLANES):
        lo_cols = slice(ch * LANES, (ch + 1) * LANES)
        hi_cols = slice(half + ch * LANES, half + (ch + 1) * LANES)
        acc_lo = x_ref[:, lo_cols]
        acc_hi = x_ref[:, hi_cols]
        for k in range(TOP_K):
            lo, hi = _unpack_chunk(buf_ref[slot, k, pl.ds(ch, tc, stride=ROW_SUBLANES), :])
            acc_lo = acc_lo + gates[:, k:k + 1] * lo
            acc_hi = acc_hi + gates[:, k:k + 1] * hi
        x2_ref[:, lo_cols] = acc_lo
        x2_ref[:, hi_cols] = acc_hi

    x = x2_ref[...]
    ms = jnp.mean(x * x, axis=-1, keepdims=True)
    h = (x * lax.rsqrt(ms + RMS_EPS) * gn_ref[...]).astype(BF16)
    gate = jax.nn.sigmoid(_dot(h, wg_ref[...]))
    up = _dot(p_ref[...].astype(BF16), wu_ref[...])
    ms_u = jnp.mean(up * up, axis=-1, keepdims=True)
    o_ref[...] = x + up * lax.rsqrt(ms_u + RMS_EPS) * gp_ref[...] * gate


def _combine_ple(x2d, gates, pair_row, ys, p2d, g_norm, w_gate, w_up, g_post, tc=256):
    t, d = x2d.shape
    pd = p2d.shape[1]
    nt = t // tc
    n_rows = ys.shape[0] // ROW_SUBLANES
    pr = pair_row.reshape(nt, 1, tc * TOP_K)
    const = lambda i: (0, 0)
    return pl.pallas_call(
        functools.partial(_combine_ple_kernel, tc=tc),
        grid=(nt,),
        in_specs=[
            pl.BlockSpec((1, 1, tc * TOP_K), lambda i: (i, 0, 0), memory_space=pltpu.SMEM),
            pl.BlockSpec((1, 1, tc * TOP_K), lambda i: (jnp.minimum(i + 1, nt - 1), 0, 0),
                         memory_space=pltpu.SMEM),
            pl.BlockSpec((tc, LANES), lambda i: (i, 0)),
            pl.BlockSpec((tc, d), lambda i: (i, 0)),
            pl.BlockSpec(memory_space=pl.ANY),
            pl.BlockSpec(memory_space=pl.ANY),
            pl.BlockSpec((tc, pd), lambda i: (i, 0)),
            pl.BlockSpec((1, d), const),
            pl.BlockSpec((d, d), const),
            pl.BlockSpec((pd, d), const),
            pl.BlockSpec((1, d), const),
        ],
        out_specs=pl.BlockSpec((tc, d), lambda i: (i, 0)),
        out_shape=jax.ShapeDtypeStruct((t, d), F32),
        scratch_shapes=[
            pltpu.VMEM((2, TOP_K, tc * ROW_SUBLANES, LANES), jnp.uint32),
            pltpu.VMEM((tc, d), F32),
            pltpu.SemaphoreType.DMA((2,)),
        ],
        compiler_params=_cparams(("arbitrary",), 48),
        name="moe_combine_ple",
    )(pr, pr, gates, x2d, ys.reshape(n_rows, ROW_SUBLANES, LANES), ys, p2d,
      g_norm.reshape(1, d).astype(F32), w_gate, w_up, g_post.reshape(1, d).astype(F32))


def _moe(x2d, gain, router_w, router_b, w_gu, b_gu, w_dn, b_dn):
    t, d = x2d.shape
    e = router_w.shape[1]
    hp, idx, gates = _router(x2d, gain, router_w, router_b)

    n_pairs = t * TOP_K
    flat_e = idx[:, :TOP_K].reshape(n_pairs)
    onehot = (flat_e[:, None] == jnp.arange(e, dtype=jnp.int32)[None, :]).astype(jnp.int32)
    csum = jnp.cumsum(onehot, axis=0)
    rank = jnp.sum(onehot * csum, axis=1) - 1
    counts = csum[-1]
    sub_e = (counts + MOE_SUB - 1) // MOE_SUB
    sub_end = jnp.cumsum(sub_e)
    sub_start = sub_end - sub_e
    dest = (jnp.sum(onehot * (sub_start * MOE_SUB)[None, :], axis=1) + rank).astype(jnp.int32)
    n_sub_total = n_pairs // MOE_SUB + e
    n_rows = n_sub_total * MOE_SUB
    row_tok = jnp.zeros((n_rows,), jnp.int32).at[dest].set(jnp.arange(n_pairs, dtype=jnp.int32) // TOP_K)

    grp_e = (sub_e + MOE_NSUB - 1) // MOE_NSUB
    grp_end = jnp.cumsum(grp_e)
    n_grp_max = (n_sub_total + MOE_NSUB - 1) // MOE_NSUB + e
    gidx = jnp.arange(n_grp_max, dtype=jnp.int32)
    g_exp = jnp.minimum(jnp.searchsorted(grp_end, gidx, side="right"), e - 1).astype(jnp.int32)
    g_local = gidx - (grp_end - grp_e)[g_exp]
    g_on = gidx < grp_end[-1]
    g_start = jnp.where(g_on, sub_start[g_exp] + g_local * MOE_NSUB, 0).astype(jnp.int32)
    g_nsub = jnp.where(g_on, jnp.minimum(MOE_NSUB, sub_e[g_exp] - g_local * MOE_NSUB), 0).astype(jnp.int32)

    ys = _experts(hp, row_tok, g_exp, g_start, g_nsub,
                  grp_end[-1].astype(jnp.int32).reshape(1), sub_end[-1].astype(jnp.int32).reshape(1),
                  w_gu, b_gu, w_dn, b_dn)
    return gates, dest, ys


def kernel(x, p, w_in, da_q_norm, da_k_norm, da_lambda_q1, da_lambda_k1, da_lambda_q2, da_lambda_k2, da_subln, gla_gate_w2, gla_gate_b, gla_out_norm, w_branch_da, w_branch_gla, w_merge_gate, b_merge_gate, w_out, norm_mix, norm_ffn, router_w, router_b, w_gate_up, b_gate_up, w_down, b_down, norm_ple, w_ple_gate, w_ple_up, norm_ple_post):
    bsz, s_len, d = x.shape
    t = bsz * s_len
    depth = w_in.shape[0]
    qk_w = DA_HEADS * 2 * DA_HEAD_DIM
    v_w = DA_HEADS * DA_V_DIM
    gk_w = gla_gate_w2.shape[2]
    gv_w = w_branch_gla.shape[1]
    rest_w = 2 * gk_w + 2 * gv_w
    x2d = x.reshape(t, d)
    for i in range(depth):
        lambda_init = 0.8 - 0.6 * math.exp(-0.3 * i)
        h = _rmsnorm(x2d, norm_mix[i])
        w = w_in[i]
        c_v = 2 * qk_w
        c_rest = c_v + v_w
        qn = _proj(h, w_in, i, 0, qk_w, out_dtype=BF16, tm=512, tn=1024,
                   gain=da_q_norm[i], scale=DA_HEAD_DIM ** -0.5 * LOG2E)
        kn = _proj(h, w_in, i, qk_w, qk_w, out_dtype=BF16, tm=512, tn=1024, gain=da_k_norm[i])
        vt = _proj_t(h, w_in, i, c_v, v_w, tm=DA_TQ, tn=1024)
        rest = _proj(h, w_in, i, c_rest, rest_w, out_dtype=BF16, tm=512, tn=1024)
        w_lr = jnp.zeros((1, d, LANES), F32).at[0, :, :GLA_GATE_RANK].set(
            w_in[i, :, c_rest + rest_w:].astype(F32))
        glr = _proj(h, w_lr, 0, 0, LANES, out_dtype=F32, tm=512, tn=LANES)

        lam_vecs = jnp.stack([da_lambda_q1[i], da_lambda_k1[i], da_lambda_q2[i], da_lambda_k2[i]])
        y_da = _diff_attention(qn, kn, vt, lam_vecs, da_subln[i], bsz, s_len, lambda_init, DA_TQ)

        w2p = jnp.zeros((LANES, gk_w), F32).at[:GLA_GATE_RANK].set(gla_gate_w2[i].astype(F32))
        y_gla = _gla(rest, (0, gk_w, 2 * gk_w, 2 * gk_w + gv_w), glr, w2p,
                     gla_gate_b[i], gla_out_norm[i], bsz, s_len)

        mixed = _merge(h, y_da, y_gla, w_merge_gate[i].astype(BF16), b_merge_gate[i].reshape(1, 2 * d).astype(F32),
                       w_branch_da[i].astype(BF16), w_branch_gla[i].astype(BF16))
        x2d = _out_proj(x2d, mixed, w_out[i].astype(BF16))

        gates, pair_row, ys = _moe(x2d, norm_ffn[i], router_w[i], router_b[i], w_gate_up[i], b_gate_up[i],
                                   w_down[i], b_down[i])
        x2d = _combine_ple(x2d, gates, pair_row, ys, p[i].reshape(t, p.shape[-1]), norm_ple[i],
                           w_ple_gate[i].astype(BF16), w_ple_up[i].astype(BF16), norm_ple_post[i])
    return x2d.reshape(bsz, s_len, d)
```

```python
import functools
import math

import jax
import jax.numpy as jnp
from jax import lax
from jax.experimental import pallas as pl
from jax.experimental.pallas import tpu as pltpu

F32 = jnp.float32
BF16 = jnp.bfloat16
HIGHEST = lax.Precision.HIGHEST

CHUNK = 64
CHUNK_SHIFT = CHUNK.bit_length() - 1
RMS_EPS = 1e-6
DA_HEADS = 8
DA_HEAD_DIM = 128
DA_V_DIM = 2 * DA_HEAD_DIM
GLA_HEADS = 4
GLA_GATE_RANK = 16
GLA_TAU = 16.0
N_EXPERTS = 32
TOP_K = 4
SWIGLU_LIMIT = 7.0
SWIGLU_ALPHA = 1.702

DA_TQ = 1024
DA_HPS = 2
DA_ONES_ROWS = 16
LOG2E = 1.4426950408889634

LANES = 128
NEG_BIG = -1e30

MIB = 1024 * 1024


def _cparams(sem, vmem_mib, flags=None):
    return pltpu.CompilerParams(dimension_semantics=sem, vmem_limit_bytes=vmem_mib * MIB, flags=flags)


def _dot(a, b):
    return jnp.dot(a, b, preferred_element_type=F32)


def _dot_nt(a, b):
    return lax.dot_general(a, b, (((1,), (1,)), ((), ())), preferred_element_type=F32)


def _dot_tn(a, b):
    return lax.dot_general(a, b, (((0,), (0,)), ((), ())), preferred_element_type=F32)


def _rmsnorm_kernel(x_ref, g_ref, o_ref):
    x = x_ref[...]
    ms = jnp.mean(x * x, axis=-1, keepdims=True)
    o_ref[...] = (x * lax.rsqrt(ms + RMS_EPS) * g_ref[...]).astype(o_ref.dtype)


def _rmsnorm(x2d, gain, tm=512):
    t, d = x2d.shape
    return pl.pallas_call(
        _rmsnorm_kernel,
        grid=(t // tm,),
        in_specs=[pl.BlockSpec((tm, d), lambda i: (i, 0)), pl.BlockSpec((1, d), lambda i: (0, 0))],
        out_specs=pl.BlockSpec((tm, d), lambda i: (i, 0)),
        out_shape=jax.ShapeDtypeStruct((t, d), BF16),
        compiler_params=_cparams(("parallel",), 32),
        name="rmsnorm",
    )(x2d, gain.reshape(1, d).astype(F32))


def _proj_kernel(a_ref, w_ref, g_ref, o_ref, wb_ref, *, group_norm, scale):
    @pl.when(pl.program_id(1) == 0)
    def _():
        wb_ref[...] = w_ref[...].astype(BF16)

    acc = _dot(a_ref[...], wb_ref[...])
    if group_norm:
        for c in range(acc.shape[1] // DA_HEAD_DIM):
            blk = acc[:, c * DA_HEAD_DIM:(c + 1) * DA_HEAD_DIM]
            ms = jnp.mean(blk * blk, axis=-1, keepdims=True)
            y = blk * lax.rsqrt(ms + RMS_EPS) * g_ref[...] * scale
            o_ref[:, c * DA_HEAD_DIM:(c + 1) * DA_HEAD_DIM] = y.astype(o_ref.dtype)
    else:
        o_ref[...] = acc.astype(o_ref.dtype)


def _proj(a, w, layer, col0, n, *, out_dtype, tm, tn, gain=None, scale=1.0):
    t, k = a.shape
    assert col0 % tn == 0 and n % tn == 0
    cb = col0 // tn
    group_norm = gain is not None
    g = (gain if group_norm else jnp.ones((DA_HEAD_DIM,), F32)).reshape(1, DA_HEAD_DIM).astype(F32)
    return pl.pallas_call(
        functools.partial(_proj_kernel, group_norm=group_norm, scale=scale),
        grid=(n // tn, t // tm),
        in_specs=[
            pl.BlockSpec((tm, k), lambda j, i: (i, 0)),
            pl.BlockSpec((None, k, tn), lambda j, i: (layer, 0, cb + j)),
            pl.BlockSpec((1, DA_HEAD_DIM), lambda j, i: (0, 0)),
        ],
        out_specs=pl.BlockSpec((tm, tn), lambda j, i: (i, j)),
        out_shape=jax.ShapeDtypeStruct((t, n), out_dtype),
        scratch_shapes=[pltpu.VMEM((k, tn), BF16)],
        compiler_params=_cparams(("parallel", "arbitrary"), 48),
        name="proj",
    )(a, w.astype(F32), g)


def _proj_t_kernel(w_ref, a_ref, o_ref, wt_ref):
    @pl.when(pl.program_id(1) == 0)
    def _():
        wt_ref[...] = w_ref[...].T.astype(BF16)

    o_ref[0] = _dot_nt(wt_ref[...], a_ref[...]).astype(o_ref.dtype)


def _proj_t(a, w, layer, col0, n, *, tm, tn):
    t, k = a.shape
    assert col0 % tn == 0 and n % tn == 0
    cb = col0 // tn
    return pl.pallas_call(
        _proj_t_kernel,
        grid=(n // tn, t // tm),
        in_specs=[
            pl.BlockSpec((None, k, tn), lambda j, i: (layer, 0, cb + j)),
            pl.BlockSpec((tm, k), lambda j, i: (i, 0)),
        ],
        out_specs=pl.BlockSpec((1, tn, tm), lambda j, i: (i, j, 0)),
        out_shape=jax.ShapeDtypeStruct((t // tm, n, tm), BF16),
        scratch_shapes=[pltpu.VMEM((tn, k), BF16)],
        compiler_params=_cparams(("parallel", "arbitrary"), 48),
        name="proj_t",
    )(w.astype(F32), a)


def _da_kernel(slope_ref, q_ref, k_ref, vt_ref, lamv_ref, subg_ref, o_ref,
               kaug_ref, qaug_ref, vaug_ref, m_ref, acc_ref, *, tq, s_len, lambda_init):
    hp = pl.program_id(1)
    i = pl.program_id(2)
    hd = DA_HEAD_DIM
    dv = DA_V_DIM
    nq = vt_ref.shape[0]
    chains = [(hh, c) for hh in range(DA_HPS) for c in range(2)]
    slopes2 = [slope_ref[hp * DA_HPS + hh] * LOG2E for hh in range(DA_HPS)]

    @pl.when(i == 0)
    def _():
        pos = lax.broadcasted_iota(jnp.int32, (s_len, hd), 0)
        lane = lax.broadcasted_iota(jnp.int32, (s_len, hd), 1)
        piece = jnp.where(lane < 6, jnp.where((lane & 1) == 0, (pos >> 4) << 4, pos & 15), 0)
        piece = piece.astype(F32).astype(BF16)
        for hh, c in chains:
            col0 = hh * dv + c * hd
            kaug_ref[hh * 2 + c, :, :hd] = k_ref[:, col0:col0 + hd]
            kaug_ref[hh * 2 + c, :, hd:] = piece
        for hh in range(DA_HPS):
            for j in range(nq):
                vaug_ref[hh, j, :dv, :] = vt_ref[j, hh * dv:(hh + 1) * dv, :]
                vaug_ref[hh, j, dv:, :] = jnp.ones((vaug_ref.shape[2] - dv, tq), BF16)

    lane_q = lax.broadcasted_iota(jnp.int32, (tq, hd), 1)
    for hh in range(DA_HPS):
        s_full = jnp.full((tq, hd), slopes2[hh], F32)
        s1 = s_full.astype(BF16).astype(F32)
        r1 = s_full - s1
        s2 = r1.astype(BF16).astype(F32)
        s3 = r1 - s2
        slope_cols = jnp.where(lane_q < 2, s1, jnp.where(lane_q < 4, s2, jnp.where(lane_q < 6, s3, 0.0)))
        slope_cols = slope_cols.astype(BF16)
        for c in range(2):
            col0 = hh * dv + c * hd
            qaug_ref[hh * 2 + c, :, :hd] = q_ref[:, col0:col0 + hd]
            qaug_ref[hh * 2 + c, :, hd:] = slope_cols

    m_ref[...] = jnp.full(m_ref.shape, NEG_BIG, F32)
    acc_ref[...] = jnp.zeros(acc_ref.shape, F32)

    def update(ci, st, va, qs=slice(None)):
        m_old = m_ref[ci, :, qs]
        m_new = jnp.maximum(m_old, jnp.max(st, axis=0, keepdims=True))
        alpha = jnp.exp2(m_old - m_new)
        p = jnp.exp2(st - m_new)
        acc_ref[ci, :, qs] = alpha * acc_ref[ci, :, qs] + _dot(va, p.astype(BF16))
        m_ref[ci, :, qs] = m_new

    def past_tile(j, carry):
        r0 = pl.multiple_of(j * tq, tq)
        for hh, c in chains:
            ci = hh * 2 + c
            update(ci, _dot_nt(kaug_ref[ci, pl.ds(r0, tq), :], qaug_ref[ci]), vaug_ref[hh, j])
        return carry

    lax.fori_loop(0, i, past_tile, 0)

    hk = tq // 2
    r0 = pl.multiple_of(i * tq, tq)

    def diag_part(k0, q0):
        nq_part = tq - q0
        krow = k0 + lax.broadcasted_iota(jnp.int32, (hk, nq_part), 0)
        qcol = q0 + lax.broadcasted_iota(jnp.int32, (hk, nq_part), 1)
        ahead = jnp.maximum(krow - qcol, 0).astype(F32)
        allowed = (krow >> CHUNK_SHIFT) <= (qcol >> CHUNK_SHIFT)
        qs = slice(q0, tq)
        for hh, c in chains:
            ci = hh * 2 + c
            st = _dot_nt(kaug_ref[ci, pl.ds(pl.multiple_of(r0 + k0, hk), hk), :], qaug_ref[ci, qs, :])
            st = jnp.where(allowed, st + (-2.0 * slopes2[hh]) * ahead, NEG_BIG)
            update(ci, st, vaug_ref[hh, i, :, k0:k0 + hk], qs)

    diag_part(0, 0)
    diag_part(hk, hk)

    lamv = lamv_ref[...]
    lam = (jnp.exp(jnp.sum(lamv[0:1] * lamv[1:2], axis=-1, keepdims=True))
           - jnp.exp(jnp.sum(lamv[2:3] * lamv[3:4], axis=-1, keepdims=True)) + lambda_init)
    for hh in range(DA_HPS):
        a0 = acc_ref[hh * 2]
        a1 = acc_ref[hh * 2 + 1]
        ot = a0[:dv, :] / a0[dv:dv + 1, :] - lam * (a1[:dv, :] / a1[dv:dv + 1, :])
        ms = jnp.mean(ot * ot, axis=0, keepdims=True)
        o = (ot * lax.rsqrt(ms + RMS_EPS)).T
        o_ref[:, hh * dv:(hh + 1) * dv] = (o * subg_ref[...] * (1.0 - lambda_init)).astype(o_ref.dtype)


def _diff_attention(qn, kn, vt, lam_vecs, subln_g, bsz, s_len, lambda_init, tq):
    assert s_len <= 4096 and s_len % tq == 0 and tq % CHUNK == 0 and DA_HEADS % DA_HPS == 0
    t = bsz * s_len
    nq = s_len // tq
    w = DA_HPS * DA_V_DIM
    rows_aug = DA_V_DIM + DA_ONES_ROWS
    slopes = jnp.exp2(-8.0 * jnp.arange(1, DA_HEADS + 1, dtype=F32) / DA_HEADS)
    grid_spec = pltpu.PrefetchScalarGridSpec(
        num_scalar_prefetch=1,
        grid=(bsz, DA_HEADS // DA_HPS, nq),
        in_specs=[
            pl.BlockSpec((tq, w), lambda b, h, i, s: (b * nq + i, h)),
            pl.BlockSpec((s_len, w), lambda b, h, i, s: (b, h)),
            pl.BlockSpec((nq, w, tq), lambda b, h, i, s: (b, h, 0)),
            pl.BlockSpec((4, DA_HEAD_DIM), lambda b, h, i, s: (0, 0)),
            pl.BlockSpec((1, DA_V_DIM), lambda b, h, i, s: (0, 0)),
        ],
        out_specs=pl.BlockSpec((tq, w), lambda b, h, i, s: (b * nq + i, h)),
        scratch_shapes=[
            pltpu.VMEM((2 * DA_HPS, s_len, 2 * DA_HEAD_DIM), BF16),
            pltpu.VMEM((2 * DA_HPS, tq, 2 * DA_HEAD_DIM), BF16),
            pltpu.VMEM((DA_HPS, nq, rows_aug, tq), BF16),
            pltpu.VMEM((2 * DA_HPS, 1, tq), F32),
            pltpu.VMEM((2 * DA_HPS, rows_aug, tq), F32),
        ],
    )
    return pl.pallas_call(
        functools.partial(_da_kernel, tq=tq, s_len=s_len, lambda_init=lambda_init),
        grid_spec=grid_spec,
        out_shape=jax.ShapeDtypeStruct((t, DA_HEADS * DA_V_DIM), BF16),
        compiler_params=_cparams(("parallel", "parallel", "arbitrary"), 56),
        name="diff_attention",
    )(slopes, qn, kn, vt, lam_vecs.astype(F32), subln_g.reshape(1, DA_V_DIM).astype(F32))


def _gla_kernel(q_ref, k_ref, v_ref, g_ref, glr_ref, w2_ref, gb_ref, og_ref, o_ref, state_ref,
                *, tb, dk, dv):
    @pl.when(pl.program_id(1) == 0)
    def _():
        state_ref[...] = jnp.zeros(state_ref.shape, F32)

    row = lax.broadcasted_iota(jnp.int32, (CHUNK, CHUNK), 0)
    col = lax.broadcasted_iota(jnp.int32, (CHUNK, CHUNK), 1)
    lower = row >= col
    tri = lower.astype(F32)
    mid = CHUNK // 2

    def chunk(c, carry):
        r0 = pl.multiple_of(c * CHUNK, CHUNK)
        z = jnp.dot(glr_ref[pl.ds(r0, CHUNK), :], w2_ref[...], precision=HIGHEST,
                    preferred_element_type=F32) + gb_ref[...]
        log_a = (jnp.minimum(z, 0.0) - jnp.log1p(jnp.exp(-jnp.abs(z)))) * (1.0 / GLA_TAU)
        b_all = jnp.dot(tri, log_a, precision=HIGHEST, preferred_element_type=F32)
        for h in range(GLA_HEADS):
            ks = slice(h * dk, (h + 1) * dk)
            vs = slice(h * dv, (h + 1) * dv)
            b = b_all[:, ks]
            b_last = b[CHUNK - 1:CHUNK, :]
            b_mid = b[mid:mid + 1, :]
            q = q_ref[pl.ds(r0, CHUNK), ks].astype(F32) * (dk ** -0.5)
            k = k_ref[pl.ds(r0, CHUNK), ks].astype(F32)
            v = v_ref[pl.ds(r0, CHUNK), vs]
            e_fwd = jnp.exp(b - b_mid)
            e_bwd = jnp.exp(b_mid - b)
            a_lo = _dot_nt((q * e_fwd).astype(BF16), (k * e_bwd).astype(BF16))
            a_up = _dot_nt((q * e_bwd).astype(BF16), (k * e_fwd).astype(BF16))
            attn = jnp.where(lower, a_lo, a_up)
            state = state_ref[h]
            o = _dot(attn.astype(BF16), v) + _dot_nt((q * jnp.exp(b)).astype(BF16), state.astype(BF16))
            kd = (k * jnp.exp(b_last - b)).astype(BF16)
            state_ref[h] = state * jnp.exp(b_last) + _dot_tn(v, kd)
            ms = jnp.mean(o * o, axis=-1, keepdims=True)
            g = g_ref[pl.ds(r0, CHUNK), vs].astype(F32)
            y = o * lax.rsqrt(ms + RMS_EPS) * og_ref[...] * (g * jax.nn.sigmoid(g))
            o_ref[pl.ds(r0, CHUNK), vs] = y.astype(o_ref.dtype)
        return carry

    lax.fori_loop(0, tb // CHUNK, chunk, 0)


def _gla(src, cols, glr, w2p, gate_b, out_g, bsz, s_len, tb=512):
    t = bsz * s_len
    nb = s_len // tb
    kw = w2p.shape[1]
    dk = kw // GLA_HEADS
    dv = out_g.shape[0]
    vw = dv * GLA_HEADS
    cq, ck, cv, cg = cols
    return pl.pallas_call(
        functools.partial(_gla_kernel, tb=tb, dk=dk, dv=dv),
        grid=(bsz, nb),
        in_specs=[
            pl.BlockSpec((tb, kw), lambda b, i: (b * nb + i, cq // kw)),
            pl.BlockSpec((tb, kw), lambda b, i: (b * nb + i, ck // kw)),
            pl.BlockSpec((tb, vw), lambda b, i: (b * nb + i, cv // vw)),
            pl.BlockSpec((tb, vw), lambda b, i: (b * nb + i, cg // vw)),
            pl.BlockSpec((tb, LANES), lambda b, i: (b * nb + i, 0)),
            pl.BlockSpec((LANES, kw), lambda b, i: (0, 0)),
            pl.BlockSpec((1, kw), lambda b, i: (0, 0)),
            pl.BlockSpec((1, dv), lambda b, i: (0, 0)),
        ],
        out_specs=pl.BlockSpec((tb, vw), lambda b, i: (b * nb + i, 0)),
        out_shape=jax.ShapeDtypeStruct((t, vw), BF16),
        scratch_shapes=[pltpu.VMEM((GLA_HEADS, dv, dk), F32)],
        compiler_params=_cparams(("parallel", "arbitrary"), 48),
        name="gla",
    )(src, src, src, src, glr, w2p, gate_b.reshape(1, kw).astype(F32), out_g.reshape(1, dv).astype(F32))


def _merge_kernel(h_ref, ya_ref, yb_ref, wga_ref, wgb_ref, ba_ref, bb_ref, wa_ref, wb_ref, o_ref):
    h = h_ref[...]
    ga = jax.nn.sigmoid(_dot(h, wga_ref[...]) + ba_ref[...])
    gb = jax.nn.sigmoid(_dot(h, wgb_ref[...]) + bb_ref[...])
    mixed = ga * _dot(ya_ref[...], wa_ref[...]) + gb * _dot(yb_ref[...], wb_ref[...])
    o_ref[...] = mixed.astype(o_ref.dtype)


def _merge(h, y_da, y_gla, w_gate, b_gate, w_da, w_gla, tm=512, tn=512):
    t, d = h.shape
    nb = d // tn
    act = lambda: pl.BlockSpec((tm, d), lambda j, i: (i, 0))
    return pl.pallas_call(
        _merge_kernel,
        grid=(nb, t // tm),
        in_specs=[
            act(), act(), act(),
            pl.BlockSpec((d, tn), lambda j, i: (0, j)),
            pl.BlockSpec((d, tn), lambda j, i: (0, nb + j)),
            pl.BlockSpec((1, tn), lambda j, i: (0, j)),
            pl.BlockSpec((1, tn), lambda j, i: (0, nb + j)),
            pl.BlockSpec((d, tn), lambda j, i: (0, j)),
            pl.BlockSpec((d, tn), lambda j, i: (0, j)),
        ],
        out_specs=pl.BlockSpec((tm, tn), lambda j, i: (i, j)),
        out_shape=jax.ShapeDtypeStruct((t, d), BF16),
        compiler_params=_cparams(("parallel", "parallel"), 48),
        name="merge",
    )(h, y_da, y_gla, w_gate, w_gate, b_gate, b_gate, w_da, w_gla)


def _out_proj_kernel(x_ref, m_ref, w_ref, o_ref):
    o_ref[...] = x_ref[...] + _dot(m_ref[...], w_ref[...])


def _out_proj(x2d, mixed, w_out, tm=256):
    t, d = x2d.shape
    return pl.pallas_call(
        _out_proj_kernel,
        grid=(t // tm,),
        in_specs=[
            pl.BlockSpec((tm, d), lambda i: (i, 0)),
            pl.BlockSpec((tm, d), lambda i: (i, 0)),
            pl.BlockSpec((d, d), lambda i: (0, 0)),
        ],
        out_specs=pl.BlockSpec((tm, d), lambda i: (i, 0)),
        out_shape=jax.ShapeDtypeStruct((t, d), F32),
        compiler_params=_cparams(("parallel",), 48),
        name="out_proj",
    )(x2d, mixed, w_out)


ROW_SUBLANES = 8


def _pack_rows(val, store):
    half = val.shape[1] // 2
    assert half == ROW_SUBLANES * LANES
    lo = lax.bitcast_convert_type(val[:, :half].astype(BF16).astype(F32), jnp.uint32)
    hi = lax.bitcast_convert_type(val[:, half:].astype(BF16).astype(F32), jnp.uint32)
    packed = (lo >> 16) | (hi & jnp.uint32(0xFFFF0000))
    for c in range(ROW_SUBLANES):
        store(c, packed[:, c * LANES:(c + 1) * LANES])


def _unpack_chunk(chunk):
    lo = lax.bitcast_convert_type(chunk << 16, F32)
    hi = lax.bitcast_convert_type(chunk & jnp.uint32(0xFFFF0000), F32)
    return lo, hi


def _router_kernel(x_ref, g_ref, rw_ref, rb_ref, hp_ref, idx_ref, gate_ref):
    x = x_ref[...]
    tm = x.shape[0]
    ms = jnp.mean(x * x, axis=-1, keepdims=True)
    h = x * lax.rsqrt(ms + RMS_EPS) * g_ref[...]

    def store(c, chunk):
        hp_ref[pl.ds(c, tm, stride=ROW_SUBLANES), :] = chunk

    _pack_rows(h, store)

    logits = jnp.dot(h, rw_ref[...], precision=HIGHEST, preferred_element_type=F32) + rb_ref[...]
    lane = lax.broadcasted_iota(jnp.int32, logits.shape, 1)
    vals, idxs = [], []
    for _ in range(TOP_K):
        m = jnp.max(logits, axis=-1, keepdims=True)
        idx = jnp.min(jnp.where(logits == m, lane, LANES), axis=-1, keepdims=True)
        vals.append(m)
        idxs.append(idx)
        logits = jnp.where(lane == idx, -jnp.inf, logits)
    exps = [jnp.exp(v - vals[0]) for v in vals]
    denom = exps[0] + exps[1] + exps[2] + exps[3]
    idx_out = jnp.zeros(lane.shape, jnp.int32)
    gate_out = jnp.zeros(lane.shape, F32)
    for k in range(TOP_K):
        idx_out = jnp.where(lane == k, idxs[k], idx_out)
        gate_out = jnp.where(lane == k, exps[k] / denom, gate_out)
    idx_ref[...] = idx_out
    gate_ref[...] = gate_out


def _router(x2d, gain, router_w, router_b, tm=256):
    t, d = x2d.shape
    e = router_w.shape[1]
    rw = jnp.zeros((d, LANES), F32).at[:, :e].set(router_w.astype(F32))
    rb = jnp.full((1, LANES), NEG_BIG, F32).at[0, :e].set(router_b.astype(F32))
    return pl.pallas_call(
        _router_kernel,
        grid=(t // tm,),
        in_specs=[
            pl.BlockSpec((tm, d), lambda i: (i, 0)),
            pl.BlockSpec((1, d), lambda i: (0, 0)),
            pl.BlockSpec((d, LANES), lambda i: (0, 0)),
            pl.BlockSpec((1, LANES), lambda i: (0, 0)),
        ],
        out_specs=[
            pl.BlockSpec((tm * ROW_SUBLANES, LANES), lambda i: (i, 0)),
            pl.BlockSpec((tm, LANES), lambda i: (i, 0)),
            pl.BlockSpec((tm, LANES), lambda i: (i, 0)),
        ],
        out_shape=[
            jax.ShapeDtypeStruct((t * ROW_SUBLANES, LANES), jnp.uint32),
            jax.ShapeDtypeStruct((t, LANES), jnp.int32),
            jax.ShapeDtypeStruct((t, LANES), F32),
        ],
        compiler_params=_cparams(("parallel",), 32),
        name="router",
    )(x2d, gain.reshape(1, d).astype(F32), rw, rb)


MOE_SUB = 256
MOE_NSUB = 6
MOE_TF = 256
MOE_ISSUE_ROWS = 64


def _expert_kernel(ge_ref, gs_ref, gn_ref, ng_ref, tot_ref,
                   hp_ref, idx_ref, wgu_ref, bgu_ref, wdn_ref, bdn_ref, ys_ref,
                   idx_smem, xu_ref, yst_ref, xb_ref, acc_ref, wgu_b_ref, wdn_b_ref, zero_ref,
                   sem_idx, sem_in, sem_out, *, n_sub_total):
    g = pl.program_id(0)
    f = pl.program_id(1)
    nf = pl.num_programs(1)
    ng = ng_ref[0]
    active = g < ng
    nsub = gn_ref[g]
    start = gs_ref[g]
    half = xb_ref.shape[2] // 2
    tf = wdn_ref.shape[1]
    sub_words = MOE_SUB * ROW_SUBLANES
    sub_shift = MOE_SUB.bit_length() - 1
    assert MOE_SUB == 1 << sub_shift

    def fetch_idx(gg):
        idx_cp = pltpu.make_async_copy(idx_ref.at[pl.ds(gs_ref[gg], MOE_NSUB)], idx_smem, sem_idx)
        idx_cp.start()
        idx_cp.wait()

    blk_rows = min(MOE_ISSUE_ROWS, MOE_SUB)

    def issue_block(blk, n_rows_group):
        base = blk * blk_rows

        @pl.when(base < n_rows_group)
        def _():
            s = base >> sub_shift
            r0 = base & (MOE_SUB - 1)
            for u in range(blk_rows):
                r = r0 + u
                src = hp_ref.at[pl.ds(pl.multiple_of(idx_smem[s, 0, r] * ROW_SUBLANES, ROW_SUBLANES),
                                      ROW_SUBLANES)]
                dst = xu_ref.at[s, pl.ds(pl.multiple_of(r * ROW_SUBLANES, ROW_SUBLANES), ROW_SUBLANES)]
                pltpu.make_async_copy(src, dst, sem_in.at[s]).start(priority=u % 2)

    def issue_blocks(first, n_rows_group):
        def one(blk, c):
            issue_block(blk, n_rows_group)
            return c

        lax.fori_loop(first, lax.div(n_rows_group + blk_rows - 1, blk_rows), one, 0)

    def wait_rows(s):
        pltpu.make_async_copy(hp_ref.at[pl.ds(0, sub_words)], xu_ref.at[s], sem_in.at[s]).wait()

    def out_copy(s, first_sub):
        r0 = pl.multiple_of((first_sub + s) * sub_words, sub_words)
        return pltpu.make_async_copy(yst_ref.at[s], ys_ref.at[pl.ds(r0, sub_words)], sem_out)

    def wait_out(gg):
        def wait(s, c):
            out_copy(s, gs_ref[gg]).wait()
            return c

        lax.fori_loop(0, gn_ref[gg], wait, 0)

    @pl.when(jnp.logical_and(g == 0, f == 0))
    def _():
        zero_ref[...] = jnp.zeros(zero_ref.shape, zero_ref.dtype)
        fetch_idx(0)
        issue_blocks(0, gn_ref[0] * MOE_SUB)

    @pl.when(jnp.logical_and(active, f == 0))
    def _():
        def unpack(s, c):
            wait_rows(s)
            for ch in range(ROW_SUBLANES):
                lo, hi = _unpack_chunk(xu_ref[s, pl.ds(ch, MOE_SUB, stride=ROW_SUBLANES), :])
                xb_ref[s, :, ch * LANES:(ch + 1) * LANES] = lo.astype(BF16)
                xb_ref[s, :, half + ch * LANES:half + (ch + 1) * LANES] = hi.astype(BF16)
            acc_ref[s] = jnp.broadcast_to(bdn_ref[0], acc_ref.shape[1:])
            return c

        lax.fori_loop(0, nsub, unpack, 0)

        @pl.when(g + 1 < ng)
        def _():
            fetch_idx(g + 1)

    @pl.when(active)
    def _():
        nxt = jnp.minimum(g + 1, pl.num_programs(0) - 1)
        next_rows = jnp.where(g + 1 < ng, gn_ref[nxt] * MOE_SUB, 0)

        def issue_next(s):
            issue_block(f * nsub + s, next_rows)

        wgu_b_ref[...] = wgu_ref[0].astype(BF16)
        hl = LANES // 2
        for m in range(tf // LANES):
            first = wdn_ref[0, m * LANES:m * LANES + hl, :]
            second = wdn_ref[0, m * LANES + hl:(m + 1) * LANES, :]
            lo = lax.bitcast_convert_type(first.astype(BF16).astype(F32), jnp.uint32) >> 16
            hi = lax.bitcast_convert_type(second.astype(BF16).astype(F32), jnp.uint32) & jnp.uint32(0xFFFF0000)
            wdn_b_ref[m * LANES:(m + 1) * LANES, :] = pltpu.bitcast(lo | hi, BF16)
        bgu = bgu_ref[0]
        lane = lax.broadcasted_iota(jnp.int32, (MOE_SUB, LANES), 1)
        even = (lane & 1) == 0

        def gate_up(s):
            gu = _dot(xb_ref[s], wgu_b_ref[...]) + bgu
            gates, ups = [], []
            for m in range(tf // LANES):
                a = gu[:, 2 * m * LANES:(2 * m + 1) * LANES]
                b = gu[:, (2 * m + 1) * LANES:(2 * m + 2) * LANES]
                gates.append(jnp.where(even, a, pltpu.roll(b, 1, 1)))
                ups.append(jnp.where(even, pltpu.roll(a, LANES - 1, 1), b))
            gate = jnp.minimum(jnp.concatenate(gates, axis=1), SWIGLU_LIMIT)
            up = jnp.clip(jnp.concatenate(ups, axis=1), -SWIGLU_LIMIT, SWIGLU_LIMIT)
            return ((up + 1.0) * gate * jax.nn.sigmoid(SWIGLU_ALPHA * gate)).astype(BF16)

        def down(s, act):
            acc_ref[s] += _dot(act, wdn_b_ref[...])

        def body(s, act_prev):
            issue_next(s)
            act = gate_up(s)
            down(s - 1, act_prev)
            return act

        issue_next(0)
        act_last = lax.fori_loop(1, nsub, body, gate_up(0))
        down(nsub - 1, act_last)

        @pl.when(f == nf - 1)
        def _():
            issue_blocks(nf * nsub, next_rows)

    @pl.when(jnp.logical_and(active, f == nf - 1))
    def _():
        @pl.when(g > 0)
        def _():
            wait_out(g - 1)

        def issue(s, c):
            def store(ch, chunk):
                yst_ref[s, pl.ds(ch, MOE_SUB, stride=ROW_SUBLANES), :] = chunk

            _pack_rows(acc_ref[s], store)
            out_copy(s, start).start()
            return c

        lax.fori_loop(0, nsub, issue, 0)

    @pl.when(jnp.logical_and(g == pl.num_programs(0) - 1, f == nf - 1))
    def _():
        wait_out(ng - 1)

        def fill(s, c):
            r0 = pl.multiple_of(s * sub_words, sub_words)
            cp = pltpu.make_async_copy(zero_ref, ys_ref.at[pl.ds(r0, sub_words)], sem_out)
            cp.start()
            cp.wait()
            return c

        lax.fori_loop(tot_ref[0], n_sub_total, fill, 0)


def _experts(hp, row_tok, grp_expert, grp_start, grp_nsub, n_groups, tot_sub, w_gu, b_gu, w_dn, b_dn):
    n_rows = row_tok.shape[0]
    sub_words = MOE_SUB * ROW_SUBLANES
    idx2 = jnp.concatenate([row_tok.reshape(n_rows // MOE_SUB, 1, MOE_SUB),
                            jnp.zeros((MOE_NSUB, 1, MOE_SUB), jnp.int32)], axis=0)
    e, d, f2 = w_gu.shape
    fdim = f2 // 2
    tf = MOE_TF
    nf = fdim // tf
    n_grp = grp_expert.shape[0]

    def wsel(g, f, ge, gs, gn, ng, tot):
        on = g < ng[0]
        last = jnp.maximum(ng[0] - 1, 0)
        return jnp.where(on, ge[g], ge[last]), jnp.where(on, f, nf - 1)

    def w_gu_map(g, f, *pre):
        ee, ff = wsel(g, f, *pre)
        return (ee, 0, ff)

    def w_dn_map(g, f, *pre):
        ee, ff = wsel(g, f, *pre)
        return (ee, ff, 0)

    def b_dn_map(g, f, *pre):
        ee, _ = wsel(g, f, *pre)
        return (ee, 0, 0)

    grid_spec = pltpu.PrefetchScalarGridSpec(
        num_scalar_prefetch=5,
        grid=(n_grp, nf),
        in_specs=[
            pl.BlockSpec(memory_space=pl.ANY),
            pl.BlockSpec(memory_space=pl.ANY),
            pl.BlockSpec((1, d, 2 * tf), w_gu_map),
            pl.BlockSpec((1, 1, 2 * tf), w_gu_map),
            pl.BlockSpec((1, tf, d), w_dn_map),
            pl.BlockSpec((1, 1, d), b_dn_map),
        ],
        out_specs=pl.BlockSpec(memory_space=pl.ANY),
        scratch_shapes=[
            pltpu.SMEM((MOE_NSUB, 1, MOE_SUB), jnp.int32),
            pltpu.VMEM((MOE_NSUB, sub_words, LANES), jnp.uint32),
            pltpu.VMEM((MOE_NSUB, sub_words, LANES), jnp.uint32),
            pltpu.VMEM((MOE_NSUB, MOE_SUB, d), BF16),
            pltpu.VMEM((MOE_NSUB, MOE_SUB, d), F32),
            pltpu.VMEM((d, 2 * tf), BF16),
            pltpu.VMEM((tf, d), BF16),
            pltpu.VMEM((sub_words, LANES), jnp.uint32),
            pltpu.SemaphoreType.DMA(()),
            pltpu.SemaphoreType.DMA((MOE_NSUB,)),
            pltpu.SemaphoreType.DMA(()),
        ],
    )
    return pl.pallas_call(
        functools.partial(_expert_kernel, n_sub_total=n_rows // MOE_SUB),
        grid_spec=grid_spec,
        out_shape=jax.ShapeDtypeStruct((n_rows * ROW_SUBLANES, LANES), jnp.uint32),
        compiler_params=_cparams(("arbitrary", "arbitrary"), 56),
        name="moe_experts",
    )(grp_expert, grp_start, grp_nsub, n_groups, tot_sub, hp, idx2,
      w_gu, b_gu.reshape(e, 1, f2).astype(F32), w_dn, b_dn.reshape(e, 1, d).astype(F32))


def _combine_ple_kernel(pr_ref, prn_ref, gate_ref, x_ref, ys_ref, ys2_ref, p_ref, gn_ref, wg_ref, wu_ref, gp_ref,
                        o_ref, buf_ref, x2_ref, sem, *, tc):
    i = pl.program_id(0)
    slot = i & 1
    n_pairs = tc * TOP_K
    tile_words = tc * ROW_SUBLANES

    def issue(idx_ref, sl):
        def pair(p2, carry):
            for u in range(2):
                p = 2 * p2 + u
                tok = p >> 2
                pltpu.make_async_copy(
                    ys_ref.at[idx_ref[0, 0, p]],
                    buf_ref.at[sl, p & (TOP_K - 1),
                               pl.ds(pl.multiple_of(tok * ROW_SUBLANES, ROW_SUBLANES), ROW_SUBLANES)],
                    sem.at[sl]).start(priority=u)
            return carry

        lax.fori_loop(0, n_pairs // 2, pair, 0, unroll=4)

    @pl.when(i == 0)
    def _():
        issue(pr_ref, 0)

    @pl.when(i + 1 < pl.num_programs(0))
    def _():
        issue(prn_ref, 1 - slot)

    for k in range(TOP_K):
        pltpu.make_async_copy(ys2_ref.at[pl.ds(0, tile_words)], buf_ref.at[slot, k], sem.at[slot]).wait()

    gates = gate_ref[...]
    half = x_ref.shape[1] // 2
    for ch in range(ROW_SUBLANES):
        lo_cols = slice(ch * LANES, (ch + 1) * LANES)
        hi_cols = slice(half + ch * LANES, half + (ch + 1) * LANES)
        acc_lo = x_ref[:, lo_cols]
        acc_hi = x_ref[:, hi_cols]
        for k in range(TOP_K):
            lo, hi = _unpack_chunk(buf_ref[slot, k, pl.ds(ch, tc, stride=ROW_SUBLANES), :])
            acc_lo = acc_lo + gates[:, k:k + 1] * lo
            acc_hi = acc_hi + gates[:, k:k + 1] * hi
        x2_ref[:, lo_cols] = acc_lo
        x2_ref[:, hi_cols] = acc_hi

    x = x2_ref[...]
    ms = jnp.mean(x * x, axis=-1, keepdims=True)
    h = (x * lax.rsqrt(ms + RMS_EPS) * gn_ref[...]).astype(BF16)
    gate = jax.nn.sigmoid(_dot(h, wg_ref[...]))
    up = _dot(p_ref[...].astype(BF16), wu_ref[...])
    ms_u = jnp.mean(up * up, axis=-1, keepdims=True)
    o_ref[...] = x + up * lax.rsqrt(ms_u + RMS_EPS) * gp_ref[...] * gate


def _combine_ple(x2d, gates, pair_row, ys, p2d, g_norm, w_gate, w_up, g_post, tc=256):
    t, d = x2d.shape
    pd = p2d.shape[1]
    nt = t // tc
    n_rows = ys.shape[0] // ROW_SUBLANES
    pr = pair_row.reshape(nt, 1, tc * TOP_K)
    const = lambda i: (0, 0)
    return pl.pallas_call(
        functools.partial(_combine_ple_kernel, tc=tc),
        grid=(nt,),
        in_specs=[
            pl.BlockSpec((1, 1, tc * TOP_K), lambda i: (i, 0, 0), memory_space=pltpu.SMEM),
            pl.BlockSpec((1, 1, tc * TOP_K), lambda i: (jnp.minimum(i + 1, nt - 1), 0, 0),
                         memory_space=pltpu.SMEM),
            pl.BlockSpec((tc, LANES), lambda i: (i, 0)),
            pl.BlockSpec((tc, d), lambda i: (i, 0)),
            pl.BlockSpec(memory_space=pl.ANY),
            pl.BlockSpec(memory_space=pl.ANY),
            pl.BlockSpec((tc, pd), lambda i: (i, 0)),
            pl.BlockSpec((1, d), const),
            pl.BlockSpec((d, d), const),
            pl.BlockSpec((pd, d), const),
            pl.BlockSpec((1, d), const),
        ],
        out_specs=pl.BlockSpec((tc, d), lambda i: (i, 0)),
        out_shape=jax.ShapeDtypeStruct((t, d), F32),
        scratch_shapes=[
            pltpu.VMEM((2, TOP_K, tc * ROW_SUBLANES, LANES), jnp.uint32),
            pltpu.VMEM((tc, d), F32),
            pltpu.SemaphoreType.DMA((2,)),
        ],
        compiler_params=_cparams(("arbitrary",), 48),
        name="moe_combine_ple",
    )(pr, pr, gates, x2d, ys.reshape(n_rows, ROW_SUBLANES, LANES), ys, p2d,
      g_norm.reshape(1, d).astype(F32), w_gate, w_up, g_post.reshape(1, d).astype(F32))


def _moe(x2d, gain, router_w, router_b, w_gu, b_gu, w_dn, b_dn):
    t, d = x2d.shape
    e = router_w.shape[1]
    hp, idx, gates = _router(x2d, gain, router_w, router_b)

    n_pairs = t * TOP_K
    flat_e = idx[:, :TOP_K].reshape(n_pairs)
    onehot = (flat_e[:, None] == jnp.arange(e, dtype=jnp.int32)[None, :]).astype(jnp.int32)
    csum = jnp.cumsum(onehot, axis=0)
    rank = jnp.sum(onehot * csum, axis=1) - 1
    counts = csum[-1]
    sub_e = (counts + MOE_SUB - 1) // MOE_SUB
    sub_end = jnp.cumsum(sub_e)
    sub_start = sub_end - sub_e
    dest = (jnp.sum(onehot * (sub_start * MOE_SUB)[None, :], axis=1) + rank).astype(jnp.int32)
    n_sub_total = n_pairs // MOE_SUB + e
    n_rows = n_sub_total * MOE_SUB
    row_tok = jnp.zeros((n_rows,), jnp.int32).at[dest].set(jnp.arange(n_pairs, dtype=jnp.int32) // TOP_K)

    grp_e = (sub_e + MOE_NSUB - 1) // MOE_NSUB
    grp_end = jnp.cumsum(grp_e)
    n_grp_max = (n_sub_total + MOE_NSUB - 1) // MOE_NSUB + e
    gidx = jnp.arange(n_grp_max, dtype=jnp.int32)
    g_exp = jnp.minimum(jnp.searchsorted(grp_end, gidx, side="right"), e - 1).astype(jnp.int32)
    g_local = gidx - (grp_end - grp_e)[g_exp]
    g_on = gidx < grp_end[-1]
    g_start = jnp.where(g_on, sub_start[g_exp] + g_local * MOE_NSUB, 0).astype(jnp.int32)
    g_nsub = jnp.where(g_on, jnp.minimum(MOE_NSUB, sub_e[g_exp] - g_local * MOE_NSUB), 0).astype(jnp.int32)

    ys = _experts(hp, row_tok, g_exp, g_start, g_nsub,
                  grp_end[-1].astype(jnp.int32).reshape(1), sub_end[-1].astype(jnp.int32).reshape(1),
                  w_gu, b_gu, w_dn, b_dn)
    return gates, dest, ys


def kernel(x, p, w_in, da_q_norm, da_k_norm, da_lambda_q1, da_lambda_k1, da_lambda_q2, da_lambda_k2, da_subln, gla_gate_w2, gla_gate_b, gla_out_norm, w_branch_da, w_branch_gla, w_merge_gate, b_merge_gate, w_out, norm_mix, norm_ffn, router_w, router_b, w_gate_up, b_gate_up, w_down, b_down, norm_ple, w_ple_gate, w_ple_up, norm_ple_post):
    bsz, s_len, d = x.shape
    t = bsz * s_len
    depth = w_in.shape[0]
    qk_w = DA_HEADS * 2 * DA_HEAD_DIM
    v_w = DA_HEADS * DA_V_DIM
    gk_w = gla_gate_w2.shape[2]
    gv_w = w_branch_gla.shape[1]
    rest_w = 2 * gk_w + 2 * gv_w
    x2d = x.reshape(t, d)
    for i in range(depth):
        lambda_init = 0.8 - 0.6 * math.exp(-0.3 * i)
        h = _rmsnorm(x2d, norm_mix[i])
        w = w_in[i]
        c_v = 2 * qk_w
        c_rest = c_v + v_w
        qn = _proj(h, w_in, i, 0, qk_w, out_dtype=BF16, tm=1024, tn=1024,
                   gain=da_q_norm[i], scale=DA_HEAD_DIM ** -0.5 * LOG2E)
        kn = _proj(h, w_in, i, qk_w, qk_w, out_dtype=BF16, tm=1024, tn=1024, gain=da_k_norm[i])
        vt = _proj_t(h, w_in, i, c_v, v_w, tm=DA_TQ, tn=1024)
        rest = _proj(h, w_in, i, c_rest, rest_w, out_dtype=BF16, tm=1024, tn=1024)
        w_lr = jnp.zeros((1, d, LANES), F32).at[0, :, :GLA_GATE_RANK].set(
            w_in[i, :, c_rest + rest_w:].astype(F32))
        glr = _proj(h, w_lr, 0, 0, LANES, out_dtype=F32, tm=512, tn=LANES)

        lam_vecs = jnp.stack([da_lambda_q1[i], da_lambda_k1[i], da_lambda_q2[i], da_lambda_k2[i]])
        y_da = _diff_attention(qn, kn, vt, lam_vecs, da_subln[i], bsz, s_len, lambda_init, DA_TQ)

        w2p = jnp.zeros((LANES, gk_w), F32).at[:GLA_GATE_RANK].set(gla_gate_w2[i].astype(F32))
        y_gla = _gla(rest, (0, gk_w, 2 * gk_w, 2 * gk_w + gv_w), glr, w2p,
                     gla_gate_b[i], gla_out_norm[i], bsz, s_len)

        mixed = _merge(h, y_da, y_gla, w_merge_gate[i].astype(BF16), b_merge_gate[i].reshape(1, 2 * d).astype(F32),
                       w_branch_da[i].astype(BF16), w_branch_gla[i].astype(BF16))
        x2d = _out_proj(x2d, mixed, w_out[i].astype(BF16))

        gates, pair_row, ys = _moe(x2d, norm_ffn[i], router_w[i], router_b[i], w_gate_up[i], b_gate_up[i],
                                   w_down[i], b_down[i])
        x2d = _combine_ple(x2d, gates, pair_row, ys, p[i].reshape(t, p.shape[-1]), norm_ple[i],
                           w_ple_gate[i].astype(BF16), w_ple_up[i].astype(BF16), norm_ple_post[i])
    return x2d.reshape(bsz, s_len, d)
```

```python
import functools
import math

import jax
import jax.numpy as jnp
from jax import lax
from jax.experimental import pallas as pl
from jax.experimental.pallas import tpu as pltpu

F32 = jnp.float32
BF16 = jnp.bfloat16
HIGHEST = lax.Precision.HIGHEST

CHUNK = 64
CHUNK_SHIFT = CHUNK.bit_length() - 1
RMS_EPS = 1e-6
DA_HEADS = 8
DA_HEAD_DIM = 128
DA_V_DIM = 2 * DA_HEAD_DIM
GLA_HEADS = 4
GLA_GATE_RANK = 16
GLA_TAU = 16.0
N_EXPERTS = 32
TOP_K = 4
SWIGLU_LIMIT = 7.0
SWIGLU_ALPHA = 1.702

DA_TQ = 1024
DA_HPS = 2
DA_ONES_ROWS = 16
LOG2E = 1.4426950408889634

LANES = 128
NEG_BIG = -1e30

MIB = 1024 * 1024


def _cparams(sem, vmem_mib, flags=None):
    return pltpu.CompilerParams(dimension_semantics=sem, vmem_limit_bytes=vmem_mib * MIB, flags=flags)


def _dot(a, b):
    return jnp.dot(a, b, preferred_element_type=F32)


def _dot_nt(a, b):
    return lax.dot_general(a, b, (((1,), (1,)), ((), ())), preferred_element_type=F32)


def _dot_tn(a, b):
    return lax.dot_general(a, b, (((0,), (0,)), ((), ())), preferred_element_type=F32)


def _rmsnorm_kernel(x_ref, g_ref, o_ref):
    x = x_ref[...]
    ms = jnp.mean(x * x, axis=-1, keepdims=True)
    o_ref[...] = (x * lax.rsqrt(ms + RMS_EPS) * g_ref[...]).astype(o_ref.dtype)


def _rmsnorm(x2d, gain, tm=512):
    t, d = x2d.shape
    return pl.pallas_call(
        _rmsnorm_kernel,
        grid=(t // tm,),
        in_specs=[pl.BlockSpec((tm, d), lambda i: (i, 0)), pl.BlockSpec((1, d), lambda i: (0, 0))],
        out_specs=pl.BlockSpec((tm, d), lambda i: (i, 0)),
        out_shape=jax.ShapeDtypeStruct((t, d), BF16),
        compiler_params=_cparams(("parallel",), 32),
        name="rmsnorm",
    )(x2d, gain.reshape(1, d).astype(F32))


def _proj_kernel(a_ref, w_ref, g_ref, o_ref, wb_ref, *, group_norm, scale):
    @pl.when(pl.program_id(1) == 0)
    def _():
        wb_ref[...] = w_ref[...].astype(BF16)

    acc = _dot(a_ref[...], wb_ref[...])
    if group_norm:
        for c in range(acc.shape[1] // DA_HEAD_DIM):
            blk = acc[:, c * DA_HEAD_DIM:(c + 1) * DA_HEAD_DIM]
            ms = jnp.mean(blk * blk, axis=-1, keepdims=True)
            y = blk * lax.rsqrt(ms + RMS_EPS) * g_ref[...] * scale
            o_ref[:, c * DA_HEAD_DIM:(c + 1) * DA_HEAD_DIM] = y.astype(o_ref.dtype)
    else:
        o_ref[...] = acc.astype(o_ref.dtype)


def _proj(a, w, layer, col0, n, *, out_dtype, tm, tn, gain=None, scale=1.0):
    t, k = a.shape
    assert col0 % tn == 0 and n % tn == 0
    cb = col0 // tn
    group_norm = gain is not None
    g = (gain if group_norm else jnp.ones((DA_HEAD_DIM,), F32)).reshape(1, DA_HEAD_DIM).astype(F32)
    return pl.pallas_call(
        functools.partial(_proj_kernel, group_norm=group_norm, scale=scale),
        grid=(n // tn, t // tm),
        in_specs=[
            pl.BlockSpec((tm, k), lambda j, i: (i, 0)),
            pl.BlockSpec((None, k, tn), lambda j, i: (layer, 0, cb + j)),
            pl.BlockSpec((1, DA_HEAD_DIM), lambda j, i: (0, 0)),
        ],
        out_specs=pl.BlockSpec((tm, tn), lambda j, i: (i, j)),
        out_shape=jax.ShapeDtypeStruct((t, n), out_dtype),
        scratch_shapes=[pltpu.VMEM((k, tn), BF16)],
        compiler_params=_cparams(("parallel", "arbitrary"), 48),
        name="proj",
    )(a, w.astype(F32), g)


def _proj_t_kernel(w_ref, a_ref, o_ref, wt_ref):
    @pl.when(pl.program_id(1) == 0)
    def _():
        wt_ref[...] = w_ref[...].T.astype(BF16)

    o_ref[0] = _dot_nt(wt_ref[...], a_ref[...]).astype(o_ref.dtype)


def _proj_t(a, w, layer, col0, n, *, tm, tn):
    t, k = a.shape
    assert col0 % tn == 0 and n % tn == 0
    cb = col0 // tn
    return pl.pallas_call(
        _proj_t_kernel,
        grid=(n // tn, t // tm),
        in_specs=[
            pl.BlockSpec((None, k, tn), lambda j, i: (layer, 0, cb + j)),
            pl.BlockSpec((tm, k), lambda j, i: (i, 0)),
        ],
        out_specs=pl.BlockSpec((1, tn, tm), lambda j, i: (i, j, 0)),
        out_shape=jax.ShapeDtypeStruct((t // tm, n, tm), BF16),
        scratch_shapes=[pltpu.VMEM((tn, k), BF16)],
        compiler_params=_cparams(("parallel", "arbitrary"), 48),
        name="proj_t",
    )(w.astype(F32), a)


def _da_kernel(slope_ref, q_ref, k_ref, vt_ref, lamv_ref, subg_ref, o_ref,
               kaug_ref, qaug_ref, vaug_ref, m_ref, acc_ref, *, tq, s_len, lambda_init):
    hp = pl.program_id(1)
    i = pl.program_id(2)
    hd = DA_HEAD_DIM
    dv = DA_V_DIM
    nq = vt_ref.shape[0]
    chains = [(hh, c) for hh in range(DA_HPS) for c in range(2)]
    slopes2 = [slope_ref[hp * DA_HPS + hh] * LOG2E for hh in range(DA_HPS)]

    @pl.when(i == 0)
    def _():
        pos = lax.broadcasted_iota(jnp.int32, (s_len, hd), 0)
        lane = lax.broadcasted_iota(jnp.int32, (s_len, hd), 1)
        piece = jnp.where(lane < 6, jnp.where((lane & 1) == 0, (pos >> 4) << 4, pos & 15), 0)
        piece = piece.astype(F32).astype(BF16)
        for hh, c in chains:
            col0 = hh * dv + c * hd
            kaug_ref[hh * 2 + c, :, :hd] = k_ref[:, col0:col0 + hd]
            kaug_ref[hh * 2 + c, :, hd:] = piece
        for hh in range(DA_HPS):
            for j in range(nq):
                vaug_ref[hh, j, :dv, :] = vt_ref[j, hh * dv:(hh + 1) * dv, :]
                vaug_ref[hh, j, dv:, :] = jnp.ones((vaug_ref.shape[2] - dv, tq), BF16)

    lane_q = lax.broadcasted_iota(jnp.int32, (tq, hd), 1)
    for hh in range(DA_HPS):
        s_full = jnp.full((tq, hd), slopes2[hh], F32)
        s1 = s_full.astype(BF16).astype(F32)
        r1 = s_full - s1
        s2 = r1.astype(BF16).astype(F32)
        s3 = r1 - s2
        slope_cols = jnp.where(lane_q < 2, s1, jnp.where(lane_q < 4, s2, jnp.where(lane_q < 6, s3, 0.0)))
        slope_cols = slope_cols.astype(BF16)
        for c in range(2):
            col0 = hh * dv + c * hd
            qaug_ref[hh * 2 + c, :, :hd] = q_ref[:, col0:col0 + hd]
            qaug_ref[hh * 2 + c, :, hd:] = slope_cols

    m_ref[...] = jnp.full(m_ref.shape, NEG_BIG, F32)
    acc_ref[...] = jnp.zeros(acc_ref.shape, F32)

    def update(ci, st, va, qs=slice(None)):
        m_old = m_ref[ci, :, qs]
        m_new = jnp.maximum(m_old, jnp.max(st, axis=0, keepdims=True))
        alpha = jnp.exp2(m_old - m_new)
        p = jnp.exp2(st - m_new)
        acc_ref[ci, :, qs] = alpha * acc_ref[ci, :, qs] + _dot(va, p.astype(BF16))
        m_ref[ci, :, qs] = m_new

    def past_tile(j, carry):
        r0 = pl.multiple_of(j * tq, tq)
        for hh, c in chains:
            ci = hh * 2 + c
            update(ci, _dot_nt(kaug_ref[ci, pl.ds(r0, tq), :], qaug_ref[ci]), vaug_ref[hh, j])
        return carry

    lax.fori_loop(0, i, past_tile, 0)

    hk = tq // 2
    r0 = pl.multiple_of(i * tq, tq)

    def diag_part(k0, q0):
        nq_part = tq - q0
        krow = k0 + lax.broadcasted_iota(jnp.int32, (hk, nq_part), 0)
        qcol = q0 + lax.broadcasted_iota(jnp.int32, (hk, nq_part), 1)
        ahead = jnp.maximum(krow - qcol, 0).astype(F32)
        allowed = (krow >> CHUNK_SHIFT) <= (qcol >> CHUNK_SHIFT)
        qs = slice(q0, tq)
        for hh, c in chains:
            ci = hh * 2 + c
            st = _dot_nt(kaug_ref[ci, pl.ds(pl.multiple_of(r0 + k0, hk), hk), :], qaug_ref[ci, qs, :])
            st = jnp.where(allowed, st + (-2.0 * slopes2[hh]) * ahead, NEG_BIG)
            update(ci, st, vaug_ref[hh, i, :, k0:k0 + hk], qs)

    diag_part(0, 0)
    diag_part(hk, hk)

    lamv = lamv_ref[...]
    lam = (jnp.exp(jnp.sum(lamv[0:1] * lamv[1:2], axis=-1, keepdims=True))
           - jnp.exp(jnp.sum(lamv[2:3] * lamv[3:4], axis=-1, keepdims=True)) + lambda_init)
    for hh in range(DA_HPS):
        a0 = acc_ref[hh * 2]
        a1 = acc_ref[hh * 2 + 1]
        ot = a0[:dv, :] / a0[dv:dv + 1, :] - lam * (a1[:dv, :] / a1[dv:dv + 1, :])
        ms = jnp.mean(ot * ot, axis=0, keepdims=True)
        o = (ot * lax.rsqrt(ms + RMS_EPS)).T
        o_ref[:, hh * dv:(hh + 1) * dv] = (o * subg_ref[...] * (1.0 - lambda_init)).astype(o_ref.dtype)


def _diff_attention(qn, kn, vt, lam_vecs, subln_g, bsz, s_len, lambda_init, tq):
    assert s_len <= 4096 and s_len % tq == 0 and tq % CHUNK == 0 and DA_HEADS % DA_HPS == 0
    t = bsz * s_len
    nq = s_len // tq
    w = DA_HPS * DA_V_DIM
    rows_aug = DA_V_DIM + DA_ONES_ROWS
    slopes = jnp.exp2(-8.0 * jnp.arange(1, DA_HEADS + 1, dtype=F32) / DA_HEADS)
    grid_spec = pltpu.PrefetchScalarGridSpec(
        num_scalar_prefetch=1,
        grid=(bsz, DA_HEADS // DA_HPS, nq),
        in_specs=[
            pl.BlockSpec((tq, w), lambda b, h, i, s: (b * nq + i, h)),
            pl.BlockSpec((s_len, w), lambda b, h, i, s: (b, h)),
            pl.BlockSpec((nq, w, tq), lambda b, h, i, s: (b, h, 0)),
            pl.BlockSpec((4, DA_HEAD_DIM), lambda b, h, i, s: (0, 0)),
            pl.BlockSpec((1, DA_V_DIM), lambda b, h, i, s: (0, 0)),
        ],
        out_specs=pl.BlockSpec((tq, w), lambda b, h, i, s: (b * nq + i, h)),
        scratch_shapes=[
            pltpu.VMEM((2 * DA_HPS, s_len, 2 * DA_HEAD_DIM), BF16),
            pltpu.VMEM((2 * DA_HPS, tq, 2 * DA_HEAD_DIM), BF16),
            pltpu.VMEM((DA_HPS, nq, rows_aug, tq), BF16),
            pltpu.VMEM((2 * DA_HPS, 1, tq), F32),
            pltpu.VMEM((2 * DA_HPS, rows_aug, tq), F32),
        ],
    )
    return pl.pallas_call(
        functools.partial(_da_kernel, tq=tq, s_len=s_len, lambda_init=lambda_init),
        grid_spec=grid_spec,
        out_shape=jax.ShapeDtypeStruct((t, DA_HEADS * DA_V_DIM), BF16),
        compiler_params=_cparams(("parallel", "parallel", "arbitrary"), 56),
        name="diff_attention",
    )(slopes, qn, kn, vt, lam_vecs.astype(F32), subln_g.reshape(1, DA_V_DIM).astype(F32))


def _gla_kernel(q_ref, k_ref, v_ref, g_ref, glr_ref, w2_ref, gb_ref, og_ref, o_ref, state_ref,
                *, tb, dk, dv):
    @pl.when(pl.program_id(0) == 0)
    def _():
        state_ref[...] = jnp.zeros(state_ref.shape, F32)

    bsz = q_ref.shape[0]
    row = lax.broadcasted_iota(jnp.int32, (CHUNK, CHUNK), 0)
    col = lax.broadcasted_iota(jnp.int32, (CHUNK, CHUNK), 1)
    lower = row >= col
    tri = lower.astype(F32)
    mid = CHUNK // 2

    def chunk(c, carry):
        r0 = pl.multiple_of(c * CHUNK, CHUNK)
        rows = pl.ds(r0, CHUNK)
        for bi in range(bsz):
            z = jnp.dot(glr_ref[bi, rows, :], w2_ref[...], precision=HIGHEST,
                        preferred_element_type=F32) + gb_ref[...]
            log_a = (jnp.minimum(z, 0.0) - jnp.log1p(jnp.exp(-jnp.abs(z)))) * (1.0 / GLA_TAU)
            b_all = jnp.dot(tri, log_a, precision=HIGHEST, preferred_element_type=F32)
            for h in range(GLA_HEADS):
                ks = slice(h * dk, (h + 1) * dk)
                vs = slice(h * dv, (h + 1) * dv)
                b = b_all[:, ks]
                b_last = b[CHUNK - 1:CHUNK, :]
                b_mid = b[mid:mid + 1, :]
                q = q_ref[bi, rows, ks].astype(F32) * (dk ** -0.5)
                k = k_ref[bi, rows, ks].astype(F32)
                v = v_ref[bi, rows, vs]
                e_fwd = jnp.exp(b - b_mid)
                e_bwd = jnp.exp(b_mid - b)
                a_lo = _dot_nt((q * e_fwd).astype(BF16), (k * e_bwd).astype(BF16))
                a_up = _dot_nt((q * e_bwd).astype(BF16), (k * e_fwd).astype(BF16))
                attn = jnp.where(lower, a_lo, a_up)
                state = state_ref[bi, h]
                o = _dot(attn.astype(BF16), v) + _dot_nt((q * jnp.exp(b)).astype(BF16), state.astype(BF16))
                kd = (k * jnp.exp(b_last - b)).astype(BF16)
                state_ref[bi, h] = state * jnp.exp(b_last) + _dot_tn(v, kd)
                ms = jnp.mean(o * o, axis=-1, keepdims=True)
                g = g_ref[bi, rows, vs].astype(F32)
                y = o * lax.rsqrt(ms + RMS_EPS) * og_ref[...] * (g * jax.nn.sigmoid(g))
                o_ref[bi, rows, vs] = y.astype(o_ref.dtype)
        return carry

    lax.fori_loop(0, tb // CHUNK, chunk, 0)


def _gla(src, cols, glr, w2p, gate_b, out_g, bsz, s_len, tb=512):
    t = bsz * s_len
    nb = s_len // tb
    kw = w2p.shape[1]
    dk = kw // GLA_HEADS
    dv = out_g.shape[0]
    vw = dv * GLA_HEADS
    cq, ck, cv, cg = cols
    src3 = src.reshape(bsz, s_len, src.shape[1])
    out = pl.pallas_call(
        functools.partial(_gla_kernel, tb=tb, dk=dk, dv=dv),
        grid=(nb,),
        in_specs=[
            pl.BlockSpec((bsz, tb, kw), lambda i: (0, i, cq // kw)),
            pl.BlockSpec((bsz, tb, kw), lambda i: (0, i, ck // kw)),
            pl.BlockSpec((bsz, tb, vw), lambda i: (0, i, cv // vw)),
            pl.BlockSpec((bsz, tb, vw), lambda i: (0, i, cg // vw)),
            pl.BlockSpec((bsz, tb, LANES), lambda i: (0, i, 0)),
            pl.BlockSpec((LANES, kw), lambda i: (0, 0)),
            pl.BlockSpec((1, kw), lambda i: (0, 0)),
            pl.BlockSpec((1, dv), lambda i: (0, 0)),
        ],
        out_specs=pl.BlockSpec((bsz, tb, vw), lambda i: (0, i, 0)),
        out_shape=jax.ShapeDtypeStruct((bsz, s_len, vw), BF16),
        scratch_shapes=[pltpu.VMEM((bsz, GLA_HEADS, dv, dk), F32)],
        compiler_params=_cparams(("arbitrary",), 56),
        name="gla",
    )(src3, src3, src3, src3, glr.reshape(bsz, s_len, LANES), w2p,
      gate_b.reshape(1, kw).astype(F32), out_g.reshape(1, dv).astype(F32))
    return out.reshape(t, vw)


def _merge_kernel(h_ref, ya_ref, yb_ref, wga_ref, wgb_ref, ba_ref, bb_ref, wa_ref, wb_ref, o_ref):
    h = h_ref[...]
    ga = jax.nn.sigmoid(_dot(h, wga_ref[...]) + ba_ref[...])
    gb = jax.nn.sigmoid(_dot(h, wgb_ref[...]) + bb_ref[...])
    mixed = ga * _dot(ya_ref[...], wa_ref[...]) + gb * _dot(yb_ref[...], wb_ref[...])
    o_ref[...] = mixed.astype(o_ref.dtype)


def _merge(h, y_da, y_gla, w_gate, b_gate, w_da, w_gla, tm=512, tn=512):
    t, d = h.shape
    nb = d // tn
    act = lambda: pl.BlockSpec((tm, d), lambda j, i: (i, 0))
    return pl.pallas_call(
        _merge_kernel,
        grid=(nb, t // tm),
        in_specs=[
            act(), act(), act(),
            pl.BlockSpec((d, tn), lambda j, i: (0, j)),
            pl.BlockSpec((d, tn), lambda j, i: (0, nb + j)),
            pl.BlockSpec((1, tn), lambda j, i: (0, j)),
            pl.BlockSpec((1, tn), lambda j, i: (0, nb + j)),
            pl.BlockSpec((d, tn), lambda j, i: (0, j)),
            pl.BlockSpec((d, tn), lambda j, i: (0, j)),
        ],
        out_specs=pl.BlockSpec((tm, tn), lambda j, i: (i, j)),
        out_shape=jax.ShapeDtypeStruct((t, d), BF16),
        compiler_params=_cparams(("parallel", "parallel"), 48),
        name="merge",
    )(h, y_da, y_gla, w_gate, w_gate, b_gate, b_gate, w_da, w_gla)


def _out_proj_kernel(x_ref, m_ref, w_ref, o_ref):
    o_ref[...] = x_ref[...] + _dot(m_ref[...], w_ref[...])


def _out_proj(x2d, mixed, w_out, tm=256):
    t, d = x2d.shape
    return pl.pallas_call(
        _out_proj_kernel,
        grid=(t // tm,),
        in_specs=[
            pl.BlockSpec((tm, d), lambda i: (i, 0)),
            pl.BlockSpec((tm, d), lambda i: (i, 0)),
            pl.BlockSpec((d, d), lambda i: (0, 0)),
        ],
        out_specs=pl.BlockSpec((tm, d), lambda i: (i, 0)),
        out_shape=jax.ShapeDtypeStruct((t, d), F32),
        compiler_params=_cparams(("parallel",), 48),
        name="out_proj",
    )(x2d, mixed, w_out)


ROW_SUBLANES = 8


def _pack_rows(val, store):
    half = val.shape[1] // 2
    assert half == ROW_SUBLANES * LANES
    lo = lax.bitcast_convert_type(val[:, :half].astype(BF16).astype(F32), jnp.uint32)
    hi = lax.bitcast_convert_type(val[:, half:].astype(BF16).astype(F32), jnp.uint32)
    packed = (lo >> 16) | (hi & jnp.uint32(0xFFFF0000))
    for c in range(ROW_SUBLANES):
        store(c, packed[:, c * LANES:(c + 1) * LANES])


def _unpack_chunk(chunk):
    lo = lax.bitcast_convert_type(chunk << 16, F32)
    hi = lax.bitcast_convert_type(chunk & jnp.uint32(0xFFFF0000), F32)
    return lo, hi


def _router_kernel(x_ref, g_ref, rw_ref, rb_ref, hp_ref, idx_ref, gate_ref):
    x = x_ref[...]
    tm = x.shape[0]
    ms = jnp.mean(x * x, axis=-1, keepdims=True)
    h = x * lax.rsqrt(ms + RMS_EPS) * g_ref[...]

    def store(c, chunk):
        hp_ref[pl.ds(c, tm, stride=ROW_SUBLANES), :] = chunk

    _pack_rows(h, store)

    logits = jnp.dot(h, rw_ref[...], precision=HIGHEST, preferred_element_type=F32) + rb_ref[...]
    lane = lax.broadcasted_iota(jnp.int32, logits.shape, 1)
    vals, idxs = [], []
    for _ in range(TOP_K):
        m = jnp.max(logits, axis=-1, keepdims=True)
        idx = jnp.min(jnp.where(logits == m, lane, LANES), axis=-1, keepdims=True)
        vals.append(m)
        idxs.append(idx)
        logits = jnp.where(lane == idx, -jnp.inf, logits)
    exps = [jnp.exp(v - vals[0]) for v in vals]
    denom = exps[0] + exps[1] + exps[2] + exps[3]
    idx_out = jnp.zeros(lane.shape, jnp.int32)
    gate_out = jnp.zeros(lane.shape, F32)
    for k in range(TOP_K):
        idx_out = jnp.where(lane == k, idxs[k], idx_out)
        gate_out = jnp.where(lane == k, exps[k] / denom, gate_out)
    idx_ref[...] = idx_out
    gate_ref[...] = gate_out


def _router(x2d, gain, router_w, router_b, tm=256):
    t, d = x2d.shape
    e = router_w.shape[1]
    rw = jnp.zeros((d, LANES), F32).at[:, :e].set(router_w.astype(F32))
    rb = jnp.full((1, LANES), NEG_BIG, F32).at[0, :e].set(router_b.astype(F32))
    return pl.pallas_call(
        _router_kernel,
        grid=(t // tm,),
        in_specs=[
            pl.BlockSpec((tm, d), lambda i: (i, 0)),
            pl.BlockSpec((1, d), lambda i: (0, 0)),
            pl.BlockSpec((d, LANES), lambda i: (0, 0)),
            pl.BlockSpec((1, LANES), lambda i: (0, 0)),
        ],
        out_specs=[
            pl.BlockSpec((tm * ROW_SUBLANES, LANES), lambda i: (i, 0)),
            pl.BlockSpec((tm, LANES), lambda i: (i, 0)),
            pl.BlockSpec((tm, LANES), lambda i: (i, 0)),
        ],
        out_shape=[
            jax.ShapeDtypeStruct((t * ROW_SUBLANES, LANES), jnp.uint32),
            jax.ShapeDtypeStruct((t, LANES), jnp.int32),
            jax.ShapeDtypeStruct((t, LANES), F32),
        ],
        compiler_params=_cparams(("parallel",), 32),
        name="router",
    )(x2d, gain.reshape(1, d).astype(F32), rw, rb)


MOE_SUB = 256
MOE_NSUB = 6
MOE_TF = 256
MOE_ISSUE_ROWS = 64


def _expert_kernel(ge_ref, gs_ref, gn_ref, ng_ref, tot_ref,
                   hp_ref, idx_ref, wgu_ref, bgu_ref, wdn_ref, bdn_ref, ys_ref,
                   idx_smem, xu_ref, yst_ref, xb_ref, acc_ref, wgu_b_ref, wdn_b_ref, zero_ref,
                   sem_idx, sem_in, sem_out, *, n_sub_total):
    g = pl.program_id(0)
    f = pl.program_id(1)
    nf = pl.num_programs(1)
    ng = ng_ref[0]
    active = g < ng
    nsub = gn_ref[g]
    start = gs_ref[g]
    half = xb_ref.shape[2] // 2
    tf = wdn_ref.shape[1]
    sub_words = MOE_SUB * ROW_SUBLANES
    sub_shift = MOE_SUB.bit_length() - 1
    assert MOE_SUB == 1 << sub_shift

    def fetch_idx(gg):
        idx_cp = pltpu.make_async_copy(idx_ref.at[pl.ds(gs_ref[gg], MOE_NSUB)], idx_smem, sem_idx)
        idx_cp.start()
        idx_cp.wait()

    blk_rows = min(MOE_ISSUE_ROWS, MOE_SUB)

    def issue_block(blk, n_rows_group):
        base = blk * blk_rows

        @pl.when(base < n_rows_group)
        def _():
            s = base >> sub_shift
            r0 = base & (MOE_SUB - 1)
            for u in range(blk_rows):
                r = r0 + u
                src = hp_ref.at[pl.ds(pl.multiple_of(idx_smem[s, 0, r] * ROW_SUBLANES, ROW_SUBLANES),
                                      ROW_SUBLANES)]
                dst = xu_ref.at[s, pl.ds(pl.multiple_of(r * ROW_SUBLANES, ROW_SUBLANES), ROW_SUBLANES)]
                pltpu.make_async_copy(src, dst, sem_in.at[s]).start(priority=u % 2)

    def issue_blocks(first, n_rows_group):
        def one(blk, c):
            issue_block(blk, n_rows_group)
            return c

        lax.fori_loop(first, lax.div(n_rows_group + blk_rows - 1, blk_rows), one, 0)

    def wait_rows(s):
        pltpu.make_async_copy(hp_ref.at[pl.ds(0, sub_words)], xu_ref.at[s], sem_in.at[s]).wait()

    def out_copy(s, first_sub):
        r0 = pl.multiple_of((first_sub + s) * sub_words, sub_words)
        return pltpu.make_async_copy(yst_ref.at[s], ys_ref.at[pl.ds(r0, sub_words)], sem_out)

    def wait_out(gg):
        def wait(s, c):
            out_copy(s, gs_ref[gg]).wait()
            return c

        lax.fori_loop(0, gn_ref[gg], wait, 0)

    @pl.when(jnp.logical_and(g == 0, f == 0))
    def _():
        zero_ref[...] = jnp.zeros(zero_ref.shape, zero_ref.dtype)
        fetch_idx(0)
        issue_blocks(0, gn_ref[0] * MOE_SUB)

    @pl.when(jnp.logical_and(active, f == 0))
    def _():
        def unpack(s, c):
            wait_rows(s)
            for ch in range(ROW_SUBLANES):
                lo, hi = _unpack_chunk(xu_ref[s, pl.ds(ch, MOE_SUB, stride=ROW_SUBLANES), :])
                xb_ref[s, :, ch * LANES:(ch + 1) * LANES] = lo.astype(BF16)
                xb_ref[s, :, half + ch * LANES:half + (ch + 1) * LANES] = hi.astype(BF16)
            acc_ref[s] = jnp.broadcast_to(bdn_ref[0], acc_ref.shape[1:])
            return c

        lax.fori_loop(0, nsub, unpack, 0)

        @pl.when(g + 1 < ng)
        def _():
            fetch_idx(g + 1)

    @pl.when(active)
    def _():
        nxt = jnp.minimum(g + 1, pl.num_programs(0) - 1)
        next_rows = jnp.where(g + 1 < ng, gn_ref[nxt] * MOE_SUB, 0)

        def issue_next(s):
            issue_block(f * nsub + s, next_rows)

        wgu_b_ref[...] = wgu_ref[0].astype(BF16)
        hl = LANES // 2
        for m in range(tf // LANES):
            first = wdn_ref[0, m * LANES:m * LANES + hl, :]
            second = wdn_ref[0, m * LANES + hl:(m + 1) * LANES, :]
            lo = lax.bitcast_convert_type(first.astype(BF16).astype(F32), jnp.uint32) >> 16
            hi = lax.bitcast_convert_type(second.astype(BF16).astype(F32), jnp.uint32) & jnp.uint32(0xFFFF0000)
            wdn_b_ref[m * LANES:(m + 1) * LANES, :] = pltpu.bitcast(lo | hi, BF16)
        bgu = bgu_ref[0]
        lane = lax.broadcasted_iota(jnp.int32, (MOE_SUB, LANES), 1)
        even = (lane & 1) == 0

        def gate_up(s):
            gu = _dot(xb_ref[s], wgu_b_ref[...]) + bgu
            gates, ups = [], []
            for m in range(tf // LANES):
                a = gu[:, 2 * m * LANES:(2 * m + 1) * LANES]
                b = gu[:, (2 * m + 1) * LANES:(2 * m + 2) * LANES]
                gates.append(jnp.where(even, a, pltpu.roll(b, 1, 1)))
                ups.append(jnp.where(even, pltpu.roll(a, LANES - 1, 1), b))
            gate = jnp.minimum(jnp.concatenate(gates, axis=1), SWIGLU_LIMIT)
            up = jnp.clip(jnp.concatenate(ups, axis=1), -SWIGLU_LIMIT, SWIGLU_LIMIT)
            return ((up + 1.0) * gate * jax.nn.sigmoid(SWIGLU_ALPHA * gate)).astype(BF16)

        def down(s, act):
            acc_ref[s] += _dot(act, wdn_b_ref[...])

        def body(s, act_prev):
            issue_next(s)
            act = gate_up(s)
            down(s - 1, act_prev)
            return act

        issue_next(0)
        act_last = lax.fori_loop(1, nsub, body, gate_up(0))
        down(nsub - 1, act_last)

        @pl.when(f == nf - 1)
        def _():
            issue_blocks(nf * nsub, next_rows)

    @pl.when(jnp.logical_and(active, f == nf - 1))
    def _():
        @pl.when(g > 0)
        def _():
            wait_out(g - 1)

        def issue(s, c):
            def store(ch, chunk):
                yst_ref[s, pl.ds(ch, MOE_SUB, stride=ROW_SUBLANES), :] = chunk

            _pack_rows(acc_ref[s], store)
            out_copy(s, start).start()
            return c

        lax.fori_loop(0, nsub, issue, 0)

    @pl.when(jnp.logical_and(g == pl.num_programs(0) - 1, f == nf - 1))
    def _():
        wait_out(ng - 1)

        def fill(s, c):
            r0 = pl.multiple_of(s * sub_words, sub_words)
            cp = pltpu.make_async_copy(zero_ref, ys_ref.at[pl.ds(r0, sub_words)], sem_out)
            cp.start()
            cp.wait()
            return c

        lax.fori_loop(tot_ref[0], n_sub_total, fill, 0)


def _experts(hp, row_tok, grp_expert, grp_start, grp_nsub, n_groups, tot_sub, w_gu, b_gu, w_dn, b_dn):
    n_rows = row_tok.shape[0]
    sub_words = MOE_SUB * ROW_SUBLANES
    idx2 = jnp.concatenate([row_tok.reshape(n_rows // MOE_SUB, 1, MOE_SUB),
                            jnp.zeros((MOE_NSUB, 1, MOE_SUB), jnp.int32)], axis=0)
    e, d, f2 = w_gu.shape
    fdim = f2 // 2
    tf = MOE_TF
    nf = fdim // tf
    n_grp = grp_expert.shape[0]

    def wsel(g, f, ge, gs, gn, ng, tot):
        on = g < ng[0]
        last = jnp.maximum(ng[0] - 1, 0)
        return jnp.where(on, ge[g], ge[last]), jnp.where(on, f, nf - 1)

    def w_gu_map(g, f, *pre):
        ee, ff = wsel(g, f, *pre)
        return (ee, 0, ff)

    def w_dn_map(g, f, *pre):
        ee, ff = wsel(g, f, *pre)
        return (ee, ff, 0)

    def b_dn_map(g, f, *pre):
        ee, _ = wsel(g, f, *pre)
        return (ee, 0, 0)

    grid_spec = pltpu.PrefetchScalarGridSpec(
        num_scalar_prefetch=5,
        grid=(n_grp, nf),
        in_specs=[
            pl.BlockSpec(memory_space=pl.ANY),
            pl.BlockSpec(memory_space=pl.ANY),
            pl.BlockSpec((1, d, 2 * tf), w_gu_map),
            pl.BlockSpec((1, 1, 2 * tf), w_gu_map),
            pl.BlockSpec((1, tf, d), w_dn_map),
            pl.BlockSpec((1, 1, d), b_dn_map),
        ],
        out_specs=pl.BlockSpec(memory_space=pl.ANY),
        scratch_shapes=[
            pltpu.SMEM((MOE_NSUB, 1, MOE_SUB), jnp.int32),
            pltpu.VMEM((MOE_NSUB, sub_words, LANES), jnp.uint32),
            pltpu.VMEM((MOE_NSUB, sub_words, LANES), jnp.uint32),
            pltpu.VMEM((MOE_NSUB, MOE_SUB, d), BF16),
            pltpu.VMEM((MOE_NSUB, MOE_SUB, d), F32),
            pltpu.VMEM((d, 2 * tf), BF16),
            pltpu.VMEM((tf, d), BF16),
            pltpu.VMEM((sub_words, LANES), jnp.uint32),
            pltpu.SemaphoreType.DMA(()),
            pltpu.SemaphoreType.DMA((MOE_NSUB,)),
            pltpu.SemaphoreType.DMA(()),
        ],
    )
    return pl.pallas_call(
        functools.partial(_expert_kernel, n_sub_total=n_rows // MOE_SUB),
        grid_spec=grid_spec,
        out_shape=jax.ShapeDtypeStruct((n_rows * ROW_SUBLANES, LANES), jnp.uint32),
        compiler_params=_cparams(("arbitrary", "arbitrary"), 56),
        name="moe_experts",
    )(grp_expert, grp_start, grp_nsub, n_groups, tot_sub, hp, idx2,
      w_gu, b_gu.reshape(e, 1, f2).astype(F32), w_dn, b_dn.reshape(e, 1, d).astype(F32))


def _combine_ple_kernel(pr_ref, prn_ref, gate_ref, x_ref, ys_ref, ys2_ref, p_ref, gn_ref, wg_ref, wu_ref, gp_ref,
                        o_ref, buf_ref, x2_ref, sem, *, tc):
    i = pl.program_id(0)
    slot = i & 1
    n_pairs = tc * TOP_K
    tile_words = tc * ROW_SUBLANES

    def issue(idx_ref, sl):
        def pair(p2, carry):
            for u in range(2):
                p = 2 * p2 + u
                tok = p >> 2
                pltpu.make_async_copy(
                    ys_ref.at[idx_ref[0, 0, p]],
                    buf_ref.at[sl, p & (TOP_K - 1),
                               pl.ds(pl.multiple_of(tok * ROW_SUBLANES, ROW_SUBLANES), ROW_SUBLANES)],
                    sem.at[sl]).start(priority=u)
            return carry

        lax.fori_loop(0, n_pairs // 2, pair, 0, unroll=4)

    @pl.when(i == 0)
    def _():
        issue(pr_ref, 0)

    @pl.when(i + 1 < pl.num_programs(0))
    def _():
        issue(prn_ref, 1 - slot)

    for k in range(TOP_K):
        pltpu.make_async_copy(ys2_ref.at[pl.ds(0, tile_words)], buf_ref.at[slot, k], sem.at[slot]).wait()

    gates = gate_ref[...]
    half = x_ref.shape[1] // 2
    for ch in range(ROW_SUBLANES):
        lo_cols = slice(ch * LANES, (ch + 1) * LANES)
        hi_cols = slice(half + ch * LANES, half + (ch + 1) * LANES)
        acc_lo = x_ref[:, lo_cols]
        acc_hi = x_ref[:, hi_cols]
        for k in range(TOP_K):
            lo, hi = _unpack_chunk(buf_ref[slot, k, pl.ds(ch, tc, stride=ROW_SUBLANES), :])
            acc_lo = acc_lo + gates[:, k:k + 1] * lo
            acc_hi = acc_hi + gates[:, k:k + 1] * hi
        x2_ref[:, lo_cols] = acc_lo
        x2_ref[:, hi_cols] = acc_hi

    x = x2_ref[...]
    ms = jnp.mean(x * x, axis=-1, keepdims=True)
    h = (x * lax.rsqrt(ms + RMS_EPS) * gn_ref[...]).astype(BF16)
    gate = jax.nn.sigmoid(_dot(h, wg_ref[...]))
    up = _dot(p_ref[...].astype(BF16), wu_ref[...])
    ms_u = jnp.mean(up * up, axis=-1, keepdims=True)
    o_ref[...] = x + up * lax.rsqrt(ms_u + RMS_EPS) * gp_ref[...] * gate


def _combine_ple(x2d, gates, pair_row, ys, p2d, g_norm, w_gate, w_up, g_post, tc=256):
    t, d = x2d.shape
    pd = p2d.shape[1]
    nt = t // tc
    n_rows = ys.shape[0] // ROW_SUBLANES
    pr = pair_row.reshape(nt, 1, tc * TOP_K)
    const = lambda i: (0, 0)
    return pl.pallas_call(
        functools.partial(_combine_ple_kernel, tc=tc),
        grid=(nt,),
        in_specs=[
            pl.BlockSpec((1, 1, tc * TOP_K), lambda i: (i, 0, 0), memory_space=pltpu.SMEM),
            pl.BlockSpec((1, 1, tc * TOP_K), lambda i: (jnp.minimum(i + 1, nt - 1), 0, 0),
                         memory_space=pltpu.SMEM),
            pl.BlockSpec((tc, LANES), lambda i: (i, 0)),
            pl.BlockSpec((tc, d), lambda i: (i, 0)),
            pl.BlockSpec(memory_space=pl.ANY),
            pl.BlockSpec(memory_space=pl.ANY),
            pl.BlockSpec((tc, pd), lambda i: (i, 0)),
            pl.BlockSpec((1, d), const),
            pl.BlockSpec((d, d), const),
            pl.BlockSpec((pd, d), const),
            pl.BlockSpec((1, d), const),
        ],
        out_specs=pl.BlockSpec((tc, d), lambda i: (i, 0)),
        out_shape=jax.ShapeDtypeStruct((t, d), F32),
        scratch_shapes=[
            pltpu.VMEM((2, TOP_K, tc * ROW_SUBLANES, LANES), jnp.uint32),
            pltpu.VMEM((tc, d), F32),
            pltpu.SemaphoreType.DMA((2,)),
        ],
        compiler_params=_cparams(("arbitrary",), 48),
        name="moe_combine_ple",
    )(pr, pr, gates, x2d, ys.reshape(n_rows, ROW_SUBLANES, LANES), ys, p2d,
      g_norm.reshape(1, d).astype(F32), w_gate, w_up, g_post.reshape(1, d).astype(F32))


def _moe(x2d, gain, router_w, router_b, w_gu, b_gu, w_dn, b_dn):
    t, d = x2d.shape
    e = router_w.shape[1]
    hp, idx, gates = _router(x2d, gain, router_w, router_b)

    n_pairs = t * TOP_K
    flat_e = idx[:, :TOP_K].reshape(n_pairs)
    onehot = (flat_e[:, None] == jnp.arange(e, dtype=jnp.int32)[None, :]).astype(jnp.int32)
    csum = jnp.cumsum(onehot, axis=0)
    rank = jnp.sum(onehot * csum, axis=1) - 1
    counts = csum[-1]
    sub_e = (counts + MOE_SUB - 1) // MOE_SUB
    sub_end = jnp.cumsum(sub_e)
    sub_start = sub_end - sub_e
    dest = (jnp.sum(onehot * (sub_start * MOE_SUB)[None, :], axis=1) + rank).astype(jnp.int32)
    n_sub_total = n_pairs // MOE_SUB + e
    n_rows = n_sub_total * MOE_SUB
    row_tok = jnp.zeros((n_rows,), jnp.int32).at[dest].set(jnp.arange(n_pairs, dtype=jnp.int32) // TOP_K)

    grp_e = (sub_e + MOE_NSUB - 1) // MOE_NSUB
    grp_end = jnp.cumsum(grp_e)
    n_grp_max = (n_sub_total + MOE_NSUB - 1) // MOE_NSUB + e
    gidx = jnp.arange(n_grp_max, dtype=jnp.int32)
    g_exp = jnp.minimum(jnp.searchsorted(grp_end, gidx, side="right"), e - 1).astype(jnp.int32)
    g_local = gidx - (grp_end - grp_e)[g_exp]
    g_on = gidx < grp_end[-1]
    g_start = jnp.where(g_on, sub_start[g_exp] + g_local * MOE_NSUB, 0).astype(jnp.int32)
    g_nsub = jnp.where(g_on, jnp.minimum(MOE_NSUB, sub_e[g_exp] - g_local * MOE_NSUB), 0).astype(jnp.int32)

    ys = _experts(hp, row_tok, g_exp, g_start, g_nsub,
                  grp_end[-1].astype(jnp.int32).reshape(1), sub_end[-1].astype(jnp.int32).reshape(1),
                  w_gu, b_gu, w_dn, b_dn)
    return gates, dest, ys


def kernel(x, p, w_in, da_q_norm, da_k_norm, da_lambda_q1, da_lambda_k1, da_lambda_q2, da_lambda_k2, da_subln, gla_gate_w2, gla_gate_b, gla_out_norm, w_branch_da, w_branch_gla, w_merge_gate, b_merge_gate, w_out, norm_mix, norm_ffn, router_w, router_b, w_gate_up, b_gate_up, w_down, b_down, norm_ple, w_ple_gate, w_ple_up, norm_ple_post):
    bsz, s_len, d = x.shape
    t = bsz * s_len
    depth = w_in.shape[0]
    qk_w = DA_HEADS * 2 * DA_HEAD_DIM
    v_w = DA_HEADS * DA_V_DIM
    gk_w = gla_gate_w2.shape[2]
    gv_w = w_branch_gla.shape[1]
    rest_w = 2 * gk_w + 2 * gv_w
    x2d = x.reshape(t, d)
    for i in range(depth):
        lambda_init = 0.8 - 0.6 * math.exp(-0.3 * i)
        h = _rmsnorm(x2d, norm_mix[i])
        w = w_in[i]
        c_v = 2 * qk_w
        c_rest = c_v + v_w
        qn = _proj(h, w_in, i, 0, qk_w, out_dtype=BF16, tm=1024, tn=1024,
                   gain=da_q_norm[i], scale=DA_HEAD_DIM ** -0.5 * LOG2E)
        kn = _proj(h, w_in, i, qk_w, qk_w, out_dtype=BF16, tm=1024, tn=1024, gain=da_k_norm[i])
        vt = _proj_t(h, w_in, i, c_v, v_w, tm=DA_TQ, tn=1024)
        rest = _proj(h, w_in, i, c_rest, rest_w, out_dtype=BF16, tm=1024, tn=1024)
        w_lr = jnp.zeros((1, d, LANES), F32).at[0, :, :GLA_GATE_RANK].set(
            w_in[i, :, c_rest + rest_w:].astype(F32))
        glr = _proj(h, w_lr, 0, 0, LANES, out_dtype=F32, tm=512, tn=LANES)

        lam_vecs = jnp.stack([da_lambda_q1[i], da_lambda_k1[i], da_lambda_q2[i], da_lambda_k2[i]])
        y_da = _diff_attention(qn, kn, vt, lam_vecs, da_subln[i], bsz, s_len, lambda_init, DA_TQ)

        w2p = jnp.zeros((LANES, gk_w), F32).at[:GLA_GATE_RANK].set(gla_gate_w2[i].astype(F32))
        y_gla = _gla(rest, (0, gk_w, 2 * gk_w, 2 * gk_w + gv_w), glr, w2p,
                     gla_gate_b[i], gla_out_norm[i], bsz, s_len)

        mixed = _merge(h, y_da, y_gla, w_merge_gate[i].astype(BF16), b_merge_gate[i].reshape(1, 2 * d).astype(F32),
                       w_branch_da[i].astype(BF16), w_branch_gla[i].astype(BF16))
        x2d = _out_proj(x2d, mixed, w_out[i].astype(BF16))

        gates, pair_row, ys = _moe(x2d, norm_ffn[i], router_w[i], router_b[i], w_gate_up[i], b_gate_up[i],
                                   w_down[i], b_down[i])
        x2d = _combine_ple(x2d, gates, pair_row, ys, p[i].reshape(t, p.shape[-1]), norm_ple[i],
                           w_ple_gate[i].astype(BF16), w_ple_up[i].astype(BF16), norm_ple_post[i])
    return x2d.reshape(bsz, s_len, d)
```

```python
import functools
import math

import jax
import jax.numpy as jnp
from jax import lax
from jax.experimental import pallas as pl
from jax.experimental.pallas import tpu as pltpu

F32 = jnp.float32
BF16 = jnp.bfloat16
HIGHEST = lax.Precision.HIGHEST

CHUNK = 64
CHUNK_SHIFT = CHUNK.bit_length() - 1
RMS_EPS = 1e-6
DA_HEADS = 8
DA_HEAD_DIM = 128
DA_V_DIM = 2 * DA_HEAD_DIM
GLA_HEADS = 4
GLA_GATE_RANK = 16
GLA_TAU = 16.0
N_EXPERTS = 32
TOP_K = 4
SWIGLU_LIMIT = 7.0
SWIGLU_ALPHA = 1.702

DA_TQ = 1024
DA_HPS = 2
DA_ONES_ROWS = 16
LOG2E = 1.4426950408889634

LANES = 128
NEG_BIG = -1e30

MIB = 1024 * 1024


def _cparams(sem, vmem_mib, flags=None):
    return pltpu.CompilerParams(dimension_semantics=sem, vmem_limit_bytes=vmem_mib * MIB, flags=flags)


def _dot(a, b):
    return jnp.dot(a, b, preferred_element_type=F32)


def _dot_nt(a, b):
    return lax.dot_general(a, b, (((1,), (1,)), ((), ())), preferred_element_type=F32)


def _dot_tn(a, b):
    return lax.dot_general(a, b, (((0,), (0,)), ((), ())), preferred_element_type=F32)


def _rmsnorm_kernel(x_ref, g_ref, o_ref):
    x = x_ref[...]
    ms = jnp.mean(x * x, axis=-1, keepdims=True)
    o_ref[...] = (x * lax.rsqrt(ms + RMS_EPS) * g_ref[...]).astype(o_ref.dtype)


def _rmsnorm(x2d, gain, tm=1024):
    t, d = x2d.shape
    return pl.pallas_call(
        _rmsnorm_kernel,
        grid=(t // tm,),
        in_specs=[pl.BlockSpec((tm, d), lambda i: (i, 0)), pl.BlockSpec((1, d), lambda i: (0, 0))],
        out_specs=pl.BlockSpec((tm, d), lambda i: (i, 0)),
        out_shape=jax.ShapeDtypeStruct((t, d), BF16),
        compiler_params=_cparams(("parallel",), 32),
        name="rmsnorm",
    )(x2d, gain.reshape(1, d).astype(F32))


def _proj_kernel(a_ref, w_ref, g_ref, o_ref, wb_ref, *, group_norm, scale):
    @pl.when(pl.program_id(1) == 0)
    def _():
        wb_ref[...] = w_ref[...].astype(BF16)

    acc = _dot(a_ref[...], wb_ref[...])
    if group_norm:
        for c in range(acc.shape[1] // DA_HEAD_DIM):
            blk = acc[:, c * DA_HEAD_DIM:(c + 1) * DA_HEAD_DIM]
            ms = jnp.mean(blk * blk, axis=-1, keepdims=True)
            y = blk * lax.rsqrt(ms + RMS_EPS) * g_ref[...] * scale
            o_ref[:, c * DA_HEAD_DIM:(c + 1) * DA_HEAD_DIM] = y.astype(o_ref.dtype)
    else:
        o_ref[...] = acc.astype(o_ref.dtype)


def _proj(a, w, layer, col0, n, *, out_dtype, tm, tn, gain=None, scale=1.0):
    t, k = a.shape
    assert col0 % tn == 0 and n % tn == 0
    cb = col0 // tn
    group_norm = gain is not None
    g = (gain if group_norm else jnp.ones((DA_HEAD_DIM,), F32)).reshape(1, DA_HEAD_DIM).astype(F32)
    return pl.pallas_call(
        functools.partial(_proj_kernel, group_norm=group_norm, scale=scale),
        grid=(n // tn, t // tm),
        in_specs=[
            pl.BlockSpec((tm, k), lambda j, i: (i, 0)),
            pl.BlockSpec((None, k, tn), lambda j, i: (layer, 0, cb + j)),
            pl.BlockSpec((1, DA_HEAD_DIM), lambda j, i: (0, 0)),
        ],
        out_specs=pl.BlockSpec((tm, tn), lambda j, i: (i, j)),
        out_shape=jax.ShapeDtypeStruct((t, n), out_dtype),
        scratch_shapes=[pltpu.VMEM((k, tn), BF16)],
        compiler_params=_cparams(("parallel", "arbitrary"), 48),
        name="proj",
    )(a, w.astype(F32), g)


def _proj_t_kernel(w_ref, a_ref, o_ref, wt_ref):
    @pl.when(pl.program_id(1) == 0)
    def _():
        wt_ref[...] = w_ref[...].T.astype(BF16)

    o_ref[0] = _dot_nt(wt_ref[...], a_ref[...]).astype(o_ref.dtype)


def _proj_t(a, w, layer, col0, n, *, tm, tn):
    t, k = a.shape
    assert col0 % tn == 0 and n % tn == 0
    cb = col0 // tn
    return pl.pallas_call(
        _proj_t_kernel,
        grid=(n // tn, t // tm),
        in_specs=[
            pl.BlockSpec((None, k, tn), lambda j, i: (layer, 0, cb + j)),
            pl.BlockSpec((tm, k), lambda j, i: (i, 0)),
        ],
        out_specs=pl.BlockSpec((1, tn, tm), lambda j, i: (i, j, 0)),
        out_shape=jax.ShapeDtypeStruct((t // tm, n, tm), BF16),
        scratch_shapes=[pltpu.VMEM((tn, k), BF16)],
        compiler_params=_cparams(("parallel", "arbitrary"), 48),
        name="proj_t",
    )(w.astype(F32), a)


def _da_kernel(slope_ref, q_ref, k_ref, vt_ref, lamv_ref, subg_ref, o_ref,
               kaug_ref, qaug_ref, vaug_ref, m_ref, acc_ref, *, tq, s_len, lambda_init):
    hp = pl.program_id(1)
    i = pl.program_id(2)
    hd = DA_HEAD_DIM
    dv = DA_V_DIM
    nq = vt_ref.shape[0]
    chains = [(hh, c) for hh in range(DA_HPS) for c in range(2)]
    slopes2 = [slope_ref[hp * DA_HPS + hh] * LOG2E for hh in range(DA_HPS)]

    @pl.when(i == 0)
    def _():
        pos = lax.broadcasted_iota(jnp.int32, (s_len, hd), 0)
        lane = lax.broadcasted_iota(jnp.int32, (s_len, hd), 1)
        piece = jnp.where(lane < 6, jnp.where((lane & 1) == 0, (pos >> 4) << 4, pos & 15), 0)
        piece = piece.astype(F32).astype(BF16)
        for hh, c in chains:
            col0 = hh * dv + c * hd
            kaug_ref[hh * 2 + c, :, :hd] = k_ref[:, col0:col0 + hd]
            kaug_ref[hh * 2 + c, :, hd:] = piece
        for hh in range(DA_HPS):
            for j in range(nq):
                vaug_ref[hh, j, :dv, :] = vt_ref[j, hh * dv:(hh + 1) * dv, :]
                vaug_ref[hh, j, dv:, :] = jnp.ones((vaug_ref.shape[2] - dv, tq), BF16)

    lane_q = lax.broadcasted_iota(jnp.int32, (tq, hd), 1)
    for hh in range(DA_HPS):
        s_full = jnp.full((tq, hd), slopes2[hh], F32)
        s1 = s_full.astype(BF16).astype(F32)
        r1 = s_full - s1
        s2 = r1.astype(BF16).astype(F32)
        s3 = r1 - s2
        slope_cols = jnp.where(lane_q < 2, s1, jnp.where(lane_q < 4, s2, jnp.where(lane_q < 6, s3, 0.0)))
        slope_cols = slope_cols.astype(BF16)
        for c in range(2):
            col0 = hh * dv + c * hd
            qaug_ref[hh * 2 + c, :, :hd] = q_ref[:, col0:col0 + hd]
            qaug_ref[hh * 2 + c, :, hd:] = slope_cols

    m_ref[...] = jnp.full(m_ref.shape, NEG_BIG, F32)
    acc_ref[...] = jnp.zeros(acc_ref.shape, F32)

    def update(ci, st, va, qs=slice(None)):
        m_old = m_ref[ci, :, qs]
        m_new = jnp.maximum(m_old, jnp.max(st, axis=0, keepdims=True))
        alpha = jnp.exp2(m_old - m_new)
        p = jnp.exp2(st - m_new)
        acc_ref[ci, :, qs] = alpha * acc_ref[ci, :, qs] + _dot(va, p.astype(BF16))
        m_ref[ci, :, qs] = m_new

    def past_tile(j, carry):
        r0 = pl.multiple_of(j * tq, tq)
        for hh, c in chains:
            ci = hh * 2 + c
            update(ci, _dot_nt(kaug_ref[ci, pl.ds(r0, tq), :], qaug_ref[ci]), vaug_ref[hh, j])
        return carry

    lax.fori_loop(0, i, past_tile, 0)

    hk = tq // 2
    r0 = pl.multiple_of(i * tq, tq)

    def diag_part(k0, q0):
        nq_part = tq - q0
        krow = k0 + lax.broadcasted_iota(jnp.int32, (hk, nq_part), 0)
        qcol = q0 + lax.broadcasted_iota(jnp.int32, (hk, nq_part), 1)
        ahead = jnp.maximum(krow - qcol, 0).astype(F32)
        allowed = (krow >> CHUNK_SHIFT) <= (qcol >> CHUNK_SHIFT)
        qs = slice(q0, tq)
        for hh, c in chains:
            ci = hh * 2 + c
            st = _dot_nt(kaug_ref[ci, pl.ds(pl.multiple_of(r0 + k0, hk), hk), :], qaug_ref[ci, qs, :])
            st = jnp.where(allowed, st + (-2.0 * slopes2[hh]) * ahead, NEG_BIG)
            update(ci, st, vaug_ref[hh, i, :, k0:k0 + hk], qs)

    diag_part(0, 0)
    diag_part(hk, hk)

    lamv = lamv_ref[...]
    lam = (jnp.exp(jnp.sum(lamv[0:1] * lamv[1:2], axis=-1, keepdims=True))
           - jnp.exp(jnp.sum(lamv[2:3] * lamv[3:4], axis=-1, keepdims=True)) + lambda_init)
    for hh in range(DA_HPS):
        a0 = acc_ref[hh * 2]
        a1 = acc_ref[hh * 2 + 1]
        ot = a0[:dv, :] / a0[dv:dv + 1, :] - lam * (a1[:dv, :] / a1[dv:dv + 1, :])
        ms = jnp.mean(ot * ot, axis=0, keepdims=True)
        o = (ot * lax.rsqrt(ms + RMS_EPS)).T
        o_ref[:, hh * dv:(hh + 1) * dv] = (o * subg_ref[...] * (1.0 - lambda_init)).astype(o_ref.dtype)


def _diff_attention(qn, kn, vt, lam_vecs, subln_g, bsz, s_len, lambda_init, tq):
    assert s_len <= 4096 and s_len % tq == 0 and tq % CHUNK == 0 and DA_HEADS % DA_HPS == 0
    t = bsz * s_len
    nq = s_len // tq
    w = DA_HPS * DA_V_DIM
    rows_aug = DA_V_DIM + DA_ONES_ROWS
    slopes = jnp.exp2(-8.0 * jnp.arange(1, DA_HEADS + 1, dtype=F32) / DA_HEADS)
    grid_spec = pltpu.PrefetchScalarGridSpec(
        num_scalar_prefetch=1,
        grid=(bsz, DA_HEADS // DA_HPS, nq),
        in_specs=[
            pl.BlockSpec((tq, w), lambda b, h, i, s: (b * nq + i, h)),
            pl.BlockSpec((s_len, w), lambda b, h, i, s: (b, h)),
            pl.BlockSpec((nq, w, tq), lambda b, h, i, s: (b, h, 0)),
            pl.BlockSpec((4, DA_HEAD_DIM), lambda b, h, i, s: (0, 0)),
            pl.BlockSpec((1, DA_V_DIM), lambda b, h, i, s: (0, 0)),
        ],
        out_specs=pl.BlockSpec((tq, w), lambda b, h, i, s: (b * nq + i, h)),
        scratch_shapes=[
            pltpu.VMEM((2 * DA_HPS, s_len, 2 * DA_HEAD_DIM), BF16),
            pltpu.VMEM((2 * DA_HPS, tq, 2 * DA_HEAD_DIM), BF16),
            pltpu.VMEM((DA_HPS, nq, rows_aug, tq), BF16),
            pltpu.VMEM((2 * DA_HPS, 1, tq), F32),
            pltpu.VMEM((2 * DA_HPS, rows_aug, tq), F32),
        ],
    )
    return pl.pallas_call(
        functools.partial(_da_kernel, tq=tq, s_len=s_len, lambda_init=lambda_init),
        grid_spec=grid_spec,
        out_shape=jax.ShapeDtypeStruct((t, DA_HEADS * DA_V_DIM), BF16),
        compiler_params=_cparams(("parallel", "parallel", "arbitrary"), 56),
        name="diff_attention",
    )(slopes, qn, kn, vt, lam_vecs.astype(F32), subln_g.reshape(1, DA_V_DIM).astype(F32))


def _gla_kernel(q_ref, k_ref, v_ref, g_ref, glr_ref, w2_ref, gb_ref, og_ref, o_ref, state_ref,
                *, tb, dk, dv):
    @pl.when(pl.program_id(0) == 0)
    def _():
        state_ref[...] = jnp.zeros(state_ref.shape, F32)

    bsz = q_ref.shape[0]
    row = lax.broadcasted_iota(jnp.int32, (CHUNK, CHUNK), 0)
    col = lax.broadcasted_iota(jnp.int32, (CHUNK, CHUNK), 1)
    lower = row >= col
    tri = lower.astype(F32)
    mid = CHUNK // 2

    def chunk(c, carry):
        r0 = pl.multiple_of(c * CHUNK, CHUNK)
        rows = pl.ds(r0, CHUNK)
        for bi in range(bsz):
            z = jnp.dot(glr_ref[bi, rows, :], w2_ref[...], precision=HIGHEST,
                        preferred_element_type=F32) + gb_ref[...]
            log_a = (jnp.minimum(z, 0.0) - jnp.log1p(jnp.exp(-jnp.abs(z)))) * (1.0 / GLA_TAU)
            b_all = jnp.dot(tri, log_a, precision=HIGHEST, preferred_element_type=F32)
            for h in range(GLA_HEADS):
                ks = slice(h * dk, (h + 1) * dk)
                vs = slice(h * dv, (h + 1) * dv)
                b = b_all[:, ks]
                b_last = b[CHUNK - 1:CHUNK, :]
                b_mid = b[mid:mid + 1, :]
                q = q_ref[bi, rows, ks].astype(F32) * (dk ** -0.5)
                k = k_ref[bi, rows, ks].astype(F32)
                v = v_ref[bi, rows, vs]
                e_fwd = jnp.exp(b - b_mid)
                e_bwd = jnp.exp(b_mid - b)
                a_lo = _dot_nt((q * e_fwd).astype(BF16), (k * e_bwd).astype(BF16))
                a_up = _dot_nt((q * e_bwd).astype(BF16), (k * e_fwd).astype(BF16))
                attn = jnp.where(lower, a_lo, a_up)
                state = state_ref[bi, h]
                o = _dot(attn.astype(BF16), v) + _dot_nt((q * jnp.exp(b)).astype(BF16), state.astype(BF16))
                kd = (k * jnp.exp(b_last - b)).astype(BF16)
                state_ref[bi, h] = state * jnp.exp(b_last) + _dot_tn(v, kd)
                ms = jnp.mean(o * o, axis=-1, keepdims=True)
                g = g_ref[bi, rows, vs].astype(F32)
                y = o * lax.rsqrt(ms + RMS_EPS) * og_ref[...] * (g * jax.nn.sigmoid(g))
                o_ref[bi, rows, vs] = y.astype(o_ref.dtype)
        return carry

    lax.fori_loop(0, tb // CHUNK, chunk, 0)


def _gla(src, cols, glr, w2p, gate_b, out_g, bsz, s_len, tb=512):
    t = bsz * s_len
    nb = s_len // tb
    kw = w2p.shape[1]
    dk = kw // GLA_HEADS
    dv = out_g.shape[0]
    vw = dv * GLA_HEADS
    cq, ck, cv, cg = cols
    src3 = src.reshape(bsz, s_len, src.shape[1])
    out = pl.pallas_call(
        functools.partial(_gla_kernel, tb=tb, dk=dk, dv=dv),
        grid=(nb,),
        in_specs=[
            pl.BlockSpec((bsz, tb, kw), lambda i: (0, i, cq // kw)),
            pl.BlockSpec((bsz, tb, kw), lambda i: (0, i, ck // kw)),
            pl.BlockSpec((bsz, tb, vw), lambda i: (0, i, cv // vw)),
            pl.BlockSpec((bsz, tb, vw), lambda i: (0, i, cg // vw)),
            pl.BlockSpec((bsz, tb, LANES), lambda i: (0, i, 0)),
            pl.BlockSpec((LANES, kw), lambda i: (0, 0)),
            pl.BlockSpec((1, kw), lambda i: (0, 0)),
            pl.BlockSpec((1, dv), lambda i: (0, 0)),
        ],
        out_specs=pl.BlockSpec((bsz, tb, vw), lambda i: (0, i, 0)),
        out_shape=jax.ShapeDtypeStruct((bsz, s_len, vw), BF16),
        scratch_shapes=[pltpu.VMEM((bsz, GLA_HEADS, dv, dk), F32)],
        compiler_params=_cparams(("arbitrary",), 56),
        name="gla",
    )(src3, src3, src3, src3, glr.reshape(bsz, s_len, LANES), w2p,
      gate_b.reshape(1, kw).astype(F32), out_g.reshape(1, dv).astype(F32))
    return out.reshape(t, vw)


def _merge_kernel(h_ref, ya_ref, yb_ref, wga_ref, wgb_ref, ba_ref, bb_ref, wa_ref, wb_ref, o_ref):
    h = h_ref[...]
    ga = jax.nn.sigmoid(_dot(h, wga_ref[...]) + ba_ref[...])
    gb = jax.nn.sigmoid(_dot(h, wgb_ref[...]) + bb_ref[...])
    mixed = ga * _dot(ya_ref[...], wa_ref[...]) + gb * _dot(yb_ref[...], wb_ref[...])
    o_ref[...] = mixed.astype(o_ref.dtype)


def _merge(h, y_da, y_gla, w_gate, b_gate, w_da, w_gla, tm=512, tn=512):
    t, d = h.shape
    nb = d // tn
    act = lambda: pl.BlockSpec((tm, d), lambda j, i: (i, 0))
    return pl.pallas_call(
        _merge_kernel,
        grid=(nb, t // tm),
        in_specs=[
            act(), act(), act(),
            pl.BlockSpec((d, tn), lambda j, i: (0, j)),
            pl.BlockSpec((d, tn), lambda j, i: (0, nb + j)),
            pl.BlockSpec((1, tn), lambda j, i: (0, j)),
            pl.BlockSpec((1, tn), lambda j, i: (0, nb + j)),
            pl.BlockSpec((d, tn), lambda j, i: (0, j)),
            pl.BlockSpec((d, tn), lambda j, i: (0, j)),
        ],
        out_specs=pl.BlockSpec((tm, tn), lambda j, i: (i, j)),
        out_shape=jax.ShapeDtypeStruct((t, d), BF16),
        compiler_params=_cparams(("parallel", "parallel"), 48),
        name="merge",
    )(h, y_da, y_gla, w_gate, w_gate, b_gate, b_gate, w_da, w_gla)


def _out_proj_kernel(x_ref, m_ref, w_ref, o_ref):
    o_ref[...] = x_ref[...] + _dot(m_ref[...], w_ref[...])


def _out_proj(x2d, mixed, w_out, tm=512):
    t, d = x2d.shape
    return pl.pallas_call(
        _out_proj_kernel,
        grid=(t // tm,),
        in_specs=[
            pl.BlockSpec((tm, d), lambda i: (i, 0)),
            pl.BlockSpec((tm, d), lambda i: (i, 0)),
            pl.BlockSpec((d, d), lambda i: (0, 0)),
        ],
        out_specs=pl.BlockSpec((tm, d), lambda i: (i, 0)),
        out_shape=jax.ShapeDtypeStruct((t, d), F32),
        compiler_params=_cparams(("parallel",), 48),
        name="out_proj",
    )(x2d, mixed, w_out)


ROW_SUBLANES = 8


def _pack_rows(val, store):
    half = val.shape[1] // 2
    assert half == ROW_SUBLANES * LANES
    lo = lax.bitcast_convert_type(val[:, :half].astype(BF16).astype(F32), jnp.uint32)
    hi = lax.bitcast_convert_type(val[:, half:].astype(BF16).astype(F32), jnp.uint32)
    packed = (lo >> 16) | (hi & jnp.uint32(0xFFFF0000))
    for c in range(ROW_SUBLANES):
        store(c, packed[:, c * LANES:(c + 1) * LANES])


def _unpack_chunk(chunk):
    lo = lax.bitcast_convert_type(chunk << 16, F32)
    hi = lax.bitcast_convert_type(chunk & jnp.uint32(0xFFFF0000), F32)
    return lo, hi


def _router_kernel(x_ref, g_ref, rw_ref, rb_ref, hp_ref, idx_ref, gate_ref):
    x = x_ref[...]
    tm = x.shape[0]
    ms = jnp.mean(x * x, axis=-1, keepdims=True)
    h = x * lax.rsqrt(ms + RMS_EPS) * g_ref[...]

    def store(c, chunk):
        hp_ref[pl.ds(c, tm, stride=ROW_SUBLANES), :] = chunk

    _pack_rows(h, store)

    logits = jnp.dot(h, rw_ref[...], precision=HIGHEST, preferred_element_type=F32) + rb_ref[...]
    lane = lax.broadcasted_iota(jnp.int32, logits.shape, 1)
    vals, idxs = [], []
    for _ in range(TOP_K):
        m = jnp.max(logits, axis=-1, keepdims=True)
        idx = jnp.min(jnp.where(logits == m, lane, LANES), axis=-1, keepdims=True)
        vals.append(m)
        idxs.append(idx)
        logits = jnp.where(lane == idx, -jnp.inf, logits)
    exps = [jnp.exp(v - vals[0]) for v in vals]
    denom = exps[0] + exps[1] + exps[2] + exps[3]
    idx_out = jnp.zeros(lane.shape, jnp.int32)
    gate_out = jnp.zeros(lane.shape, F32)
    for k in range(TOP_K):
        idx_out = jnp.where(lane == k, idxs[k], idx_out)
        gate_out = jnp.where(lane == k, exps[k] / denom, gate_out)
    idx_ref[...] = idx_out
    gate_ref[...] = gate_out


def _router(x2d, gain, router_w, router_b, tm=512):
    t, d = x2d.shape
    e = router_w.shape[1]
    rw = jnp.zeros((d, LANES), F32).at[:, :e].set(router_w.astype(F32))
    rb = jnp.full((1, LANES), NEG_BIG, F32).at[0, :e].set(router_b.astype(F32))
    return pl.pallas_call(
        _router_kernel,
        grid=(t // tm,),
        in_specs=[
            pl.BlockSpec((tm, d), lambda i: (i, 0)),
            pl.BlockSpec((1, d), lambda i: (0, 0)),
            pl.BlockSpec((d, LANES), lambda i: (0, 0)),
            pl.BlockSpec((1, LANES), lambda i: (0, 0)),
        ],
        out_specs=[
            pl.BlockSpec((tm * ROW_SUBLANES, LANES), lambda i: (i, 0)),
            pl.BlockSpec((tm, LANES), lambda i: (i, 0)),
            pl.BlockSpec((tm, LANES), lambda i: (i, 0)),
        ],
        out_shape=[
            jax.ShapeDtypeStruct((t * ROW_SUBLANES, LANES), jnp.uint32),
            jax.ShapeDtypeStruct((t, LANES), jnp.int32),
            jax.ShapeDtypeStruct((t, LANES), F32),
        ],
        compiler_params=_cparams(("parallel",), 32),
        name="router",
    )(x2d, gain.reshape(1, d).astype(F32), rw, rb)


MOE_SUB = 256
MOE_NSUB = 6
MOE_TF = 256
MOE_ISSUE_ROWS = 64


def _expert_kernel(ge_ref, gs_ref, gn_ref, ng_ref, tot_ref,
                   hp_ref, idx_ref, wgu_ref, bgu_ref, wdn_ref, bdn_ref, ys_ref,
                   idx_smem, xu_ref, yst_ref, xb_ref, acc_ref, wgu_b_ref, wdn_b_ref, zero_ref,
                   sem_idx, sem_in, sem_out, *, n_sub_total):
    g = pl.program_id(0)
    f = pl.program_id(1)
    nf = pl.num_programs(1)
    ng = ng_ref[0]
    active = g < ng
    nsub = gn_ref[g]
    start = gs_ref[g]
    half = xb_ref.shape[2] // 2
    tf = wdn_ref.shape[1]
    sub_words = MOE_SUB * ROW_SUBLANES
    sub_shift = MOE_SUB.bit_length() - 1
    assert MOE_SUB == 1 << sub_shift

    def fetch_idx(gg):
        idx_cp = pltpu.make_async_copy(idx_ref.at[pl.ds(gs_ref[gg], MOE_NSUB)], idx_smem, sem_idx)
        idx_cp.start()
        idx_cp.wait()

    blk_rows = min(MOE_ISSUE_ROWS, MOE_SUB)

    def issue_block(blk, n_rows_group):
        base = blk * blk_rows

        @pl.when(base < n_rows_group)
        def _():
            s = base >> sub_shift
            r0 = base & (MOE_SUB - 1)
            for u in range(blk_rows):
                r = r0 + u
                src = hp_ref.at[pl.ds(pl.multiple_of(idx_smem[s, 0, r] * ROW_SUBLANES, ROW_SUBLANES),
                                      ROW_SUBLANES)]
                dst = xu_ref.at[s, pl.ds(pl.multiple_of(r * ROW_SUBLANES, ROW_SUBLANES), ROW_SUBLANES)]
                pltpu.make_async_copy(src, dst, sem_in.at[s]).start(priority=u % 2)

    def issue_blocks(first, n_rows_group):
        def one(blk, c):
            issue_block(blk, n_rows_group)
            return c

        lax.fori_loop(first, lax.div(n_rows_group + blk_rows - 1, blk_rows), one, 0)

    def wait_rows(s):
        pltpu.make_async_copy(hp_ref.at[pl.ds(0, sub_words)], xu_ref.at[s], sem_in.at[s]).wait()

    def out_copy(s, first_sub):
        r0 = pl.multiple_of((first_sub + s) * sub_words, sub_words)
        return pltpu.make_async_copy(yst_ref.at[s], ys_ref.at[pl.ds(r0, sub_words)], sem_out)

    def wait_out(gg):
        def wait(s, c):
            out_copy(s, gs_ref[gg]).wait()
            return c

        lax.fori_loop(0, gn_ref[gg], wait, 0)

    @pl.when(jnp.logical_and(g == 0, f == 0))
    def _():
        zero_ref[...] = jnp.zeros(zero_ref.shape, zero_ref.dtype)
        fetch_idx(0)
        issue_blocks(0, gn_ref[0] * MOE_SUB)

    @pl.when(jnp.logical_and(active, f == 0))
    def _():
        def unpack(s, c):
            wait_rows(s)
            for ch in range(ROW_SUBLANES):
                lo, hi = _unpack_chunk(xu_ref[s, pl.ds(ch, MOE_SUB, stride=ROW_SUBLANES), :])
                xb_ref[s, :, ch * LANES:(ch + 1) * LANES] = lo.astype(BF16)
                xb_ref[s, :, half + ch * LANES:half + (ch + 1) * LANES] = hi.astype(BF16)
            acc_ref[s] = jnp.broadcast_to(bdn_ref[0], acc_ref.shape[1:])
            return c

        lax.fori_loop(0, nsub, unpack, 0)

        @pl.when(g + 1 < ng)
        def _():
            fetch_idx(g + 1)

    @pl.when(active)
    def _():
        nxt = jnp.minimum(g + 1, pl.num_programs(0) - 1)
        next_rows = jnp.where(g + 1 < ng, gn_ref[nxt] * MOE_SUB, 0)

        def issue_next(s):
            issue_block(f * nsub + s, next_rows)

        wgu_b_ref[...] = wgu_ref[0].astype(BF16)
        hl = LANES // 2
        for m in range(tf // LANES):
            first = wdn_ref[0, m * LANES:m * LANES + hl, :]
            second = wdn_ref[0, m * LANES + hl:(m + 1) * LANES, :]
            lo = lax.bitcast_convert_type(first.astype(BF16).astype(F32), jnp.uint32) >> 16
            hi = lax.bitcast_convert_type(second.astype(BF16).astype(F32), jnp.uint32) & jnp.uint32(0xFFFF0000)
            wdn_b_ref[m * LANES:(m + 1) * LANES, :] = pltpu.bitcast(lo | hi, BF16)
        bgu = bgu_ref[0]
        lane = lax.broadcasted_iota(jnp.int32, (MOE_SUB, LANES), 1)
        even = (lane & 1) == 0

        def gate_up(s):
            gu = _dot(xb_ref[s], wgu_b_ref[...]) + bgu
            gates, ups = [], []
            for m in range(tf // LANES):
                a = gu[:, 2 * m * LANES:(2 * m + 1) * LANES]
                b = gu[:, (2 * m + 1) * LANES:(2 * m + 2) * LANES]
                gates.append(jnp.where(even, a, pltpu.roll(b, 1, 1)))
                ups.append(jnp.where(even, pltpu.roll(a, LANES - 1, 1), b))
            gate = jnp.minimum(jnp.concatenate(gates, axis=1), SWIGLU_LIMIT)
            up = jnp.clip(jnp.concatenate(ups, axis=1), -SWIGLU_LIMIT, SWIGLU_LIMIT)
            return ((up + 1.0) * gate * jax.nn.sigmoid(SWIGLU_ALPHA * gate)).astype(BF16)

        def down(s, act):
            acc_ref[s] += _dot(act, wdn_b_ref[...])

        def body(s, act_prev):
            issue_next(s)
            act = gate_up(s)
            down(s - 1, act_prev)
            return act

        issue_next(0)
        act_last = lax.fori_loop(1, nsub, body, gate_up(0))
        down(nsub - 1, act_last)

        @pl.when(f == nf - 1)
        def _():
            issue_blocks(nf * nsub, next_rows)

    @pl.when(jnp.logical_and(active, f == nf - 1))
    def _():
        @pl.when(g > 0)
        def _():
            wait_out(g - 1)

        def issue(s, c):
            def store(ch, chunk):
                yst_ref[s, pl.ds(ch, MOE_SUB, stride=ROW_SUBLANES), :] = chunk

            _pack_rows(acc_ref[s], store)
            out_copy(s, start).start()
            return c

        lax.fori_loop(0, nsub, issue, 0)

    @pl.when(jnp.logical_and(g == pl.num_programs(0) - 1, f == nf - 1))
    def _():
        wait_out(ng - 1)

        def fill(s, c):
            r0 = pl.multiple_of(s * sub_words, sub_words)
            cp = pltpu.make_async_copy(zero_ref, ys_ref.at[pl.ds(r0, sub_words)], sem_out)
            cp.start()
            cp.wait()
            return c

        lax.fori_loop(tot_ref[0], n_sub_total, fill, 0)


def _experts(hp, row_tok, grp_expert, grp_start, grp_nsub, n_groups, tot_sub, w_gu, b_gu, w_dn, b_dn):
    n_rows = row_tok.shape[0]
    sub_words = MOE_SUB * ROW_SUBLANES
    idx2 = jnp.concatenate([row_tok.reshape(n_rows // MOE_SUB, 1, MOE_SUB),
                            jnp.zeros((MOE_NSUB, 1, MOE_SUB), jnp.int32)], axis=0)
    e, d, f2 = w_gu.shape
    fdim = f2 // 2
    tf = MOE_TF
    nf = fdim // tf
    n_grp = grp_expert.shape[0]

    def wsel(g, f, ge, gs, gn, ng, tot):
        on = g < ng[0]
        last = jnp.maximum(ng[0] - 1, 0)
        return jnp.where(on, ge[g], ge[last]), jnp.where(on, f, nf - 1)

    def w_gu_map(g, f, *pre):
        ee, ff = wsel(g, f, *pre)
        return (ee, 0, ff)

    def w_dn_map(g, f, *pre):
        ee, ff = wsel(g, f, *pre)
        return (ee, ff, 0)

    def b_dn_map(g, f, *pre):
        ee, _ = wsel(g, f, *pre)
        return (ee, 0, 0)

    grid_spec = pltpu.PrefetchScalarGridSpec(
        num_scalar_prefetch=5,
        grid=(n_grp, nf),
        in_specs=[
            pl.BlockSpec(memory_space=pl.ANY),
            pl.BlockSpec(memory_space=pl.ANY),
            pl.BlockSpec((1, d, 2 * tf), w_gu_map),
            pl.BlockSpec((1, 1, 2 * tf), w_gu_map),
            pl.BlockSpec((1, tf, d), w_dn_map),
            pl.BlockSpec((1, 1, d), b_dn_map),
        ],
        out_specs=pl.BlockSpec(memory_space=pl.ANY),
        scratch_shapes=[
            pltpu.SMEM((MOE_NSUB, 1, MOE_SUB), jnp.int32),
            pltpu.VMEM((MOE_NSUB, sub_words, LANES), jnp.uint32),
            pltpu.VMEM((MOE_NSUB, sub_words, LANES), jnp.uint32),
            pltpu.VMEM((MOE_NSUB, MOE_SUB, d), BF16),
            pltpu.VMEM((MOE_NSUB, MOE_SUB, d), F32),
            pltpu.VMEM((d, 2 * tf), BF16),
            pltpu.VMEM((tf, d), BF16),
            pltpu.VMEM((sub_words, LANES), jnp.uint32),
            pltpu.SemaphoreType.DMA(()),
            pltpu.SemaphoreType.DMA((MOE_NSUB,)),
            pltpu.SemaphoreType.DMA(()),
        ],
    )
    return pl.pallas_call(
        functools.partial(_expert_kernel, n_sub_total=n_rows // MOE_SUB),
        grid_spec=grid_spec,
        out_shape=jax.ShapeDtypeStruct((n_rows * ROW_SUBLANES, LANES), jnp.uint32),
        compiler_params=_cparams(("arbitrary", "arbitrary"), 56),
        name="moe_experts",
    )(grp_expert, grp_start, grp_nsub, n_groups, tot_sub, hp, idx2,
      w_gu, b_gu.reshape(e, 1, f2).astype(F32), w_dn, b_dn.reshape(e, 1, d).astype(F32))


def _combine_ple_kernel(pr_ref, prn_ref, gate_ref, x_ref, ys_ref, ys2_ref, p_ref, gn_ref, wg_ref, wu_ref, gp_ref,
                        o_ref, buf_ref, x2_ref, sem, *, tc):
    i = pl.program_id(0)
    slot = i & 1
    n_pairs = tc * TOP_K
    tile_words = tc * ROW_SUBLANES

    def issue(idx_ref, sl):
        def pair(p2, carry):
            for u in range(2):
                p = 2 * p2 + u
                tok = p >> 2
                pltpu.make_async_copy(
                    ys_ref.at[idx_ref[0, 0, p]],
                    buf_ref.at[sl, p & (TOP_K - 1),
                               pl.ds(pl.multiple_of(tok * ROW_SUBLANES, ROW_SUBLANES), ROW_SUBLANES)],
                    sem.at[sl]).start(priority=u)
            return carry

        lax.fori_loop(0, n_pairs // 2, pair, 0, unroll=4)

    @pl.when(i == 0)
    def _():
        issue(pr_ref, 0)

    @pl.when(i + 1 < pl.num_programs(0))
    def _():
        issue(prn_ref, 1 - slot)

    for k in range(TOP_K):
        pltpu.make_async_copy(ys2_ref.at[pl.ds(0, tile_words)], buf_ref.at[slot, k], sem.at[slot]).wait()

    gates = gate_ref[...]
    half = x_ref.shape[1] // 2
    for ch in range(ROW_SUBLANES):
        lo_cols = slice(ch * LANES, (ch + 1) * LANES)
        hi_cols = slice(half + ch * LANES, half + (ch + 1) * LANES)
        acc_lo = x_ref[:, lo_cols]
        acc_hi = x_ref[:, hi_cols]
        for k in range(TOP_K):
            lo, hi = _unpack_chunk(buf_ref[slot, k, pl.ds(ch, tc, stride=ROW_SUBLANES), :])
            acc_lo = acc_lo + gates[:, k:k + 1] * lo
            acc_hi = acc_hi + gates[:, k:k + 1] * hi
        x2_ref[:, lo_cols] = acc_lo
        x2_ref[:, hi_cols] = acc_hi

    x = x2_ref[...]
    ms = jnp.mean(x * x, axis=-1, keepdims=True)
    h = (x * lax.rsqrt(ms + RMS_EPS) * gn_ref[...]).astype(BF16)
    gate = jax.nn.sigmoid(_dot(h, wg_ref[...]))
    up = _dot(p_ref[...].astype(BF16), wu_ref[...])
    ms_u = jnp.mean(up * up, axis=-1, keepdims=True)
    o_ref[...] = x + up * lax.rsqrt(ms_u + RMS_EPS) * gp_ref[...] * gate


def _combine_ple(x2d, gates, pair_row, ys, p2d, g_norm, w_gate, w_up, g_post, tc=256):
    t, d = x2d.shape
    pd = p2d.shape[1]
    nt = t // tc
    n_rows = ys.shape[0] // ROW_SUBLANES
    pr = pair_row.reshape(nt, 1, tc * TOP_K)
    const = lambda i: (0, 0)
    return pl.pallas_call(
        functools.partial(_combine_ple_kernel, tc=tc),
        grid=(nt,),
        in_specs=[
            pl.BlockSpec((1, 1, tc * TOP_K), lambda i: (i, 0, 0), memory_space=pltpu.SMEM),
            pl.BlockSpec((1, 1, tc * TOP_K), lambda i: (jnp.minimum(i + 1, nt - 1), 0, 0),
                         memory_space=pltpu.SMEM),
            pl.BlockSpec((tc, LANES), lambda i: (i, 0)),
            pl.BlockSpec((tc, d), lambda i: (i, 0)),
            pl.BlockSpec(memory_space=pl.ANY),
            pl.BlockSpec(memory_space=pl.ANY),
            pl.BlockSpec((tc, pd), lambda i: (i, 0)),
            pl.BlockSpec((1, d), const),
            pl.BlockSpec((d, d), const),
            pl.BlockSpec((pd, d), const),
            pl.BlockSpec((1, d), const),
        ],
        out_specs=pl.BlockSpec((tc, d), lambda i: (i, 0)),
        out_shape=jax.ShapeDtypeStruct((t, d), F32),
        scratch_shapes=[
            pltpu.VMEM((2, TOP_K, tc * ROW_SUBLANES, LANES), jnp.uint32),
            pltpu.VMEM((tc, d), F32),
            pltpu.SemaphoreType.DMA((2,)),
        ],
        compiler_params=_cparams(("arbitrary",), 48),
        name="moe_combine_ple",
    )(pr, pr, gates, x2d, ys.reshape(n_rows, ROW_SUBLANES, LANES), ys, p2d,
      g_norm.reshape(1, d).astype(F32), w_gate, w_up, g_post.reshape(1, d).astype(F32))


def _moe(x2d, gain, router_w, router_b, w_gu, b_gu, w_dn, b_dn):
    t, d = x2d.shape
    e = router_w.shape[1]
    hp, idx, gates = _router(x2d, gain, router_w, router_b)

    n_pairs = t * TOP_K
    flat_e = idx[:, :TOP_K].reshape(n_pairs)
    onehot = (flat_e[:, None] == jnp.arange(e, dtype=jnp.int32)[None, :]).astype(jnp.int32)
    csum = jnp.cumsum(onehot, axis=0)
    rank = jnp.sum(onehot * csum, axis=1) - 1
    counts = csum[-1]
    sub_e = (counts + MOE_SUB - 1) // MOE_SUB
    sub_end = jnp.cumsum(sub_e)
    sub_start = sub_end - sub_e
    dest = (jnp.sum(onehot * (sub_start * MOE_SUB)[None, :], axis=1) + rank).astype(jnp.int32)
    n_sub_total = n_pairs // MOE_SUB + e
    n_rows = n_sub_total * MOE_SUB
    row_tok = jnp.zeros((n_rows,), jnp.int32).at[dest].set(jnp.arange(n_pairs, dtype=jnp.int32) // TOP_K)

    grp_e = (sub_e + MOE_NSUB - 1) // MOE_NSUB
    grp_end = jnp.cumsum(grp_e)
    n_grp_max = (n_sub_total + MOE_NSUB - 1) // MOE_NSUB + e
    gidx = jnp.arange(n_grp_max, dtype=jnp.int32)
    g_exp = jnp.minimum(jnp.searchsorted(grp_end, gidx, side="right"), e - 1).astype(jnp.int32)
    g_local = gidx - (grp_end - grp_e)[g_exp]
    g_on = gidx < grp_end[-1]
    g_start = jnp.where(g_on, sub_start[g_exp] + g_local * MOE_NSUB, 0).astype(jnp.int32)
    g_nsub = jnp.where(g_on, jnp.minimum(MOE_NSUB, sub_e[g_exp] - g_local * MOE_NSUB), 0).astype(jnp.int32)

    ys = _experts(hp, row_tok, g_exp, g_start, g_nsub,
                  grp_end[-1].astype(jnp.int32).reshape(1), sub_end[-1].astype(jnp.int32).reshape(1),
                  w_gu, b_gu, w_dn, b_dn)
    return gates, dest, ys


def kernel(x, p, w_in, da_q_norm, da_k_norm, da_lambda_q1, da_lambda_k1, da_lambda_q2, da_lambda_k2, da_subln, gla_gate_w2, gla_gate_b, gla_out_norm, w_branch_da, w_branch_gla, w_merge_gate, b_merge_gate, w_out, norm_mix, norm_ffn, router_w, router_b, w_gate_up, b_gate_up, w_down, b_down, norm_ple, w_ple_gate, w_ple_up, norm_ple_post):
    bsz, s_len, d = x.shape
    t = bsz * s_len
    depth = w_in.shape[0]
    qk_w = DA_HEADS * 2 * DA_HEAD_DIM
    v_w = DA_HEADS * DA_V_DIM
    gk_w = gla_gate_w2.shape[2]
    gv_w = w_branch_gla.shape[1]
    rest_w = 2 * gk_w + 2 * gv_w
    x2d = x.reshape(t, d)
    for i in range(depth):
        lambda_init = 0.8 - 0.6 * math.exp(-0.3 * i)
        h = _rmsnorm(x2d, norm_mix[i])
        w = w_in[i]
        c_v = 2 * qk_w
        c_rest = c_v + v_w
        qn = _proj(h, w_in, i, 0, qk_w, out_dtype=BF16, tm=1024, tn=1024,
                   gain=da_q_norm[i], scale=DA_HEAD_DIM ** -0.5 * LOG2E)
        kn = _proj(h, w_in, i, qk_w, qk_w, out_dtype=BF16, tm=1024, tn=1024, gain=da_k_norm[i])
        vt = _proj_t(h, w_in, i, c_v, v_w, tm=DA_TQ, tn=1024)
        rest = _proj(h, w_in, i, c_rest, rest_w, out_dtype=BF16, tm=1024, tn=1024)
        w_lr = jnp.zeros((1, d, LANES), F32).at[0, :, :GLA_GATE_RANK].set(
            w_in[i, :, c_rest + rest_w:].astype(F32))
        glr = _proj(h, w_lr, 0, 0, LANES, out_dtype=F32, tm=512, tn=LANES)

        lam_vecs = jnp.stack([da_lambda_q1[i], da_lambda_k1[i], da_lambda_q2[i], da_lambda_k2[i]])
        y_da = _diff_attention(qn, kn, vt, lam_vecs, da_subln[i], bsz, s_len, lambda_init, DA_TQ)

        w2p = jnp.zeros((LANES, gk_w), F32).at[:GLA_GATE_RANK].set(gla_gate_w2[i].astype(F32))
        y_gla = _gla(rest, (0, gk_w, 2 * gk_w, 2 * gk_w + gv_w), glr, w2p,
                     gla_gate_b[i], gla_out_norm[i], bsz, s_len)

        mixed = _merge(h, y_da, y_gla, w_merge_gate[i].astype(BF16), b_merge_gate[i].reshape(1, 2 * d).astype(F32),
                       w_branch_da[i].astype(BF16), w_branch_gla[i].astype(BF16))
        x2d = _out_proj(x2d, mixed, w_out[i].astype(BF16))

        gates, pair_row, ys = _moe(x2d, norm_ffn[i], router_w[i], router_b[i], w_gate_up[i], b_gate_up[i],
                                   w_down[i], b_down[i])
        x2d = _combine_ple(x2d, gates, pair_row, ys, p[i].reshape(t, p.shape[-1]), norm_ple[i],
                           w_ple_gate[i].astype(BF16), w_ple_up[i].astype(BF16), norm_ple_post[i])
    return x2d.reshape(bsz, s_len, d)
```

```python
import functools
import math

import jax
import jax.numpy as jnp
from jax import lax
from jax.experimental import pallas as pl
from jax.experimental.pallas import tpu as pltpu

F32 = jnp.float32
BF16 = jnp.bfloat16
HIGHEST = lax.Precision.HIGHEST

CHUNK = 64
CHUNK_SHIFT = CHUNK.bit_length() - 1
RMS_EPS = 1e-6
DA_HEADS = 8
DA_HEAD_DIM = 128
DA_V_DIM = 2 * DA_HEAD_DIM
GLA_HEADS = 4
GLA_GATE_RANK = 16
GLA_TAU = 16.0
N_EXPERTS = 32
TOP_K = 4
SWIGLU_LIMIT = 7.0
SWIGLU_ALPHA = 1.702

DA_TQ = 1024
DA_HPS = 2
DA_ONES_ROWS = 16
LOG2E = 1.4426950408889634

LANES = 128
NEG_BIG = -1e30

MIB = 1024 * 1024


def _cparams(sem, vmem_mib, flags=None):
    return pltpu.CompilerParams(dimension_semantics=sem, vmem_limit_bytes=vmem_mib * MIB, flags=flags)


def _dot(a, b):
    return jnp.dot(a, b, preferred_element_type=F32)


def _dot_nt(a, b):
    return lax.dot_general(a, b, (((1,), (1,)), ((), ())), preferred_element_type=F32)


def _dot_tn(a, b):
    return lax.dot_general(a, b, (((0,), (0,)), ((), ())), preferred_element_type=F32)


def _rmsnorm_kernel(x_ref, g_ref, o_ref):
    x = x_ref[...]
    ms = jnp.mean(x * x, axis=-1, keepdims=True)
    o_ref[...] = (x * lax.rsqrt(ms + RMS_EPS) * g_ref[...]).astype(o_ref.dtype)


def _rmsnorm(x2d, gain, tm=1024):
    t, d = x2d.shape
    return pl.pallas_call(
        _rmsnorm_kernel,
        grid=(t // tm,),
        in_specs=[pl.BlockSpec((tm, d), lambda i: (i, 0)), pl.BlockSpec((1, d), lambda i: (0, 0))],
        out_specs=pl.BlockSpec((tm, d), lambda i: (i, 0)),
        out_shape=jax.ShapeDtypeStruct((t, d), BF16),
        compiler_params=_cparams(("parallel",), 32),
        name="rmsnorm",
    )(x2d, gain.reshape(1, d).astype(F32))


def _proj_kernel(a_ref, w_ref, g_ref, o_ref, wb_ref, *, group_norm, scale):
    @pl.when(pl.program_id(1) == 0)
    def _():
        wb_ref[...] = w_ref[...].astype(BF16)

    acc = _dot(a_ref[...], wb_ref[...])
    if group_norm:
        for c in range(acc.shape[1] // DA_HEAD_DIM):
            blk = acc[:, c * DA_HEAD_DIM:(c + 1) * DA_HEAD_DIM]
            ms = jnp.mean(blk * blk, axis=-1, keepdims=True)
            y = blk * lax.rsqrt(ms + RMS_EPS) * g_ref[...] * scale
            o_ref[:, c * DA_HEAD_DIM:(c + 1) * DA_HEAD_DIM] = y.astype(o_ref.dtype)
    else:
        o_ref[...] = acc.astype(o_ref.dtype)


def _proj(a, w, layer, col0, n, *, out_dtype, tm, tn, gain=None, scale=1.0):
    t, k = a.shape
    assert col0 % tn == 0 and n % tn == 0
    cb = col0 // tn
    group_norm = gain is not None
    g = (gain if group_norm else jnp.ones((DA_HEAD_DIM,), F32)).reshape(1, DA_HEAD_DIM).astype(F32)
    return pl.pallas_call(
        functools.partial(_proj_kernel, group_norm=group_norm, scale=scale),
        grid=(n // tn, t // tm),
        in_specs=[
            pl.BlockSpec((tm, k), lambda j, i: (i, 0)),
            pl.BlockSpec((None, k, tn), lambda j, i: (layer, 0, cb + j)),
            pl.BlockSpec((1, DA_HEAD_DIM), lambda j, i: (0, 0)),
        ],
        out_specs=pl.BlockSpec((tm, tn), lambda j, i: (i, j)),
        out_shape=jax.ShapeDtypeStruct((t, n), out_dtype),
        scratch_shapes=[pltpu.VMEM((k, tn), BF16)],
        compiler_params=_cparams(("parallel", "arbitrary"), 48),
        name="proj",
    )(a, w.astype(F32), g)


def _proj_t_kernel(w_ref, a_ref, o_ref, wt_ref):
    @pl.when(pl.program_id(1) == 0)
    def _():
        wt_ref[...] = w_ref[...].T.astype(BF16)

    o_ref[0] = _dot_nt(wt_ref[...], a_ref[...]).astype(o_ref.dtype)


def _proj_t(a, w, layer, col0, n, *, tm, tn):
    t, k = a.shape
    assert col0 % tn == 0 and n % tn == 0
    cb = col0 // tn
    return pl.pallas_call(
        _proj_t_kernel,
        grid=(n // tn, t // tm),
        in_specs=[
            pl.BlockSpec((None, k, tn), lambda j, i: (layer, 0, cb + j)),
            pl.BlockSpec((tm, k), lambda j, i: (i, 0)),
        ],
        out_specs=pl.BlockSpec((1, tn, tm), lambda j, i: (i, j, 0)),
        out_shape=jax.ShapeDtypeStruct((t // tm, n, tm), BF16),
        scratch_shapes=[pltpu.VMEM((tn, k), BF16)],
        compiler_params=_cparams(("parallel", "arbitrary"), 48),
        name="proj_t",
    )(w.astype(F32), a)


def _da_kernel(slope_ref, q_ref, k_ref, vt_ref, lamv_ref, subg_ref, o_ref,
               kaug_ref, qaug_ref, vaug_ref, m_ref, acc_ref, *, tq, s_len, lambda_init):
    hp = pl.program_id(1)
    i = pl.program_id(2)
    hd = DA_HEAD_DIM
    dv = DA_V_DIM
    nq = vt_ref.shape[0]
    chains = [(hh, c) for hh in range(DA_HPS) for c in range(2)]
    slopes2 = [slope_ref[hp * DA_HPS + hh] * LOG2E for hh in range(DA_HPS)]

    @pl.when(i == 0)
    def _():
        pos = lax.broadcasted_iota(jnp.int32, (s_len, hd), 0)
        lane = lax.broadcasted_iota(jnp.int32, (s_len, hd), 1)
        piece = jnp.where(lane < 6, jnp.where((lane & 1) == 0, (pos >> 4) << 4, pos & 15), 0)
        piece = piece.astype(F32).astype(BF16)
        for hh, c in chains:
            col0 = hh * dv + c * hd
            kaug_ref[hh * 2 + c, :, :hd] = k_ref[:, col0:col0 + hd]
            kaug_ref[hh * 2 + c, :, hd:] = piece
        for hh in range(DA_HPS):
            for j in range(nq):
                vaug_ref[hh, j, :dv, :] = vt_ref[j, hh * dv:(hh + 1) * dv, :]
                vaug_ref[hh, j, dv:, :] = jnp.ones((vaug_ref.shape[2] - dv, tq), BF16)

    lane_q = lax.broadcasted_iota(jnp.int32, (tq, hd), 1)
    for hh in range(DA_HPS):
        s_full = jnp.full((tq, hd), slopes2[hh], F32)
        s1 = s_full.astype(BF16).astype(F32)
        r1 = s_full - s1
        s2 = r1.astype(BF16).astype(F32)
        s3 = r1 - s2
        slope_cols = jnp.where(lane_q < 2, s1, jnp.where(lane_q < 4, s2, jnp.where(lane_q < 6, s3, 0.0)))
        slope_cols = slope_cols.astype(BF16)
        for c in range(2):
            col0 = hh * dv + c * hd
            qaug_ref[hh * 2 + c, :, :hd] = q_ref[:, col0:col0 + hd]
            qaug_ref[hh * 2 + c, :, hd:] = slope_cols

    m_ref[...] = jnp.full(m_ref.shape, NEG_BIG, F32)
    acc_ref[...] = jnp.zeros(acc_ref.shape, F32)

    def update(ci, st, va, qs=slice(None)):
        m_old = m_ref[ci, :, qs]
        m_new = jnp.maximum(m_old, jnp.max(st, axis=0, keepdims=True))
        alpha = jnp.exp2(m_old - m_new)
        p = jnp.exp2(st - m_new)
        acc_ref[ci, :, qs] = alpha * acc_ref[ci, :, qs] + _dot(va, p.astype(BF16))
        m_ref[ci, :, qs] = m_new

    def past_tile(j, carry):
        r0 = pl.multiple_of(j * tq, tq)
        for hh, c in chains:
            ci = hh * 2 + c
            update(ci, _dot_nt(kaug_ref[ci, pl.ds(r0, tq), :], qaug_ref[ci]), vaug_ref[hh, j])
        return carry

    lax.fori_loop(0, i, past_tile, 0)

    hk = tq // 2
    r0 = pl.multiple_of(i * tq, tq)

    def diag_part(k0, q0):
        nq_part = tq - q0
        krow = k0 + lax.broadcasted_iota(jnp.int32, (hk, nq_part), 0)
        qcol = q0 + lax.broadcasted_iota(jnp.int32, (hk, nq_part), 1)
        ahead = jnp.maximum(krow - qcol, 0).astype(F32)
        allowed = (krow >> CHUNK_SHIFT) <= (qcol >> CHUNK_SHIFT)
        qs = slice(q0, tq)
        for hh, c in chains:
            ci = hh * 2 + c
            st = _dot_nt(kaug_ref[ci, pl.ds(pl.multiple_of(r0 + k0, hk), hk), :], qaug_ref[ci, qs, :])
            st = jnp.where(allowed, st + (-2.0 * slopes2[hh]) * ahead, NEG_BIG)
            update(ci, st, vaug_ref[hh, i, :, k0:k0 + hk], qs)

    diag_part(0, 0)
    diag_part(hk, hk)

    lamv = lamv_ref[...]
    lam = (jnp.exp(jnp.sum(lamv[0:1] * lamv[1:2], axis=-1, keepdims=True))
           - jnp.exp(jnp.sum(lamv[2:3] * lamv[3:4], axis=-1, keepdims=True)) + lambda_init)
    for hh in range(DA_HPS):
        a0 = acc_ref[hh * 2]
        a1 = acc_ref[hh * 2 + 1]
        ot = a0[:dv, :] / a0[dv:dv + 1, :] - lam * (a1[:dv, :] / a1[dv:dv + 1, :])
        ms = jnp.mean(ot * ot, axis=0, keepdims=True)
        o = (ot * lax.rsqrt(ms + RMS_EPS)).T
        o_ref[:, hh * dv:(hh + 1) * dv] = (o * subg_ref[...] * (1.0 - lambda_init)).astype(o_ref.dtype)


def _diff_attention(qn, kn, vt, lam_vecs, subln_g, bsz, s_len, lambda_init, tq):
    assert s_len <= 4096 and s_len % tq == 0 and tq % CHUNK == 0 and DA_HEADS % DA_HPS == 0
    t = bsz * s_len
    nq = s_len // tq
    w = DA_HPS * DA_V_DIM
    rows_aug = DA_V_DIM + DA_ONES_ROWS
    slopes = jnp.exp2(-8.0 * jnp.arange(1, DA_HEADS + 1, dtype=F32) / DA_HEADS)
    grid_spec = pltpu.PrefetchScalarGridSpec(
        num_scalar_prefetch=1,
        grid=(bsz, DA_HEADS // DA_HPS, nq),
        in_specs=[
            pl.BlockSpec((tq, w), lambda b, h, i, s: (b * nq + i, h)),
            pl.BlockSpec((s_len, w), lambda b, h, i, s: (b, h)),
            pl.BlockSpec((nq, w, tq), lambda b, h, i, s: (b, h, 0)),
            pl.BlockSpec((4, DA_HEAD_DIM), lambda b, h, i, s: (0, 0)),
            pl.BlockSpec((1, DA_V_DIM), lambda b, h, i, s: (0, 0)),
        ],
        out_specs=pl.BlockSpec((tq, w), lambda b, h, i, s: (b * nq + i, h)),
        scratch_shapes=[
            pltpu.VMEM((2 * DA_HPS, s_len, 2 * DA_HEAD_DIM), BF16),
            pltpu.VMEM((2 * DA_HPS, tq, 2 * DA_HEAD_DIM), BF16),
            pltpu.VMEM((DA_HPS, nq, rows_aug, tq), BF16),
            pltpu.VMEM((2 * DA_HPS, 1, tq), F32),
            pltpu.VMEM((2 * DA_HPS, rows_aug, tq), F32),
        ],
    )
    return pl.pallas_call(
        functools.partial(_da_kernel, tq=tq, s_len=s_len, lambda_init=lambda_init),
        grid_spec=grid_spec,
        out_shape=jax.ShapeDtypeStruct((t, DA_HEADS * DA_V_DIM), BF16),
        compiler_params=_cparams(("parallel", "parallel", "arbitrary"), 56),
        name="diff_attention",
    )(slopes, qn, kn, vt, lam_vecs.astype(F32), subln_g.reshape(1, DA_V_DIM).astype(F32))


def _gla_kernel(q_ref, k_ref, v_ref, g_ref, glr_ref, w2_ref, gb_ref, og_ref, o_ref, state_ref,
                *, tb, dk, dv):
    @pl.when(pl.program_id(0) == 0)
    def _():
        state_ref[...] = jnp.zeros(state_ref.shape, F32)

    bsz = q_ref.shape[0]
    row = lax.broadcasted_iota(jnp.int32, (CHUNK, CHUNK), 0)
    col = lax.broadcasted_iota(jnp.int32, (CHUNK, CHUNK), 1)
    lower = row >= col
    tri = lower.astype(F32)
    mid = CHUNK // 2

    def chunk(c, carry):
        r0 = pl.multiple_of(c * CHUNK, CHUNK)
        rows = pl.ds(r0, CHUNK)
        for bi in range(bsz):
            z = jnp.dot(glr_ref[bi, rows, :], w2_ref[...], precision=HIGHEST,
                        preferred_element_type=F32) + gb_ref[...]
            log_a = (jnp.minimum(z, 0.0) - jnp.log1p(jnp.exp(-jnp.abs(z)))) * (1.0 / GLA_TAU)
            b_all = jnp.dot(tri, log_a, precision=HIGHEST, preferred_element_type=F32)
            for h in range(GLA_HEADS):
                ks = slice(h * dk, (h + 1) * dk)
                vs = slice(h * dv, (h + 1) * dv)
                b = b_all[:, ks]
                b_last = b[CHUNK - 1:CHUNK, :]
                b_mid = b[mid:mid + 1, :]
                q = q_ref[bi, rows, ks].astype(F32) * (dk ** -0.5)
                k = k_ref[bi, rows, ks].astype(F32)
                v = v_ref[bi, rows, vs]
                e_fwd = jnp.exp(b - b_mid)
                e_bwd = jnp.exp(b_mid - b)
                a_lo = _dot_nt((q * e_fwd).astype(BF16), (k * e_bwd).astype(BF16))
                a_up = _dot_nt((q * e_bwd).astype(BF16), (k * e_fwd).astype(BF16))
                attn = jnp.where(lower, a_lo, a_up)
                state = state_ref[bi, h]
                o = _dot(attn.astype(BF16), v) + _dot_nt((q * jnp.exp(b)).astype(BF16), state.astype(BF16))
                kd = (k * jnp.exp(b_last - b)).astype(BF16)
                state_ref[bi, h] = state * jnp.exp(b_last) + _dot_tn(v, kd)
                ms = jnp.mean(o * o, axis=-1, keepdims=True)
                g = g_ref[bi, rows, vs].astype(F32)
                y = o * lax.rsqrt(ms + RMS_EPS) * og_ref[...] * (g * jax.nn.sigmoid(g))
                o_ref[bi, rows, vs] = y.astype(o_ref.dtype)
        return carry

    lax.fori_loop(0, tb // CHUNK, chunk, 0)


def _gla(src, cols, glr, w2p, gate_b, out_g, bsz, s_len, tb=512):
    t = bsz * s_len
    nb = s_len // tb
    kw = w2p.shape[1]
    dk = kw // GLA_HEADS
    dv = out_g.shape[0]
    vw = dv * GLA_HEADS
    cq, ck, cv, cg = cols
    src3 = src.reshape(bsz, s_len, src.shape[1])
    out = pl.pallas_call(
        functools.partial(_gla_kernel, tb=tb, dk=dk, dv=dv),
        grid=(nb,),
        in_specs=[
            pl.BlockSpec((bsz, tb, kw), lambda i: (0, i, cq // kw)),
            pl.BlockSpec((bsz, tb, kw), lambda i: (0, i, ck // kw)),
            pl.BlockSpec((bsz, tb, vw), lambda i: (0, i, cv // vw)),
            pl.BlockSpec((bsz, tb, vw), lambda i: (0, i, cg // vw)),
            pl.BlockSpec((bsz, tb, LANES), lambda i: (0, i, 0)),
            pl.BlockSpec((LANES, kw), lambda i: (0, 0)),
            pl.BlockSpec((1, kw), lambda i: (0, 0)),
            pl.BlockSpec((1, dv), lambda i: (0, 0)),
        ],
        out_specs=pl.BlockSpec((bsz, tb, vw), lambda i: (0, i, 0)),
        out_shape=jax.ShapeDtypeStruct((bsz, s_len, vw), BF16),
        scratch_shapes=[pltpu.VMEM((bsz, GLA_HEADS, dv, dk), F32)],
        compiler_params=_cparams(("arbitrary",), 56),
        name="gla",
    )(src3, src3, src3, src3, glr.reshape(bsz, s_len, LANES), w2p,
      gate_b.reshape(1, kw).astype(F32), out_g.reshape(1, dv).astype(F32))
    return out.reshape(t, vw)


def _merge_kernel(h_ref, ya_ref, yb_ref, wga_ref, wgb_ref, ba_ref, bb_ref, wa_ref, wb_ref, o_ref):
    h = h_ref[...]
    ga = jax.nn.sigmoid(_dot(h, wga_ref[...]) + ba_ref[...])
    gb = jax.nn.sigmoid(_dot(h, wgb_ref[...]) + bb_ref[...])
    mixed = ga * _dot(ya_ref[...], wa_ref[...]) + gb * _dot(yb_ref[...], wb_ref[...])
    o_ref[...] = mixed.astype(o_ref.dtype)


def _merge(h, y_da, y_gla, w_gate, b_gate, w_da, w_gla, tm=512, tn=512):
    t, d = h.shape
    nb = d // tn
    act = lambda: pl.BlockSpec((tm, d), lambda j, i: (i, 0))
    return pl.pallas_call(
        _merge_kernel,
        grid=(nb, t // tm),
        in_specs=[
            act(), act(), act(),
            pl.BlockSpec((d, tn), lambda j, i: (0, j)),
            pl.BlockSpec((d, tn), lambda j, i: (0, nb + j)),
            pl.BlockSpec((1, tn), lambda j, i: (0, j)),
            pl.BlockSpec((1, tn), lambda j, i: (0, nb + j)),
            pl.BlockSpec((d, tn), lambda j, i: (0, j)),
            pl.BlockSpec((d, tn), lambda j, i: (0, j)),
        ],
        out_specs=pl.BlockSpec((tm, tn), lambda j, i: (i, j)),
        out_shape=jax.ShapeDtypeStruct((t, d), BF16),
        compiler_params=_cparams(("parallel", "parallel"), 48),
        name="merge",
    )(h, y_da, y_gla, w_gate, w_gate, b_gate, b_gate, w_da, w_gla)


def _out_proj_kernel(x_ref, m_ref, w_ref, o_ref):
    o_ref[...] = x_ref[...] + _dot(m_ref[...], w_ref[...])


def _out_proj(x2d, mixed, w_out, tm=512):
    t, d = x2d.shape
    return pl.pallas_call(
        _out_proj_kernel,
        grid=(t // tm,),
        in_specs=[
            pl.BlockSpec((tm, d), lambda i: (i, 0)),
            pl.BlockSpec((tm, d), lambda i: (i, 0)),
            pl.BlockSpec((d, d), lambda i: (0, 0)),
        ],
        out_specs=pl.BlockSpec((tm, d), lambda i: (i, 0)),
        out_shape=jax.ShapeDtypeStruct((t, d), F32),
        compiler_params=_cparams(("parallel",), 48),
        name="out_proj",
    )(x2d, mixed, w_out)


ROW_SUBLANES = 8


def _pack_rows(val, store):
    half = val.shape[1] // 2
    assert half == ROW_SUBLANES * LANES
    lo = lax.bitcast_convert_type(val[:, :half].astype(BF16).astype(F32), jnp.uint32)
    hi = lax.bitcast_convert_type(val[:, half:].astype(BF16).astype(F32), jnp.uint32)
    packed = (lo >> 16) | (hi & jnp.uint32(0xFFFF0000))
    for c in range(ROW_SUBLANES):
        store(c, packed[:, c * LANES:(c + 1) * LANES])


def _unpack_chunk(chunk):
    lo = lax.bitcast_convert_type(chunk << 16, F32)
    hi = lax.bitcast_convert_type(chunk & jnp.uint32(0xFFFF0000), F32)
    return lo, hi


def _router_kernel(x_ref, g_ref, rw_ref, rb_ref, hp_ref, idx_ref, gate_ref):
    x = x_ref[...]
    tm = x.shape[0]
    ms = jnp.mean(x * x, axis=-1, keepdims=True)
    h = x * lax.rsqrt(ms + RMS_EPS) * g_ref[...]

    def store(c, chunk):
        hp_ref[pl.ds(c, tm, stride=ROW_SUBLANES), :] = chunk

    _pack_rows(h, store)

    logits = jnp.dot(h, rw_ref[...], precision=HIGHEST, preferred_element_type=F32) + rb_ref[...]
    lane = lax.broadcasted_iota(jnp.int32, logits.shape, 1)
    vals, idxs = [], []
    for _ in range(TOP_K):
        m = jnp.max(logits, axis=-1, keepdims=True)
        idx = jnp.min(jnp.where(logits == m, lane, LANES), axis=-1, keepdims=True)
        vals.append(m)
        idxs.append(idx)
        logits = jnp.where(lane == idx, -jnp.inf, logits)
    exps = [jnp.exp(v - vals[0]) for v in vals]
    denom = exps[0] + exps[1] + exps[2] + exps[3]
    idx_out = jnp.zeros(lane.shape, jnp.int32)
    gate_out = jnp.zeros(lane.shape, F32)
    for k in range(TOP_K):
        idx_out = jnp.where(lane == k, idxs[k], idx_out)
        gate_out = jnp.where(lane == k, exps[k] / denom, gate_out)
    idx_ref[...] = idx_out
    gate_ref[...] = gate_out


def _router(x2d, gain, router_w, router_b, tm=512):
    t, d = x2d.shape
    e = router_w.shape[1]
    rw = jnp.zeros((d, LANES), F32).at[:, :e].set(router_w.astype(F32))
    rb = jnp.full((1, LANES), NEG_BIG, F32).at[0, :e].set(router_b.astype(F32))
    return pl.pallas_call(
        _router_kernel,
        grid=(t // tm,),
        in_specs=[
            pl.BlockSpec((tm, d), lambda i: (i, 0)),
            pl.BlockSpec((1, d), lambda i: (0, 0)),
            pl.BlockSpec((d, LANES), lambda i: (0, 0)),
            pl.BlockSpec((1, LANES), lambda i: (0, 0)),
        ],
        out_specs=[
            pl.BlockSpec((tm * ROW_SUBLANES, LANES), lambda i: (i, 0)),
            pl.BlockSpec((tm, LANES), lambda i: (i, 0)),
            pl.BlockSpec((tm, LANES), lambda i: (i, 0)),
        ],
        out_shape=[
            jax.ShapeDtypeStruct((t * ROW_SUBLANES, LANES), jnp.uint32),
            jax.ShapeDtypeStruct((t, LANES), jnp.int32),
            jax.ShapeDtypeStruct((t, LANES), F32),
        ],
        compiler_params=_cparams(("parallel",), 32),
        name="router",
    )(x2d, gain.reshape(1, d).astype(F32), rw, rb)


MOE_SUB = 256
MOE_NSUB = 6
MOE_TF = 256
MOE_ISSUE_ROWS = 64


def _expert_kernel(ge_ref, gs_ref, gn_ref, ng_ref, tot_ref,
                   hp_ref, idx_ref, wgu_ref, bgu_ref, wdn_ref, bdn_ref, ys_ref,
                   idx_smem, xu_ref, yst_ref, xb_ref, acc_ref, wgu_b_ref, wdn_b_ref, zero_ref,
                   sem_idx, sem_in, sem_out, *, n_sub_total):
    g = pl.program_id(0)
    f = pl.program_id(1)
    nf = pl.num_programs(1)
    ng = ng_ref[0]
    active = g < ng
    nsub = gn_ref[g]
    start = gs_ref[g]
    half = xb_ref.shape[2] // 2
    tf = wdn_ref.shape[1]
    sub_words = MOE_SUB * ROW_SUBLANES
    sub_shift = MOE_SUB.bit_length() - 1
    assert MOE_SUB == 1 << sub_shift

    def fetch_idx(gg):
        idx_cp = pltpu.make_async_copy(idx_ref.at[pl.ds(gs_ref[gg], MOE_NSUB)], idx_smem, sem_idx)
        idx_cp.start()
        idx_cp.wait()

    blk_rows = min(MOE_ISSUE_ROWS, MOE_SUB)

    def issue_block(blk, n_rows_group):
        base = blk * blk_rows

        @pl.when(base < n_rows_group)
        def _():
            s = base >> sub_shift
            r0 = base & (MOE_SUB - 1)
            for u in range(blk_rows):
                r = r0 + u
                src = hp_ref.at[pl.ds(pl.multiple_of(idx_smem[s, 0, r] * ROW_SUBLANES, ROW_SUBLANES),
                                      ROW_SUBLANES)]
                dst = xu_ref.at[s, pl.ds(pl.multiple_of(r * ROW_SUBLANES, ROW_SUBLANES), ROW_SUBLANES)]
                pltpu.make_async_copy(src, dst, sem_in.at[s]).start(priority=u % 2)

    def issue_blocks(first, n_rows_group):
        def one(blk, c):
            issue_block(blk, n_rows_group)
            return c

        lax.fori_loop(first, lax.div(n_rows_group + blk_rows - 1, blk_rows), one, 0)

    def wait_rows(s):
        pltpu.make_async_copy(hp_ref.at[pl.ds(0, sub_words)], xu_ref.at[s], sem_in.at[s]).wait()

    def out_copy(s, first_sub):
        r0 = pl.multiple_of((first_sub + s) * sub_words, sub_words)
        return pltpu.make_async_copy(yst_ref.at[s], ys_ref.at[pl.ds(r0, sub_words)], sem_out)

    def wait_out(gg):
        def wait(s, c):
            out_copy(s, gs_ref[gg]).wait()
            return c

        lax.fori_loop(0, gn_ref[gg], wait, 0)

    @pl.when(jnp.logical_and(g == 0, f == 0))
    def _():
        zero_ref[...] = jnp.zeros(zero_ref.shape, zero_ref.dtype)
        fetch_idx(0)
        issue_blocks(0, gn_ref[0] * MOE_SUB)

    @pl.when(jnp.logical_and(active, f == 0))
    def _():
        def unpack(s, c):
            wait_rows(s)
            for ch in range(ROW_SUBLANES):
                lo, hi = _unpack_chunk(xu_ref[s, pl.ds(ch, MOE_SUB, stride=ROW_SUBLANES), :])
                xb_ref[s, :, ch * LANES:(ch + 1) * LANES] = lo.astype(BF16)
                xb_ref[s, :, half + ch * LANES:half + (ch + 1) * LANES] = hi.astype(BF16)
            acc_ref[s] = jnp.broadcast_to(bdn_ref[0], acc_ref.shape[1:])
            return c

        lax.fori_loop(0, nsub, unpack, 0)

        @pl.when(g + 1 < ng)
        def _():
            fetch_idx(g + 1)

    @pl.when(active)
    def _():
        nxt = jnp.minimum(g + 1, pl.num_programs(0) - 1)
        next_rows = jnp.where(g + 1 < ng, gn_ref[nxt] * MOE_SUB, 0)

        def issue_next(s):
            issue_block(f * nsub + s, next_rows)

        wgu_b_ref[...] = wgu_ref[0].astype(BF16)
        hl = LANES // 2
        for m in range(tf // LANES):
            first = wdn_ref[0, m * LANES:m * LANES + hl, :]
            second = wdn_ref[0, m * LANES + hl:(m + 1) * LANES, :]
            lo = lax.bitcast_convert_type(first.astype(BF16).astype(F32), jnp.uint32) >> 16
            hi = lax.bitcast_convert_type(second.astype(BF16).astype(F32), jnp.uint32) & jnp.uint32(0xFFFF0000)
            wdn_b_ref[m * LANES:(m + 1) * LANES, :] = pltpu.bitcast(lo | hi, BF16)
        bgu = bgu_ref[0]

        def gate_up(s0, n):
            rows = n * MOE_SUB
            even = (lax.broadcasted_iota(jnp.int32, (rows, LANES), 1) & 1) == 0
            x = xb_ref[pl.ds(s0, n)].reshape(rows, xb_ref.shape[2])
            gu = _dot(x, wgu_b_ref[...]) + bgu
            gates, ups = [], []
            for m in range(tf // LANES):
                a = gu[:, 2 * m * LANES:(2 * m + 1) * LANES]
                b = gu[:, (2 * m + 1) * LANES:(2 * m + 2) * LANES]
                gates.append(jnp.where(even, a, pltpu.roll(b, 1, 1)))
                ups.append(jnp.where(even, pltpu.roll(a, LANES - 1, 1), b))
            gate = jnp.minimum(jnp.concatenate(gates, axis=1), SWIGLU_LIMIT)
            up = jnp.clip(jnp.concatenate(ups, axis=1), -SWIGLU_LIMIT, SWIGLU_LIMIT)
            return ((up + 1.0) * gate * jax.nn.sigmoid(SWIGLU_ALPHA * gate)).astype(BF16)

        def down(s0, n, act):
            contrib = _dot(act, wdn_b_ref[...])
            acc_ref[pl.ds(s0, n)] += contrib.reshape(n, MOE_SUB, acc_ref.shape[2])

        npair = nsub >> 1

        @pl.when(npair > 0)
        def _():
            def body(j, act_prev):
                issue_next(2 * j)
                issue_next(2 * j + 1)
                act = gate_up(2 * j, 2)
                down(2 * (j - 1), 2, act_prev)
                return act

            issue_next(0)
            issue_next(1)
            act_last = lax.fori_loop(1, npair, body, gate_up(0, 2))
            down(2 * (npair - 1), 2, act_last)

        @pl.when((nsub & 1) == 1)
        def _():
            issue_next(nsub - 1)
            down(nsub - 1, 1, gate_up(nsub - 1, 1))

        @pl.when(f == nf - 1)
        def _():
            issue_blocks(nf * nsub, next_rows)

    @pl.when(jnp.logical_and(active, f == nf - 1))
    def _():
        @pl.when(g > 0)
        def _():
            wait_out(g - 1)

        def issue(s, c):
            def store(ch, chunk):
                yst_ref[s, pl.ds(ch, MOE_SUB, stride=ROW_SUBLANES), :] = chunk

            _pack_rows(acc_ref[s], store)
            out_copy(s, start).start()
            return c

        lax.fori_loop(0, nsub, issue, 0)

    @pl.when(jnp.logical_and(g == pl.num_programs(0) - 1, f == nf - 1))
    def _():
        wait_out(ng - 1)

        def fill(s, c):
            r0 = pl.multiple_of(s * sub_words, sub_words)
            cp = pltpu.make_async_copy(zero_ref, ys_ref.at[pl.ds(r0, sub_words)], sem_out)
            cp.start()
            cp.wait()
            return c

        lax.fori_loop(tot_ref[0], n_sub_total, fill, 0)


def _experts(hp, row_tok, grp_expert, grp_start, grp_nsub, n_groups, tot_sub, w_gu, b_gu, w_dn, b_dn):
    n_rows = row_tok.shape[0]
    sub_words = MOE_SUB * ROW_SUBLANES
    idx2 = jnp.concatenate([row_tok.reshape(n_rows // MOE_SUB, 1, MOE_SUB),
                            jnp.zeros((MOE_NSUB, 1, MOE_SUB), jnp.int32)], axis=0)
    e, d, f2 = w_gu.shape
    fdim = f2 // 2
    tf = MOE_TF
    nf = fdim // tf
    n_grp = grp_expert.shape[0]

    def wsel(g, f, ge, gs, gn, ng, tot):
        on = g < ng[0]
        last = jnp.maximum(ng[0] - 1, 0)
        return jnp.where(on, ge[g], ge[last]), jnp.where(on, f, nf - 1)

    def w_gu_map(g, f, *pre):
        ee, ff = wsel(g, f, *pre)
        return (ee, 0, ff)

    def w_dn_map(g, f, *pre):
        ee, ff = wsel(g, f, *pre)
        return (ee, ff, 0)

    def b_dn_map(g, f, *pre):
        ee, _ = wsel(g, f, *pre)
        return (ee, 0, 0)

    grid_spec = pltpu.PrefetchScalarGridSpec(
        num_scalar_prefetch=5,
        grid=(n_grp, nf),
        in_specs=[
            pl.BlockSpec(memory_space=pl.ANY),
            pl.BlockSpec(memory_space=pl.ANY),
            pl.BlockSpec((1, d, 2 * tf), w_gu_map),
            pl.BlockSpec((1, 1, 2 * tf), w_gu_map),
            pl.BlockSpec((1, tf, d), w_dn_map),
            pl.BlockSpec((1, 1, d), b_dn_map),
        ],
        out_specs=pl.BlockSpec(memory_space=pl.ANY),
        scratch_shapes=[
            pltpu.SMEM((MOE_NSUB, 1, MOE_SUB), jnp.int32),
            pltpu.VMEM((MOE_NSUB, sub_words, LANES), jnp.uint32),
            pltpu.VMEM((MOE_NSUB, sub_words, LANES), jnp.uint32),
            pltpu.VMEM((MOE_NSUB, MOE_SUB, d), BF16),
            pltpu.VMEM((MOE_NSUB, MOE_SUB, d), F32),
            pltpu.VMEM((d, 2 * tf), BF16),
            pltpu.VMEM((tf, d), BF16),
            pltpu.VMEM((sub_words, LANES), jnp.uint32),
            pltpu.SemaphoreType.DMA(()),
            pltpu.SemaphoreType.DMA((MOE_NSUB,)),
            pltpu.SemaphoreType.DMA(()),
        ],
    )
    return pl.pallas_call(
        functools.partial(_expert_kernel, n_sub_total=n_rows // MOE_SUB),
        grid_spec=grid_spec,
        out_shape=jax.ShapeDtypeStruct((n_rows * ROW_SUBLANES, LANES), jnp.uint32),
        compiler_params=_cparams(("arbitrary", "arbitrary"), 56),
        name="moe_experts",
    )(grp_expert, grp_start, grp_nsub, n_groups, tot_sub, hp, idx2,
      w_gu, b_gu.reshape(e, 1, f2).astype(F32), w_dn, b_dn.reshape(e, 1, d).astype(F32))


def _combine_ple_kernel(pr_ref, prn_ref, gate_ref, x_ref, ys_ref, ys2_ref, p_ref, gn_ref, wg_ref, wu_ref, gp_ref,
                        o_ref, buf_ref, x2_ref, sem, *, tc):
    i = pl.program_id(0)
    slot = i & 1
    n_pairs = tc * TOP_K
    tile_words = tc * ROW_SUBLANES

    def issue(idx_ref, sl):
        def pair(p2, carry):
            for u in range(2):
                p = 2 * p2 + u
                tok = p >> 2
                pltpu.make_async_copy(
                    ys_ref.at[idx_ref[0, 0, p]],
                    buf_ref.at[sl, p & (TOP_K - 1),
                               pl.ds(pl.multiple_of(tok * ROW_SUBLANES, ROW_SUBLANES), ROW_SUBLANES)],
                    sem.at[sl]).start(priority=u)
            return carry

        lax.fori_loop(0, n_pairs // 2, pair, 0, unroll=4)

    @pl.when(i == 0)
    def _():
        issue(pr_ref, 0)

    @pl.when(i + 1 < pl.num_programs(0))
    def _():
        issue(prn_ref, 1 - slot)

    for k in range(TOP_K):
        pltpu.make_async_copy(ys2_ref.at[pl.ds(0, tile_words)], buf_ref.at[slot, k], sem.at[slot]).wait()

    gates = gate_ref[...]
    half = x_ref.shape[1] // 2
    for ch in range(ROW_SUBLANES):
        lo_cols = slice(ch * LANES, (ch + 1) * LANES)
        hi_cols = slice(half + ch * LANES, half + (ch + 1) * LANES)
        acc_lo = x_ref[:, lo_cols]
        acc_hi = x_ref[:, hi_cols]
        for k in range(TOP_K):
            lo, hi = _unpack_chunk(buf_ref[slot, k, pl.ds(ch, tc, stride=ROW_SUBLANES), :])
            acc_lo = acc_lo + gates[:, k:k + 1] * lo
            acc_hi = acc_hi + gates[:, k:k + 1] * hi
        x2_ref[:, lo_cols] = acc_lo
        x2_ref[:, hi_cols] = acc_hi

    x = x2_ref[...]
    ms = jnp.mean(x * x, axis=-1, keepdims=True)
    h = (x * lax.rsqrt(ms + RMS_EPS) * gn_ref[...]).astype(BF16)
    gate = jax.nn.sigmoid(_dot(h, wg_ref[...]))
    up = _dot(p_ref[...].astype(BF16), wu_ref[...])
    ms_u = jnp.mean(up * up, axis=-1, keepdims=True)
    o_ref[...] = x + up * lax.rsqrt(ms_u + RMS_EPS) * gp_ref[...] * gate


def _combine_ple(x2d, gates, pair_row, ys, p2d, g_norm, w_gate, w_up, g_post, tc=256):
    t, d = x2d.shape
    pd = p2d.shape[1]
    nt = t // tc
    n_rows = ys.shape[0] // ROW_SUBLANES
    pr = pair_row.reshape(nt, 1, tc * TOP_K)
    const = lambda i: (0, 0)
    return pl.pallas_call(
        functools.partial(_combine_ple_kernel, tc=tc),
        grid=(nt,),
        in_specs=[
            pl.BlockSpec((1, 1, tc * TOP_K), lambda i: (i, 0, 0), memory_space=pltpu.SMEM),
            pl.BlockSpec((1, 1, tc * TOP_K), lambda i: (jnp.minimum(i + 1, nt - 1), 0, 0),
                         memory_space=pltpu.SMEM),
            pl.BlockSpec((tc, LANES), lambda i: (i, 0)),
            pl.BlockSpec((tc, d), lambda i: (i, 0)),
            pl.BlockSpec(memory_space=pl.ANY),
            pl.BlockSpec(memory_space=pl.ANY),
            pl.BlockSpec((tc, pd), lambda i: (i, 0)),
            pl.BlockSpec((1, d), const),
            pl.BlockSpec((d, d), const),
            pl.BlockSpec((pd, d), const),
            pl.BlockSpec((1, d), const),
        ],
        out_specs=pl.BlockSpec((tc, d), lambda i: (i, 0)),
        out_shape=jax.ShapeDtypeStruct((t, d), F32),
        scratch_shapes=[
            pltpu.VMEM((2, TOP_K, tc * ROW_SUBLANES, LANES), jnp.uint32),
            pltpu.VMEM((tc, d), F32),
            pltpu.SemaphoreType.DMA((2,)),
        ],
        compiler_params=_cparams(("arbitrary",), 48),
        name="moe_combine_ple",
    )(pr, pr, gates, x2d, ys.reshape(n_rows, ROW_SUBLANES, LANES), ys, p2d,
      g_norm.reshape(1, d).astype(F32), w_gate, w_up, g_post.reshape(1, d).astype(F32))


def _moe(x2d, gain, router_w, router_b, w_gu, b_gu, w_dn, b_dn):
    t, d = x2d.shape
    e = router_w.shape[1]
    hp, idx, gates = _router(x2d, gain, router_w, router_b)

    n_pairs = t * TOP_K
    flat_e = idx[:, :TOP_K].reshape(n_pairs)
    onehot = (flat_e[:, None] == jnp.arange(e, dtype=jnp.int32)[None, :]).astype(jnp.int32)
    csum = jnp.cumsum(onehot, axis=0)
    rank = jnp.sum(onehot * csum, axis=1) - 1
    counts = csum[-1]
    sub_e = (counts + MOE_SUB - 1) // MOE_SUB
    sub_end = jnp.cumsum(sub_e)
    sub_start = sub_end - sub_e
    dest = (jnp.sum(onehot * (sub_start * MOE_SUB)[None, :], axis=1) + rank).astype(jnp.int32)
    n_sub_total = n_pairs // MOE_SUB + e
    n_rows = n_sub_total * MOE_SUB
    row_tok = jnp.zeros((n_rows,), jnp.int32).at[dest].set(jnp.arange(n_pairs, dtype=jnp.int32) // TOP_K)

    grp_e = (sub_e + MOE_NSUB - 1) // MOE_NSUB
    grp_end = jnp.cumsum(grp_e)
    n_grp_max = (n_sub_total + MOE_NSUB - 1) // MOE_NSUB + e
    gidx = jnp.arange(n_grp_max, dtype=jnp.int32)
    g_exp = jnp.minimum(jnp.searchsorted(grp_end, gidx, side="right"), e - 1).astype(jnp.int32)
    g_local = gidx - (grp_end - grp_e)[g_exp]
    g_on = gidx < grp_end[-1]
    g_start = jnp.where(g_on, sub_start[g_exp] + g_local * MOE_NSUB, 0).astype(jnp.int32)
    g_nsub = jnp.where(g_on, jnp.minimum(MOE_NSUB, sub_e[g_exp] - g_local * MOE_NSUB), 0).astype(jnp.int32)

    ys = _experts(hp, row_tok, g_exp, g_start, g_nsub,
                  grp_end[-1].astype(jnp.int32).reshape(1), sub_end[-1].astype(jnp.int32).reshape(1),
                  w_gu, b_gu, w_dn, b_dn)
    return gates, dest, ys


def kernel(x, p, w_in, da_q_norm, da_k_norm, da_lambda_q1, da_lambda_k1, da_lambda_q2, da_lambda_k2, da_subln, gla_gate_w2, gla_gate_b, gla_out_norm, w_branch_da, w_branch_gla, w_merge_gate, b_merge_gate, w_out, norm_mix, norm_ffn, router_w, router_b, w_gate_up, b_gate_up, w_down, b_down, norm_ple, w_ple_gate, w_ple_up, norm_ple_post):
    bsz, s_len, d = x.shape
    t = bsz * s_len
    depth = w_in.shape[0]
    qk_w = DA_HEADS * 2 * DA_HEAD_DIM
    v_w = DA_HEADS * DA_V_DIM
    gk_w = gla_gate_w2.shape[2]
    gv_w = w_branch_gla.shape[1]
    rest_w = 2 * gk_w + 2 * gv_w
    x2d = x.reshape(t, d)
    for i in range(depth):
        lambda_init = 0.8 - 0.6 * math.exp(-0.3 * i)
        h = _rmsnorm(x2d, norm_mix[i])
        w = w_in[i]
        c_v = 2 * qk_w
        c_rest = c_v + v_w
        qn = _proj(h, w_in, i, 0, qk_w, out_dtype=BF16, tm=1024, tn=1024,
                   gain=da_q_norm[i], scale=DA_HEAD_DIM ** -0.5 * LOG2E)
        kn = _proj(h, w_in, i, qk_w, qk_w, out_dtype=BF16, tm=1024, tn=1024, gain=da_k_norm[i])
        vt = _proj_t(h, w_in, i, c_v, v_w, tm=DA_TQ, tn=1024)
        rest = _proj(h, w_in, i, c_rest, rest_w, out_dtype=BF16, tm=1024, tn=1024)
        w_lr = jnp.zeros((1, d, LANES), F32).at[0, :, :GLA_GATE_RANK].set(
            w_in[i, :, c_rest + rest_w:].astype(F32))
        glr = _proj(h, w_lr, 0, 0, LANES, out_dtype=F32, tm=512, tn=LANES)

        lam_vecs = jnp.stack([da_lambda_q1[i], da_lambda_k1[i], da_lambda_q2[i], da_lambda_k2[i]])
        y_da = _diff_attention(qn, kn, vt, lam_vecs, da_subln[i], bsz, s_len, lambda_init, DA_TQ)

        w2p = jnp.zeros((LANES, gk_w), F32).at[:GLA_GATE_RANK].set(gla_gate_w2[i].astype(F32))
        y_gla = _gla(rest, (0, gk_w, 2 * gk_w, 2 * gk_w + gv_w), glr, w2p,
                     gla_gate_b[i], gla_out_norm[i], bsz, s_len)

        mixed = _merge(h, y_da, y_gla, w_merge_gate[i].astype(BF16), b_merge_gate[i].reshape(1, 2 * d).astype(F32),
                       w_branch_da[i].astype(BF16), w_branch_gla[i].astype(BF16))
        x2d = _out_proj(x2d, mixed, w_out[i].astype(BF16))

        gates, pair_row, ys = _moe(x2d, norm_ffn[i], router_w[i], router_b[i], w_gate_up[i], b_gate_up[i],
                                   w_down[i], b_down[i])
        x2d = _combine_ple(x2d, gates, pair_row, ys, p[i].reshape(t, p.shape[-1]), norm_ple[i],
                           w_ple_gate[i].astype(BF16), w_ple_up[i].astype(BF16), norm_ple_post[i])
    return x2d.reshape(bsz, s_len, d)
```

```python
import functools
import math

import jax
import jax.numpy as jnp
from jax import lax
from jax.experimental import pallas as pl
from jax.experimental.pallas import tpu as pltpu

F32 = jnp.float32
BF16 = jnp.bfloat16
HIGHEST = lax.Precision.HIGHEST

CHUNK = 64
CHUNK_SHIFT = CHUNK.bit_length() - 1
RMS_EPS = 1e-6
DA_HEADS = 8
DA_HEAD_DIM = 128
DA_V_DIM = 2 * DA_HEAD_DIM
GLA_HEADS = 4
GLA_GATE_RANK = 16
GLA_TAU = 16.0
N_EXPERTS = 32
TOP_K = 4
SWIGLU_LIMIT = 7.0
SWIGLU_ALPHA = 1.702

DA_TQ = 1024
DA_HPS = 2
DA_ONES_ROWS = 16
LOG2E = 1.4426950408889634

LANES = 128
NEG_BIG = -1e30

MIB = 1024 * 1024


def _cparams(sem, vmem_mib, flags=None):
    return pltpu.CompilerParams(dimension_semantics=sem, vmem_limit_bytes=vmem_mib * MIB, flags=flags)


def _dot(a, b):
    return jnp.dot(a, b, preferred_element_type=F32)


def _dot_nt(a, b):
    return lax.dot_general(a, b, (((1,), (1,)), ((), ())), preferred_element_type=F32)


def _dot_tn(a, b):
    return lax.dot_general(a, b, (((0,), (0,)), ((), ())), preferred_element_type=F32)


def _rmsnorm_kernel(x_ref, g_ref, o_ref):
    x = x_ref[...]
    ms = jnp.mean(x * x, axis=-1, keepdims=True)
    o_ref[...] = (x * lax.rsqrt(ms + RMS_EPS) * g_ref[...]).astype(o_ref.dtype)


def _rmsnorm(x2d, gain, tm=1024):
    t, d = x2d.shape
    return pl.pallas_call(
        _rmsnorm_kernel,
        grid=(t // tm,),
        in_specs=[pl.BlockSpec((tm, d), lambda i: (i, 0)), pl.BlockSpec((1, d), lambda i: (0, 0))],
        out_specs=pl.BlockSpec((tm, d), lambda i: (i, 0)),
        out_shape=jax.ShapeDtypeStruct((t, d), BF16),
        compiler_params=_cparams(("parallel",), 32),
        name="rmsnorm",
    )(x2d, gain.reshape(1, d).astype(F32))


def _proj_kernel(a_ref, w_ref, g_ref, o_ref, wb_ref, *, group_norm, scale):
    @pl.when(pl.program_id(1) == 0)
    def _():
        wb_ref[...] = w_ref[...].astype(BF16)

    acc = _dot(a_ref[...], wb_ref[...])
    if group_norm:
        for c in range(acc.shape[1] // DA_HEAD_DIM):
            blk = acc[:, c * DA_HEAD_DIM:(c + 1) * DA_HEAD_DIM]
            ms = jnp.mean(blk * blk, axis=-1, keepdims=True)
            y = blk * lax.rsqrt(ms + RMS_EPS) * g_ref[...] * scale
            o_ref[:, c * DA_HEAD_DIM:(c + 1) * DA_HEAD_DIM] = y.astype(o_ref.dtype)
    else:
        o_ref[...] = acc.astype(o_ref.dtype)


def _proj(a, w, layer, col0, n, *, out_dtype, tm, tn, gain=None, scale=1.0):
    t, k = a.shape
    assert col0 % tn == 0 and n % tn == 0
    cb = col0 // tn
    group_norm = gain is not None
    g = (gain if group_norm else jnp.ones((DA_HEAD_DIM,), F32)).reshape(1, DA_HEAD_DIM).astype(F32)
    return pl.pallas_call(
        functools.partial(_proj_kernel, group_norm=group_norm, scale=scale),
        grid=(n // tn, t // tm),
        in_specs=[
            pl.BlockSpec((tm, k), lambda j, i: (i, 0)),
            pl.BlockSpec((None, k, tn), lambda j, i: (layer, 0, cb + j)),
            pl.BlockSpec((1, DA_HEAD_DIM), lambda j, i: (0, 0)),
        ],
        out_specs=pl.BlockSpec((tm, tn), lambda j, i: (i, j)),
        out_shape=jax.ShapeDtypeStruct((t, n), out_dtype),
        scratch_shapes=[pltpu.VMEM((k, tn), BF16)],
        compiler_params=_cparams(("parallel", "arbitrary"), 48),
        name="proj",
    )(a, w.astype(F32), g)


def _proj_t_kernel(w_ref, a_ref, o_ref, wt_ref):
    @pl.when(pl.program_id(1) == 0)
    def _():
        wt_ref[...] = w_ref[...].T.astype(BF16)

    o_ref[0] = _dot_nt(wt_ref[...], a_ref[...]).astype(o_ref.dtype)


def _proj_t(a, w, layer, col0, n, *, tm, tn):
    t, k = a.shape
    assert col0 % tn == 0 and n % tn == 0
    cb = col0 // tn
    return pl.pallas_call(
        _proj_t_kernel,
        grid=(n // tn, t // tm),
        in_specs=[
            pl.BlockSpec((None, k, tn), lambda j, i: (layer, 0, cb + j)),
            pl.BlockSpec((tm, k), lambda j, i: (i, 0)),
        ],
        out_specs=pl.BlockSpec((1, tn, tm), lambda j, i: (i, j, 0)),
        out_shape=jax.ShapeDtypeStruct((t // tm, n, tm), BF16),
        scratch_shapes=[pltpu.VMEM((tn, k), BF16)],
        compiler_params=_cparams(("parallel", "arbitrary"), 48),
        name="proj_t",
    )(w.astype(F32), a)


def _da_kernel(slope_ref, q_ref, k_ref, vt_ref, lamv_ref, subg_ref, o_ref,
               kaug_ref, qaug_ref, vaug_ref, m_ref, acc_ref, *, tq, s_len, lambda_init):
    hp = pl.program_id(1)
    i = pl.program_id(2)
    hd = DA_HEAD_DIM
    dv = DA_V_DIM
    nq = vt_ref.shape[0]
    chains = [(hh, c) for hh in range(DA_HPS) for c in range(2)]
    slopes2 = [slope_ref[hp * DA_HPS + hh] * LOG2E for hh in range(DA_HPS)]

    @pl.when(i == 0)
    def _():
        pos = lax.broadcasted_iota(jnp.int32, (s_len, hd), 0)
        lane = lax.broadcasted_iota(jnp.int32, (s_len, hd), 1)
        piece = jnp.where(lane < 6, jnp.where((lane & 1) == 0, (pos >> 4) << 4, pos & 15), 0)
        piece = piece.astype(F32).astype(BF16)
        for hh, c in chains:
            col0 = hh * dv + c * hd
            kaug_ref[hh * 2 + c, :, :hd] = k_ref[:, col0:col0 + hd]
            kaug_ref[hh * 2 + c, :, hd:] = piece
        for hh in range(DA_HPS):
            for j in range(nq):
                vaug_ref[hh, j, :dv, :] = vt_ref[j, hh * dv:(hh + 1) * dv, :]
                vaug_ref[hh, j, dv:, :] = jnp.ones((vaug_ref.shape[2] - dv, tq), BF16)

    lane_q = lax.broadcasted_iota(jnp.int32, (tq, hd), 1)
    for hh in range(DA_HPS):
        s_full = jnp.full((tq, hd), slopes2[hh], F32)
        s1 = s_full.astype(BF16).astype(F32)
        r1 = s_full - s1
        s2 = r1.astype(BF16).astype(F32)
        s3 = r1 - s2
        slope_cols = jnp.where(lane_q < 2, s1, jnp.where(lane_q < 4, s2, jnp.where(lane_q < 6, s3, 0.0)))
        slope_cols = slope_cols.astype(BF16)
        for c in range(2):
            col0 = hh * dv + c * hd
            qaug_ref[hh * 2 + c, :, :hd] = q_ref[:, col0:col0 + hd]
            qaug_ref[hh * 2 + c, :, hd:] = slope_cols

    m_ref[...] = jnp.full(m_ref.shape, NEG_BIG, F32)
    acc_ref[...] = jnp.zeros(acc_ref.shape, F32)

    def update(ci, st, va, qs=slice(None)):
        m_old = m_ref[ci, :, qs]
        m_new = jnp.maximum(m_old, jnp.max(st, axis=0, keepdims=True))
        alpha = jnp.exp2(m_old - m_new)
        p = jnp.exp2(st - m_new)
        acc_ref[ci, :, qs] = alpha * acc_ref[ci, :, qs] + _dot(va, p.astype(BF16))
        m_ref[ci, :, qs] = m_new

    def past_tile(j, carry):
        r0 = pl.multiple_of(j * tq, tq)
        for hh, c in chains:
            ci = hh * 2 + c
            update(ci, _dot_nt(kaug_ref[ci, pl.ds(r0, tq), :], qaug_ref[ci]), vaug_ref[hh, j])
        return carry

    lax.fori_loop(0, i, past_tile, 0)

    hk = tq // 2
    r0 = pl.multiple_of(i * tq, tq)

    def diag_part(k0, q0):
        nq_part = tq - q0
        krow = k0 + lax.broadcasted_iota(jnp.int32, (hk, nq_part), 0)
        qcol = q0 + lax.broadcasted_iota(jnp.int32, (hk, nq_part), 1)
        ahead = jnp.maximum(krow - qcol, 0).astype(F32)
        allowed = (krow >> CHUNK_SHIFT) <= (qcol >> CHUNK_SHIFT)
        qs = slice(q0, tq)
        for hh, c in chains:
            ci = hh * 2 + c
            st = _dot_nt(kaug_ref[ci, pl.ds(pl.multiple_of(r0 + k0, hk), hk), :], qaug_ref[ci, qs, :])
            st = jnp.where(allowed, st + (-2.0 * slopes2[hh]) * ahead, NEG_BIG)
            update(ci, st, vaug_ref[hh, i, :, k0:k0 + hk], qs)

    diag_part(0, 0)
    diag_part(hk, hk)

    lamv = lamv_ref[...]
    lam = (jnp.exp(jnp.sum(lamv[0:1] * lamv[1:2], axis=-1, keepdims=True))
           - jnp.exp(jnp.sum(lamv[2:3] * lamv[3:4], axis=-1, keepdims=True)) + lambda_init)
    for hh in range(DA_HPS):
        a0 = acc_ref[hh * 2]
        a1 = acc_ref[hh * 2 + 1]
        ot = a0[:dv, :] / a0[dv:dv + 1, :] - lam * (a1[:dv, :] / a1[dv:dv + 1, :])
        ms = jnp.mean(ot * ot, axis=0, keepdims=True)
        o = (ot * lax.rsqrt(ms + RMS_EPS)).T
        o_ref[:, hh * dv:(hh + 1) * dv] = (o * subg_ref[...] * (1.0 - lambda_init)).astype(o_ref.dtype)


def _diff_attention(qn, kn, vt, lam_vecs, subln_g, bsz, s_len, lambda_init, tq):
    assert s_len <= 4096 and s_len % tq == 0 and tq % CHUNK == 0 and DA_HEADS % DA_HPS == 0
    t = bsz * s_len
    nq = s_len // tq
    w = DA_HPS * DA_V_DIM
    rows_aug = DA_V_DIM + DA_ONES_ROWS
    slopes = jnp.exp2(-8.0 * jnp.arange(1, DA_HEADS + 1, dtype=F32) / DA_HEADS)
    grid_spec = pltpu.PrefetchScalarGridSpec(
        num_scalar_prefetch=1,
        grid=(bsz, DA_HEADS // DA_HPS, nq),
        in_specs=[
            pl.BlockSpec((tq, w), lambda b, h, i, s: (b * nq + i, h)),
            pl.BlockSpec((s_len, w), lambda b, h, i, s: (b, h)),
            pl.BlockSpec((nq, w, tq), lambda b, h, i, s: (b, h, 0)),
            pl.BlockSpec((4, DA_HEAD_DIM), lambda b, h, i, s: (0, 0)),
            pl.BlockSpec((1, DA_V_DIM), lambda b, h, i, s: (0, 0)),
        ],
        out_specs=pl.BlockSpec((tq, w), lambda b, h, i, s: (b * nq + i, h)),
        scratch_shapes=[
            pltpu.VMEM((2 * DA_HPS, s_len, 2 * DA_HEAD_DIM), BF16),
            pltpu.VMEM((2 * DA_HPS, tq, 2 * DA_HEAD_DIM), BF16),
            pltpu.VMEM((DA_HPS, nq, rows_aug, tq), BF16),
            pltpu.VMEM((2 * DA_HPS, 1, tq), F32),
            pltpu.VMEM((2 * DA_HPS, rows_aug, tq), F32),
        ],
    )
    return pl.pallas_call(
        functools.partial(_da_kernel, tq=tq, s_len=s_len, lambda_init=lambda_init),
        grid_spec=grid_spec,
        out_shape=jax.ShapeDtypeStruct((t, DA_HEADS * DA_V_DIM), BF16),
        compiler_params=_cparams(("parallel", "parallel", "arbitrary"), 56),
        name="diff_attention",
    )(slopes, qn, kn, vt, lam_vecs.astype(F32), subln_g.reshape(1, DA_V_DIM).astype(F32))


def _gla_kernel(q_ref, k_ref, v_ref, g_ref, glr_ref, w2_ref, gb_ref, og_ref, o_ref, state_ref,
                *, tb, dk, dv):
    @pl.when(pl.program_id(0) == 0)
    def _():
        state_ref[...] = jnp.zeros(state_ref.shape, F32)

    bsz = q_ref.shape[0]
    row = lax.broadcasted_iota(jnp.int32, (CHUNK, CHUNK), 0)
    col = lax.broadcasted_iota(jnp.int32, (CHUNK, CHUNK), 1)
    lower = row >= col
    tri = lower.astype(F32)
    mid = CHUNK // 2

    def chunk(c, carry):
        r0 = pl.multiple_of(c * CHUNK, CHUNK)
        rows = pl.ds(r0, CHUNK)
        for bi in range(bsz):
            z = jnp.dot(glr_ref[bi, rows, :], w2_ref[...], precision=HIGHEST,
                        preferred_element_type=F32) + gb_ref[...]
            log_a = (jnp.minimum(z, 0.0) - jnp.log1p(jnp.exp(-jnp.abs(z)))) * (1.0 / GLA_TAU)
            b_all = jnp.dot(tri, log_a, precision=HIGHEST, preferred_element_type=F32)
            for h in range(GLA_HEADS):
                ks = slice(h * dk, (h + 1) * dk)
                vs = slice(h * dv, (h + 1) * dv)
                b = b_all[:, ks]
                b_last = b[CHUNK - 1:CHUNK, :]
                b_mid = b[mid:mid + 1, :]
                q = q_ref[bi, rows, ks].astype(F32) * (dk ** -0.5)
                k = k_ref[bi, rows, ks].astype(F32)
                v = v_ref[bi, rows, vs]
                e_fwd = jnp.exp(b - b_mid)
                e_bwd = jnp.exp(b_mid - b)
                a_lo = _dot_nt((q * e_fwd).astype(BF16), (k * e_bwd).astype(BF16))
                a_up = _dot_nt((q * e_bwd).astype(BF16), (k * e_fwd).astype(BF16))
                attn = jnp.where(lower, a_lo, a_up)
                state = state_ref[bi, h]
                o = _dot(attn.astype(BF16), v) + _dot_nt((q * jnp.exp(b)).astype(BF16), state.astype(BF16))
                kd = (k * jnp.exp(b_last - b)).astype(BF16)
                state_ref[bi, h] = state * jnp.exp(b_last) + _dot_tn(v, kd)
                ms = jnp.mean(o * o, axis=-1, keepdims=True)
                g = g_ref[bi, rows, vs].astype(F32)
                y = o * lax.rsqrt(ms + RMS_EPS) * og_ref[...] * (g * jax.nn.sigmoid(g))
                o_ref[bi, rows, vs] = y.astype(o_ref.dtype)
        return carry

    lax.fori_loop(0, tb // CHUNK, chunk, 0)


def _gla(src, cols, glr, w2p, gate_b, out_g, bsz, s_len, tb=512):
    t = bsz * s_len
    nb = s_len // tb
    kw = w2p.shape[1]
    dk = kw // GLA_HEADS
    dv = out_g.shape[0]
    vw = dv * GLA_HEADS
    cq, ck, cv, cg = cols
    src3 = src.reshape(bsz, s_len, src.shape[1])
    out = pl.pallas_call(
        functools.partial(_gla_kernel, tb=tb, dk=dk, dv=dv),
        grid=(nb,),
        in_specs=[
            pl.BlockSpec((bsz, tb, kw), lambda i: (0, i, cq // kw)),
            pl.BlockSpec((bsz, tb, kw), lambda i: (0, i, ck // kw)),
            pl.BlockSpec((bsz, tb, vw), lambda i: (0, i, cv // vw)),
            pl.BlockSpec((bsz, tb, vw), lambda i: (0, i, cg // vw)),
            pl.BlockSpec((bsz, tb, LANES), lambda i: (0, i, 0)),
            pl.BlockSpec((LANES, kw), lambda i: (0, 0)),
            pl.BlockSpec((1, kw), lambda i: (0, 0)),
            pl.BlockSpec((1, dv), lambda i: (0, 0)),
        ],
        out_specs=pl.BlockSpec((bsz, tb, vw), lambda i: (0, i, 0)),
        out_shape=jax.ShapeDtypeStruct((bsz, s_len, vw), BF16),
        scratch_shapes=[pltpu.VMEM((bsz, GLA_HEADS, dv, dk), F32)],
        compiler_params=_cparams(("arbitrary",), 56),
        name="gla",
    )(src3, src3, src3, src3, glr.reshape(bsz, s_len, LANES), w2p,
      gate_b.reshape(1, kw).astype(F32), out_g.reshape(1, dv).astype(F32))
    return out.reshape(t, vw)


def _merge_kernel(h_ref, ya_ref, yb_ref, wga_ref, wgb_ref, ba_ref, bb_ref, wa_ref, wb_ref, o_ref):
    h = h_ref[...]
    ga = jax.nn.sigmoid(_dot(h, wga_ref[...]) + ba_ref[...])
    gb = jax.nn.sigmoid(_dot(h, wgb_ref[...]) + bb_ref[...])
    mixed = ga * _dot(ya_ref[...], wa_ref[...]) + gb * _dot(yb_ref[...], wb_ref[...])
    o_ref[...] = mixed.astype(o_ref.dtype)


def _merge(h, y_da, y_gla, w_gate, b_gate, w_da, w_gla, tm=512, tn=512):
    t, d = h.shape
    nb = d // tn
    act = lambda: pl.BlockSpec((tm, d), lambda j, i: (i, 0))
    return pl.pallas_call(
        _merge_kernel,
        grid=(nb, t // tm),
        in_specs=[
            act(), act(), act(),
            pl.BlockSpec((d, tn), lambda j, i: (0, j)),
            pl.BlockSpec((d, tn), lambda j, i: (0, nb + j)),
            pl.BlockSpec((1, tn), lambda j, i: (0, j)),
            pl.BlockSpec((1, tn), lambda j, i: (0, nb + j)),
            pl.BlockSpec((d, tn), lambda j, i: (0, j)),
            pl.BlockSpec((d, tn), lambda j, i: (0, j)),
        ],
        out_specs=pl.BlockSpec((tm, tn), lambda j, i: (i, j)),
        out_shape=jax.ShapeDtypeStruct((t, d), BF16),
        compiler_params=_cparams(("parallel", "parallel"), 48),
        name="merge",
    )(h, y_da, y_gla, w_gate, w_gate, b_gate, b_gate, w_da, w_gla)


def _out_proj_kernel(x_ref, m_ref, w_ref, o_ref):
    o_ref[...] = x_ref[...] + _dot(m_ref[...], w_ref[...])


def _out_proj(x2d, mixed, w_out, tm=512):
    t, d = x2d.shape
    return pl.pallas_call(
        _out_proj_kernel,
        grid=(t // tm,),
        in_specs=[
            pl.BlockSpec((tm, d), lambda i: (i, 0)),
            pl.BlockSpec((tm, d), lambda i: (i, 0)),
            pl.BlockSpec((d, d), lambda i: (0, 0)),
        ],
        out_specs=pl.BlockSpec((tm, d), lambda i: (i, 0)),
        out_shape=jax.ShapeDtypeStruct((t, d), F32),
        compiler_params=_cparams(("parallel",), 48),
        name="out_proj",
    )(x2d, mixed, w_out)


ROW_SUBLANES = 8


def _pack_rows(val, store):
    half = val.shape[1] // 2
    assert half == ROW_SUBLANES * LANES
    lo = lax.bitcast_convert_type(val[:, :half].astype(BF16).astype(F32), jnp.uint32)
    hi = lax.bitcast_convert_type(val[:, half:].astype(BF16).astype(F32), jnp.uint32)
    packed = (lo >> 16) | (hi & jnp.uint32(0xFFFF0000))
    for c in range(ROW_SUBLANES):
        store(c, packed[:, c * LANES:(c + 1) * LANES])


def _unpack_chunk(chunk):
    lo = lax.bitcast_convert_type(chunk << 16, F32)
    hi = lax.bitcast_convert_type(chunk & jnp.uint32(0xFFFF0000), F32)
    return lo, hi


def _router_kernel(x_ref, g_ref, rw_ref, rb_ref, hp_ref, idx_ref, gate_ref):
    x = x_ref[...]
    tm = x.shape[0]
    ms = jnp.mean(x * x, axis=-1, keepdims=True)
    h = x * lax.rsqrt(ms + RMS_EPS) * g_ref[...]

    def store(c, chunk):
        hp_ref[pl.ds(c, tm, stride=ROW_SUBLANES), :] = chunk

    _pack_rows(h, store)

    logits = jnp.dot(h, rw_ref[...], precision=HIGHEST, preferred_element_type=F32) + rb_ref[...]
    lane = lax.broadcasted_iota(jnp.int32, logits.shape, 1)
    vals, idxs = [], []
    for _ in range(TOP_K):
        m = jnp.max(logits, axis=-1, keepdims=True)
        idx = jnp.min(jnp.where(logits == m, lane, LANES), axis=-1, keepdims=True)
        vals.append(m)
        idxs.append(idx)
        logits = jnp.where(lane == idx, -jnp.inf, logits)
    exps = [jnp.exp(v - vals[0]) for v in vals]
    denom = exps[0] + exps[1] + exps[2] + exps[3]
    idx_out = jnp.zeros(lane.shape, jnp.int32)
    gate_out = jnp.zeros(lane.shape, F32)
    for k in range(TOP_K):
        idx_out = jnp.where(lane == k, idxs[k], idx_out)
        gate_out = jnp.where(lane == k, exps[k] / denom, gate_out)
    idx_ref[...] = idx_out
    gate_ref[...] = gate_out


def _router(x2d, gain, router_w, router_b, tm=512):
    t, d = x2d.shape
    e = router_w.shape[1]
    rw = jnp.zeros((d, LANES), F32).at[:, :e].set(router_w.astype(F32))
    rb = jnp.full((1, LANES), NEG_BIG, F32).at[0, :e].set(router_b.astype(F32))
    return pl.pallas_call(
        _router_kernel,
        grid=(t // tm,),
        in_specs=[
            pl.BlockSpec((tm, d), lambda i: (i, 0)),
            pl.BlockSpec((1, d), lambda i: (0, 0)),
            pl.BlockSpec((d, LANES), lambda i: (0, 0)),
            pl.BlockSpec((1, LANES), lambda i: (0, 0)),
        ],
        out_specs=[
            pl.BlockSpec((tm * ROW_SUBLANES, LANES), lambda i: (i, 0)),
            pl.BlockSpec((tm, LANES), lambda i: (i, 0)),
            pl.BlockSpec((tm, LANES), lambda i: (i, 0)),
        ],
        out_shape=[
            jax.ShapeDtypeStruct((t * ROW_SUBLANES, LANES), jnp.uint32),
            jax.ShapeDtypeStruct((t, LANES), jnp.int32),
            jax.ShapeDtypeStruct((t, LANES), F32),
        ],
        compiler_params=_cparams(("parallel",), 32),
        name="router",
    )(x2d, gain.reshape(1, d).astype(F32), rw, rb)


MOE_SUB = 256
MOE_NSUB = 6
MOE_TF = 256
MOE_ISSUE_ROWS = 64


def _expert_kernel(ge_ref, gs_ref, gn_ref, ng_ref, tot_ref,
                   hp_ref, idx_ref, wgu_ref, bgu_ref, wdn_ref, bdn_ref, ys_ref,
                   idx_smem, xu_ref, yst_ref, xb_ref, acc_ref, wgu_b_ref, wdn_b_ref, zero_ref,
                   sem_idx, sem_in, sem_out, *, n_sub_total):
    g = pl.program_id(0)
    f = pl.program_id(1)
    nf = pl.num_programs(1)
    ng = ng_ref[0]
    active = g < ng
    nsub = gn_ref[g]
    start = gs_ref[g]
    half = xb_ref.shape[2] // 2
    tf = wdn_ref.shape[1]
    sub_words = MOE_SUB * ROW_SUBLANES
    sub_shift = MOE_SUB.bit_length() - 1
    assert MOE_SUB == 1 << sub_shift

    def fetch_idx(gg):
        idx_cp = pltpu.make_async_copy(idx_ref.at[pl.ds(gs_ref[gg], MOE_NSUB)], idx_smem, sem_idx)
        idx_cp.start()
        idx_cp.wait()

    blk_rows = min(MOE_ISSUE_ROWS, MOE_SUB)

    def issue_block(blk, n_rows_group):
        base = blk * blk_rows

        @pl.when(base < n_rows_group)
        def _():
            s = base >> sub_shift
            r0 = base & (MOE_SUB - 1)
            for u in range(blk_rows):
                r = r0 + u
                src = hp_ref.at[pl.ds(pl.multiple_of(idx_smem[s, 0, r] * ROW_SUBLANES, ROW_SUBLANES),
                                      ROW_SUBLANES)]
                dst = xu_ref.at[s, pl.ds(pl.multiple_of(r * ROW_SUBLANES, ROW_SUBLANES), ROW_SUBLANES)]
                pltpu.make_async_copy(src, dst, sem_in.at[s]).start(priority=u % 2)

    def issue_blocks(first, n_rows_group):
        def one(blk, c):
            issue_block(blk, n_rows_group)
            return c

        lax.fori_loop(first, lax.div(n_rows_group + blk_rows - 1, blk_rows), one, 0)

    def wait_rows(s):
        pltpu.make_async_copy(hp_ref.at[pl.ds(0, sub_words)], xu_ref.at[s], sem_in.at[s]).wait()

    def out_copy(s, first_sub):
        r0 = pl.multiple_of((first_sub + s) * sub_words, sub_words)
        return pltpu.make_async_copy(yst_ref.at[s], ys_ref.at[pl.ds(r0, sub_words)], sem_out)

    def wait_out(gg):
        def wait(s, c):
            out_copy(s, gs_ref[gg]).wait()
            return c

        lax.fori_loop(0, gn_ref[gg], wait, 0)

    @pl.when(jnp.logical_and(g == 0, f == 0))
    def _():
        zero_ref[...] = jnp.zeros(zero_ref.shape, zero_ref.dtype)
        fetch_idx(0)
        issue_blocks(0, gn_ref[0] * MOE_SUB)

    @pl.when(jnp.logical_and(active, f == 0))
    def _():
        def unpack(s, c):
            wait_rows(s)
            for ch in range(ROW_SUBLANES):
                lo, hi = _unpack_chunk(xu_ref[s, pl.ds(ch, MOE_SUB, stride=ROW_SUBLANES), :])
                xb_ref[s, :, ch * LANES:(ch + 1) * LANES] = lo.astype(BF16)
                xb_ref[s, :, half + ch * LANES:half + (ch + 1) * LANES] = hi.astype(BF16)
            acc_ref[s] = jnp.broadcast_to(bdn_ref[0], acc_ref.shape[1:])
            return c

        lax.fori_loop(0, nsub, unpack, 0)

        @pl.when(g + 1 < ng)
        def _():
            fetch_idx(g + 1)

    @pl.when(active)
    def _():
        nxt = jnp.minimum(g + 1, pl.num_programs(0) - 1)
        next_rows = jnp.where(g + 1 < ng, gn_ref[nxt] * MOE_SUB, 0)

        def issue_next(s):
            issue_block(f * nsub + s, next_rows)

        wgu_b_ref[...] = wgu_ref[0].astype(BF16)
        hl = LANES // 2
        for m in range(tf // LANES):
            first = wdn_ref[0, m * LANES:m * LANES + hl, :]
            second = wdn_ref[0, m * LANES + hl:(m + 1) * LANES, :]
            lo = lax.bitcast_convert_type(first.astype(BF16).astype(F32), jnp.uint32) >> 16
            hi = lax.bitcast_convert_type(second.astype(BF16).astype(F32), jnp.uint32) & jnp.uint32(0xFFFF0000)
            wdn_b_ref[m * LANES:(m + 1) * LANES, :] = pltpu.bitcast(lo | hi, BF16)
        bgu = bgu_ref[0]

        def gate_up(s0, n):
            rows = n * MOE_SUB
            even = (lax.broadcasted_iota(jnp.int32, (rows, LANES), 1) & 1) == 0
            x = xb_ref[pl.ds(s0, n)].reshape(rows, xb_ref.shape[2])
            gu = _dot(x, wgu_b_ref[...]) + bgu
            gates, ups = [], []
            for m in range(tf // LANES):
                a = gu[:, 2 * m * LANES:(2 * m + 1) * LANES]
                b = gu[:, (2 * m + 1) * LANES:(2 * m + 2) * LANES]
                gates.append(jnp.where(even, a, pltpu.roll(b, 1, 1)))
                ups.append(jnp.where(even, pltpu.roll(a, LANES - 1, 1), b))
            gate = jnp.minimum(jnp.concatenate(gates, axis=1), SWIGLU_LIMIT)
            up = jnp.clip(jnp.concatenate(ups, axis=1), -SWIGLU_LIMIT, SWIGLU_LIMIT)
            return ((up + 1.0) * gate * jax.nn.sigmoid(SWIGLU_ALPHA * gate)).astype(BF16)

        def down(s0, n, act):
            contrib = _dot(act, wdn_b_ref[...])
            acc_ref[pl.ds(s0, n)] += contrib.reshape(n, MOE_SUB, acc_ref.shape[2])

        npair = nsub >> 1
        odd = (nsub & 1) == 1

        @pl.when(npair > 0)
        def _():
            def body(j, act_prev):
                issue_next(2 * j)
                issue_next(2 * j + 1)
                act = gate_up(2 * j, 2)
                down(2 * (j - 1), 2, act_prev)
                return act

            issue_next(0)
            issue_next(1)
            act_last = lax.fori_loop(1, npair, body, gate_up(0, 2))

            @pl.when(odd)
            def _():
                issue_next(nsub - 1)
                act_odd = gate_up(nsub - 1, 1)
                down(2 * (npair - 1), 2, act_last)
                down(nsub - 1, 1, act_odd)

            @pl.when(jnp.logical_not(odd))
            def _():
                down(2 * (npair - 1), 2, act_last)

        @pl.when(nsub == 1)
        def _():
            issue_next(0)
            down(0, 1, gate_up(0, 1))

        @pl.when(f == nf - 1)
        def _():
            issue_blocks(nf * nsub, next_rows)

    @pl.when(jnp.logical_and(active, f == nf - 1))
    def _():
        @pl.when(g > 0)
        def _():
            wait_out(g - 1)

        def issue(s, c):
            def store(ch, chunk):
                yst_ref[s, pl.ds(ch, MOE_SUB, stride=ROW_SUBLANES), :] = chunk

            _pack_rows(acc_ref[s], store)
            out_copy(s, start).start()
            return c

        lax.fori_loop(0, nsub, issue, 0)

    @pl.when(jnp.logical_and(g == pl.num_programs(0) - 1, f == nf - 1))
    def _():
        wait_out(ng - 1)

        def fill(s, c):
            r0 = pl.multiple_of(s * sub_words, sub_words)
            cp = pltpu.make_async_copy(zero_ref, ys_ref.at[pl.ds(r0, sub_words)], sem_out)
            cp.start()
            cp.wait()
            return c

        lax.fori_loop(tot_ref[0], n_sub_total, fill, 0)


def _experts(hp, row_tok, grp_expert, grp_start, grp_nsub, n_groups, tot_sub, w_gu, b_gu, w_dn, b_dn):
    n_rows = row_tok.shape[0]
    sub_words = MOE_SUB * ROW_SUBLANES
    idx2 = jnp.concatenate([row_tok.reshape(n_rows // MOE_SUB, 1, MOE_SUB),
                            jnp.zeros((MOE_NSUB, 1, MOE_SUB), jnp.int32)], axis=0)
    e, d, f2 = w_gu.shape
    fdim = f2 // 2
    tf = MOE_TF
    nf = fdim // tf
    n_grp = grp_expert.shape[0]

    def wsel(g, f, ge, gs, gn, ng, tot):
        on = g < ng[0]
        last = jnp.maximum(ng[0] - 1, 0)
        return jnp.where(on, ge[g], ge[last]), jnp.where(on, f, nf - 1)

    def w_gu_map(g, f, *pre):
        ee, ff = wsel(g, f, *pre)
        return (ee, 0, ff)

    def w_dn_map(g, f, *pre):
        ee, ff = wsel(g, f, *pre)
        return (ee, ff, 0)

    def b_dn_map(g, f, *pre):
        ee, _ = wsel(g, f, *pre)
        return (ee, 0, 0)

    grid_spec = pltpu.PrefetchScalarGridSpec(
        num_scalar_prefetch=5,
        grid=(n_grp, nf),
        in_specs=[
            pl.BlockSpec(memory_space=pl.ANY),
            pl.BlockSpec(memory_space=pl.ANY),
            pl.BlockSpec((1, d, 2 * tf), w_gu_map),
            pl.BlockSpec((1, 1, 2 * tf), w_gu_map),
            pl.BlockSpec((1, tf, d), w_dn_map),
            pl.BlockSpec((1, 1, d), b_dn_map),
        ],
        out_specs=pl.BlockSpec(memory_space=pl.ANY),
        scratch_shapes=[
            pltpu.SMEM((MOE_NSUB, 1, MOE_SUB), jnp.int32),
            pltpu.VMEM((MOE_NSUB, sub_words, LANES), jnp.uint32),
            pltpu.VMEM((MOE_NSUB, sub_words, LANES), jnp.uint32),
            pltpu.VMEM((MOE_NSUB, MOE_SUB, d), BF16),
            pltpu.VMEM((MOE_NSUB, MOE_SUB, d), F32),
            pltpu.VMEM((d, 2 * tf), BF16),
            pltpu.VMEM((tf, d), BF16),
            pltpu.VMEM((sub_words, LANES), jnp.uint32),
            pltpu.SemaphoreType.DMA(()),
            pltpu.SemaphoreType.DMA((MOE_NSUB,)),
            pltpu.SemaphoreType.DMA(()),
        ],
    )
    return pl.pallas_call(
        functools.partial(_expert_kernel, n_sub_total=n_rows // MOE_SUB),
        grid_spec=grid_spec,
        out_shape=jax.ShapeDtypeStruct((n_rows * ROW_SUBLANES, LANES), jnp.uint32),
        compiler_params=_cparams(("arbitrary", "arbitrary"), 56),
        name="moe_experts",
    )(grp_expert, grp_start, grp_nsub, n_groups, tot_sub, hp, idx2,
      w_gu, b_gu.reshape(e, 1, f2).astype(F32), w_dn, b_dn.reshape(e, 1, d).astype(F32))


def _combine_ple_kernel(pr_ref, prn_ref, gate_ref, x_ref, ys_ref, ys2_ref, p_ref, gn_ref, wg_ref, wu_ref, gp_ref,
                        o_ref, buf_ref, x2_ref, sem, *, tc):
    i = pl.program_id(0)
    slot = i & 1
    n_pairs = tc * TOP_K
    tile_words = tc * ROW_SUBLANES

    def issue(idx_ref, sl):
        def pair(p2, carry):
            for u in range(2):
                p = 2 * p2 + u
                tok = p >> 2
                pltpu.make_async_copy(
                    ys_ref.at[idx_ref[0, 0, p]],
                    buf_ref.at[sl, p & (TOP_K - 1),
                               pl.ds(pl.multiple_of(tok * ROW_SUBLANES, ROW_SUBLANES), ROW_SUBLANES)],
                    sem.at[sl]).start(priority=u)
            return carry

        lax.fori_loop(0, n_pairs // 2, pair, 0, unroll=4)

    @pl.when(i == 0)
    def _():
        issue(pr_ref, 0)

    @pl.when(i + 1 < pl.num_programs(0))
    def _():
        issue(prn_ref, 1 - slot)

    for k in range(TOP_K):
        pltpu.make_async_copy(ys2_ref.at[pl.ds(0, tile_words)], buf_ref.at[slot, k], sem.at[slot]).wait()

    gates = gate_ref[...]
    half = x_ref.shape[1] // 2
    for ch in range(ROW_SUBLANES):
        lo_cols = slice(ch * LANES, (ch + 1) * LANES)
        hi_cols = slice(half + ch * LANES, half + (ch + 1) * LANES)
        acc_lo = x_ref[:, lo_cols]
        acc_hi = x_ref[:, hi_cols]
        for k in range(TOP_K):
            lo, hi = _unpack_chunk(buf_ref[slot, k, pl.ds(ch, tc, stride=ROW_SUBLANES), :])
            acc_lo = acc_lo + gates[:, k:k + 1] * lo
            acc_hi = acc_hi + gates[:, k:k + 1] * hi
        x2_ref[:, lo_cols] = acc_lo
        x2_ref[:, hi_cols] = acc_hi

    x = x2_ref[...]
    ms = jnp.mean(x * x, axis=-1, keepdims=True)
    h = (x * lax.rsqrt(ms + RMS_EPS) * gn_ref[...]).astype(BF16)
    gate = jax.nn.sigmoid(_dot(h, wg_ref[...]))
    up = _dot(p_ref[...].astype(BF16), wu_ref[...])
    ms_u = jnp.mean(up * up, axis=-1, keepdims=True)
    o_ref[...] = x + up * lax.rsqrt(ms_u + RMS_EPS) * gp_ref[...] * gate


def _combine_ple(x2d, gates, pair_row, ys, p2d, g_norm, w_gate, w_up, g_post, tc=256):
    t, d = x2d.shape
    pd = p2d.shape[1]
    nt = t // tc
    n_rows = ys.shape[0] // ROW_SUBLANES
    pr = pair_row.reshape(nt, 1, tc * TOP_K)
    const = lambda i: (0, 0)
    return pl.pallas_call(
        functools.partial(_combine_ple_kernel, tc=tc),
        grid=(nt,),
        in_specs=[
            pl.BlockSpec((1, 1, tc * TOP_K), lambda i: (i, 0, 0), memory_space=pltpu.SMEM),
            pl.BlockSpec((1, 1, tc * TOP_K), lambda i: (jnp.minimum(i + 1, nt - 1), 0, 0),
                         memory_space=pltpu.SMEM),
            pl.BlockSpec((tc, LANES), lambda i: (i, 0)),
            pl.BlockSpec((tc, d), lambda i: (i, 0)),
            pl.BlockSpec(memory_space=pl.ANY),
            pl.BlockSpec(memory_space=pl.ANY),
            pl.BlockSpec((tc, pd), lambda i: (i, 0)),
            pl.BlockSpec((1, d), const),
            pl.BlockSpec((d, d), const),
            pl.BlockSpec((pd, d), const),
            pl.BlockSpec((1, d), const),
        ],
        out_specs=pl.BlockSpec((tc, d), lambda i: (i, 0)),
        out_shape=jax.ShapeDtypeStruct((t, d), F32),
        scratch_shapes=[
            pltpu.VMEM((2, TOP_K, tc * ROW_SUBLANES, LANES), jnp.uint32),
            pltpu.VMEM((tc, d), F32),
            pltpu.SemaphoreType.DMA((2,)),
        ],
        compiler_params=_cparams(("arbitrary",), 48),
        name="moe_combine_ple",
    )(pr, pr, gates, x2d, ys.reshape(n_rows, ROW_SUBLANES, LANES), ys, p2d,
      g_norm.reshape(1, d).astype(F32), w_gate, w_up, g_post.reshape(1, d).astype(F32))


def _moe(x2d, gain, router_w, router_b, w_gu, b_gu, w_dn, b_dn):
    t, d = x2d.shape
    e = router_w.shape[1]
    hp, idx, gates = _router(x2d, gain, router_w, router_b)

    n_pairs = t * TOP_K
    flat_e = idx[:, :TOP_K].reshape(n_pairs)
    onehot = (flat_e[:, None] == jnp.arange(e, dtype=jnp.int32)[None, :]).astype(jnp.int32)
    csum = jnp.cumsum(onehot, axis=0)
    rank = jnp.sum(onehot * csum, axis=1) - 1
    counts = csum[-1]
    sub_e = (counts + MOE_SUB - 1) // MOE_SUB
    sub_end = jnp.cumsum(sub_e)
    sub_start = sub_end - sub_e
    dest = (jnp.sum(onehot * (sub_start * MOE_SUB)[None, :], axis=1) + rank).astype(jnp.int32)
    n_sub_total = n_pairs // MOE_SUB + e
    n_rows = n_sub_total * MOE_SUB
    row_tok = jnp.zeros((n_rows,), jnp.int32).at[dest].set(jnp.arange(n_pairs, dtype=jnp.int32) // TOP_K)

    grp_e = (sub_e + MOE_NSUB - 1) // MOE_NSUB
    grp_end = jnp.cumsum(grp_e)
    n_grp_max = (n_sub_total + MOE_NSUB - 1) // MOE_NSUB + e
    gidx = jnp.arange(n_grp_max, dtype=jnp.int32)
    g_exp = jnp.minimum(jnp.searchsorted(grp_end, gidx, side="right"), e - 1).astype(jnp.int32)
    g_local = gidx - (grp_end - grp_e)[g_exp]
    g_on = gidx < grp_end[-1]
    g_start = jnp.where(g_on, sub_start[g_exp] + g_local * MOE_NSUB, 0).astype(jnp.int32)
    g_nsub = jnp.where(g_on, jnp.minimum(MOE_NSUB, sub_e[g_exp] - g_local * MOE_NSUB), 0).astype(jnp.int32)

    ys = _experts(hp, row_tok, g_exp, g_start, g_nsub,
                  grp_end[-1].astype(jnp.int32).reshape(1), sub_end[-1].astype(jnp.int32).reshape(1),
                  w_gu, b_gu, w_dn, b_dn)
    return gates, dest, ys


def kernel(x, p, w_in, da_q_norm, da_k_norm, da_lambda_q1, da_lambda_k1, da_lambda_q2, da_lambda_k2, da_subln, gla_gate_w2, gla_gate_b, gla_out_norm, w_branch_da, w_branch_gla, w_merge_gate, b_merge_gate, w_out, norm_mix, norm_ffn, router_w, router_b, w_gate_up, b_gate_up, w_down, b_down, norm_ple, w_ple_gate, w_ple_up, norm_ple_post):
    bsz, s_len, d = x.shape
    t = bsz * s_len
    depth = w_in.shape[0]
    qk_w = DA_HEADS * 2 * DA_HEAD_DIM
    v_w = DA_HEADS * DA_V_DIM
    gk_w = gla_gate_w2.shape[2]
    gv_w = w_branch_gla.shape[1]
    rest_w = 2 * gk_w + 2 * gv_w
    x2d = x.reshape(t, d)
    for i in range(depth):
        lambda_init = 0.8 - 0.6 * math.exp(-0.3 * i)
        h = _rmsnorm(x2d, norm_mix[i])
        w = w_in[i]
        c_v = 2 * qk_w
        c_rest = c_v + v_w
        qn = _proj(h, w_in, i, 0, qk_w, out_dtype=BF16, tm=1024, tn=1024,
                   gain=da_q_norm[i], scale=DA_HEAD_DIM ** -0.5 * LOG2E)
        kn = _proj(h, w_in, i, qk_w, qk_w, out_dtype=BF16, tm=1024, tn=1024, gain=da_k_norm[i])
        vt = _proj_t(h, w_in, i, c_v, v_w, tm=DA_TQ, tn=1024)
        rest = _proj(h, w_in, i, c_rest, rest_w, out_dtype=BF16, tm=1024, tn=1024)
        w_lr = jnp.zeros((1, d, LANES), F32).at[0, :, :GLA_GATE_RANK].set(
            w_in[i, :, c_rest + rest_w:].astype(F32))
        glr = _proj(h, w_lr, 0, 0, LANES, out_dtype=F32, tm=512, tn=LANES)

        lam_vecs = jnp.stack([da_lambda_q1[i], da_lambda_k1[i], da_lambda_q2[i], da_lambda_k2[i]])
        y_da = _diff_attention(qn, kn, vt, lam_vecs, da_subln[i], bsz, s_len, lambda_init, DA_TQ)

        w2p = jnp.zeros((LANES, gk_w), F32).at[:GLA_GATE_RANK].set(gla_gate_w2[i].astype(F32))
        y_gla = _gla(rest, (0, gk_w, 2 * gk_w, 2 * gk_w + gv_w), glr, w2p,
                     gla_gate_b[i], gla_out_norm[i], bsz, s_len)

        mixed = _merge(h, y_da, y_gla, w_merge_gate[i].astype(BF16), b_merge_gate[i].reshape(1, 2 * d).astype(F32),
                       w_branch_da[i].astype(BF16), w_branch_gla[i].astype(BF16))
        x2d = _out_proj(x2d, mixed, w_out[i].astype(BF16))

        gates, pair_row, ys = _moe(x2d, norm_ffn[i], router_w[i], router_b[i], w_gate_up[i], b_gate_up[i],
                                   w_down[i], b_down[i])
        x2d = _combine_ple(x2d, gates, pair_row, ys, p[i].reshape(t, p.shape[-1]), norm_ple[i],
                           w_ple_gate[i].astype(BF16), w_ple_up[i].astype(BF16), norm_ple_post[i])
    return x2d.reshape(bsz, s_len, d)
```
